```python
import jax, jax.numpy as jnp
from jax import lax
import numpy as np

D_MODEL = 2048
BATCH = 4
SEQ = 2048
DEPTH = 2

N_MIXERS = 2
N_A_LAYERS = (DEPTH + 1) // 2
N_B_LAYERS = DEPTH // 2
BRANCH_WIDTH = D_MODEL
SGU_CHUNK = 128
SGU_GROUPS = 16
SGU_GROUP_WIDTH = BRANCH_WIDTH // SGU_GROUPS
MOBA_HEADS = 16
MOBA_HEAD_DIM = BRANCH_WIDTH // MOBA_HEADS
MOBA_BLOCK = 256
MOBA_TOP_K = 3
MOBA_Q_CHUNK = 16
PLE_DIM = 256
LN_EPS = 1e-5
DEEPNORM_ALPHA = (2 * DEPTH) ** 0.25
DEEPNORM_BETA = (8 * DEPTH) ** -0.25

kernel_name = "hybrid_sgu_moba_deepnorm_ple"


def layer_norm(x, g, b):
    xf = x.astype(jnp.float32)
    mu = jnp.mean(xf, axis=-1, keepdims=True)
    var = jnp.mean(jnp.square(xf - mu), axis=-1, keepdims=True)
    y = (xf - mu) * lax.rsqrt(var + LN_EPS)
    return (y * g.astype(jnp.float32) + b.astype(jnp.float32)).astype(x.dtype)


def sgu_mixer(x, w_in, v_g, v_b, w_s, b_s):
    B, S, _ = x.shape
    W = BRANCH_WIDTH
    h = x @ w_in
    uv = jax.nn.gelu(h[..., :2 * W])
    z = h[..., 2 * W:]
    u, v = uv[..., :W], uv[..., W:]
    v = layer_norm(v, v_g, v_b)
    nc = S // SGU_CHUNK
    v = v.reshape(B, nc, SGU_CHUNK, SGU_GROUPS, SGU_GROUP_WIDTH)
    causal = jnp.tril(jnp.ones((SGU_CHUNK, SGU_CHUNK), dtype=bool))
    w_causal = jnp.where(causal[None], w_s, jnp.zeros_like(w_s))
    s = jnp.einsum('gts,bnsgc->bntgc', w_causal, v) + b_s.T[None, None, :, :, None]
    s = s.reshape(B, S, W)
    return u * s * jax.nn.silu(z)


def moba_mixer(x, w_in):
    B, S, _ = x.shape
    H, hd, BLK, QC = MOBA_HEADS, MOBA_HEAD_DIM, MOBA_BLOCK, MOBA_Q_CHUNK
    h = x @ w_in
    q, k, v, z = jnp.split(h, 4, axis=-1)

    def heads(t):
        return t.reshape(B, S, H, hd).transpose(0, 2, 1, 3)

    q = heads(q) * (hd ** -0.5)
    k, v = heads(k), heads(v)
    nb = -(-S // BLK)
    pad = ((0, 0), (0, 0), (0, nb * BLK - S), (0, 0))
    k_blocks = jnp.pad(k, pad).reshape(B, H, nb, BLK, hd)
    v_blocks = jnp.pad(v, pad).reshape(B, H, nb, BLK, hd)
    k_mean = jnp.mean(k_blocks, axis=3)
    n_sel = min(MOBA_TOP_K, nb)
    blk_ids = jnp.arange(nb)
    b_ix = jnp.arange(B)[:, None, None, None]
    h_ix = jnp.arange(H)[None, :, None, None]
    neg_inf = jnp.float32(-jnp.inf)

    def chunk(c):
        q0 = c * QC
        qc = lax.dynamic_slice_in_dim(q, q0, QC, axis=2)
        own = q0 // BLK
        gate = jnp.einsum('bhqd,bhnd->bhqn', qc, k_mean).astype(jnp.float32)
        gate = jnp.where((blk_ids < own)[None, None, None, :], gate, neg_inf)
        _, idx = lax.top_k(gate, n_sel)
        slot_valid = jnp.repeat(jnp.arange(n_sel) < own, BLK)
        k_sel = k_blocks[b_ix, h_ix, idx].reshape(B, H, QC, n_sel * BLK, hd)
        v_sel = v_blocks[b_ix, h_ix, idx].reshape(B, H, QC, n_sel * BLK, hd)
        s_sel = jnp.einsum('bhqd,bhqkd->bhqk', qc, k_sel).astype(jnp.float32)
        s_sel = jnp.where(slot_valid[None, None, None, :], s_sel, neg_inf)
        k_own = lax.dynamic_slice_in_dim(k_blocks, own, 1, axis=2)[:, :, 0]
        v_own = lax.dynamic_slice_in_dim(v_blocks, own, 1, axis=2)[:, :, 0]
        s_own = jnp.einsum('bhqd,bhkd->bhqk', qc, k_own).astype(jnp.float32)
        q_pos = q0 + jnp.arange(QC)
        k_pos = own * BLK + jnp.arange(BLK)
        s_own = jnp.where((k_pos[None, :] <= q_pos[:, None])[None, None], s_own, neg_inf)
        probs = jax.nn.softmax(jnp.concatenate([s_sel, s_own], axis=-1), axis=-1).astype(v.dtype)
        o = jnp.einsum('bhqk,bhqkd->bhqd', probs[..., :n_sel * BLK], v_sel)
        o = o + jnp.einsum('bhqk,bhkd->bhqd', probs[..., n_sel * BLK:], v_own)
        return o

    out = lax.map(chunk, jnp.arange(S // QC))
    out = out.transpose(1, 0, 3, 2, 4).reshape(B, S, H * hd)
    return out * jax.nn.silu(z)


def setup_inputs(seed: int = 0) -> dict:
    key = jax.random.key(seed)
    ks = jax.random.split(key, 14)
    D, W = D_MODEL, BRANCH_WIDTH
    f32 = jnp.float32
    x = jax.random.normal(ks[0], (BATCH, SEQ, D), f32)
    p = jax.random.normal(ks[1], (DEPTH, BATCH, SEQ, PLE_DIM), f32)
    w_in_a = jax.random.normal(ks[2], (N_A_LAYERS, D, 3 * W), f32) * D ** -0.5
    sgu_norm_g = 1.0 + 0.05 * jax.random.normal(ks[3], (N_A_LAYERS, W), f32)
    sgu_norm_b = 0.05 * jax.random.normal(ks[4], (N_A_LAYERS, W), f32)
    w_s = jax.random.normal(ks[5], (N_A_LAYERS, SGU_GROUPS, SGU_CHUNK, SGU_CHUNK), f32) * SGU_CHUNK ** -0.5
    b_s = 1.0 + 0.1 * jax.random.normal(ks[6], (N_A_LAYERS, SGU_GROUPS, SGU_CHUNK), f32)
    w_in_b = jax.random.normal(ks[7], (N_B_LAYERS, D, 4 * W), f32) * D ** -0.5
    w_out = jax.random.normal(ks[8], (DEPTH, W, D), f32) * (W ** -0.5 * DEEPNORM_BETA)
    ln_g = 1.0 + 0.05 * jax.random.normal(ks[9], (DEPTH, D), f32)
    ln_b = 0.05 * jax.random.normal(ks[10], (DEPTH, D), f32)
    w_ple_gate = jax.random.normal(ks[11], (DEPTH, D, D), f32) * D ** -0.5
    w_ple_proj = jax.random.normal(ks[12], (DEPTH, PLE_DIM, D), f32) * PLE_DIM ** -0.5
    return {"x": x, "p": p, "w_in_a": w_in_a, "sgu_norm_g": sgu_norm_g,
            "sgu_norm_b": sgu_norm_b, "w_s": w_s, "b_s": b_s, "w_in_b": w_in_b,
            "w_out": w_out, "ln_g": ln_g, "ln_b": ln_b,
            "w_ple_gate": w_ple_gate, "w_ple_proj": w_ple_proj}


def reference(x, p, w_in_a, sgu_norm_g, sgu_norm_b, w_s, b_s, w_in_b,
              w_out, ln_g, ln_b, w_ple_gate, w_ple_proj):
    for i in range(DEPTH):
        j = i // N_MIXERS
        if i % N_MIXERS == 0:
            y = sgu_mixer(x, w_in_a[j], sgu_norm_g[j], sgu_norm_b[j], w_s[j], b_s[j])
        else:
            y = moba_mixer(x, w_in_b[j])
        y = y @ w_out[i]
        x = layer_norm(DEEPNORM_ALPHA * x + y, ln_g[i], ln_b[i])
        x = x + jax.nn.sigmoid(x @ w_ple_gate[i]) * (p[i] @ w_ple_proj[i])
    return x
```

```python
from functools import partial

import jax
import jax.numpy as jnp
from jax import lax
from jax.experimental import pallas as pl
from jax.experimental.pallas import tpu as pltpu

D_MODEL = 2048
WIDTH = 2048
SGU_CHUNK = 128
SGU_GROUPS = 16
HEADS = 16
HEAD_DIM = 128
MOBA_BLOCK = 256
MOBA_TOP_K = 3
PLE_DIM = 256
LN_EPS = 1e-5
DEPTH = 2
ALPHA = (2 * DEPTH) ** 0.25

BF16 = jnp.bfloat16
F32 = jnp.float32

VMEM_LIMIT = 56 * 1024 * 1024


def _dot(a, b):
    return jnp.dot(a, b, preferred_element_type=F32)


def _dot_nt(a, b):
    return lax.dot_general(a, b, (((1,), (1,)), ((), ())), preferred_element_type=F32)


def _silu(z):
    return z * jax.nn.sigmoid(z)


def _layer_norm(x, g, b):
    mu = jnp.mean(x, axis=-1, keepdims=True)
    xc = x - mu
    var = jnp.mean(xc * xc, axis=-1, keepdims=True)
    return xc * lax.rsqrt(var + LN_EPS) * g + b


def _sgu_in_kernel(x_ref, wu_ref, wv_ref, wz_ref, uz_ref, vg_ref, xb_ref):
    @pl.when(pl.program_id(1) == 0)
    def _():
        xb_ref[...] = x_ref[...].astype(BF16)

    xb = xb_ref[...]
    u = jax.nn.gelu(_dot(xb, wu_ref[...]))
    z = _dot(xb, wz_ref[...])
    uz_ref[...] = (u * _silu(z)).astype(BF16)
    vg_ref[...] = jax.nn.gelu(_dot(xb, wv_ref[...])).astype(BF16)


def _sgu_in(x, w, tm, tn):
    n = x.shape[0]
    nj = WIDTH // tn
    out = jax.ShapeDtypeStruct((n, WIDTH), BF16)
    return pl.pallas_call(
        _sgu_in_kernel,
        out_shape=(out, out),
        grid=(n // tm, nj),
        in_specs=[
            pl.BlockSpec((tm, D_MODEL), lambda i, j: (i, 0)),
            pl.BlockSpec((D_MODEL, tn), lambda i, j: (0, j)),
            pl.BlockSpec((D_MODEL, tn), lambda i, j: (0, j + nj)),
            pl.BlockSpec((D_MODEL, tn), lambda i, j: (0, j + 2 * nj)),
        ],
        out_specs=(
            pl.BlockSpec((tm, tn), lambda i, j: (i, j)),
            pl.BlockSpec((tm, tn), lambda i, j: (i, j)),
        ),
        scratch_shapes=[pltpu.VMEM((tm, D_MODEL), BF16)],
        compiler_params=pltpu.CompilerParams(
            dimension_semantics=("arbitrary", "arbitrary"),
            vmem_limit_bytes=VMEM_LIMIT),
        name="sgu_in",
    )(x, w, w, w)


def _sgu_mix_kernel(uz_ref, vg_ref, ng_ref, nb_ref, ws_ref, bs_ref, y_ref, wc_ref):
    @pl.when(pl.program_id(0) == 0)
    def _():
        row = lax.broadcasted_iota(jnp.int32, (SGU_CHUNK, SGU_CHUNK), 0)
        col = lax.broadcasted_iota(jnp.int32, (SGU_CHUNK, SGU_CHUNK), 1)
        keep = (col <= row)[None]
        wc_ref[...] = jnp.where(keep, ws_ref[...], 0.0).astype(BF16)

    tm = uz_ref.shape[0]
    vn = _layer_norm(vg_ref[...].astype(F32), ng_ref[...], nb_ref[...]).astype(BF16)
    gw = WIDTH // SGU_GROUPS
    for c in range(tm // SGU_CHUNK):
        r0 = c * SGU_CHUNK
        for g in range(SGU_GROUPS):
            c0 = g * gw
            s = _dot(wc_ref[g], vn[r0:r0 + SGU_CHUNK, c0:c0 + gw]) + bs_ref[:, g:g + 1]
            uz = uz_ref[r0:r0 + SGU_CHUNK, c0:c0 + gw].astype(F32)
            y_ref[r0:r0 + SGU_CHUNK, c0:c0 + gw] = (uz * s).astype(BF16)


def _sgu_mix(uz, vg, ng, nb, ws, bs_t, tm):
    n = uz.shape[0]
    row = lambda i: (i, 0)
    fix2 = lambda i: (0, 0)
    return pl.pallas_call(
        _sgu_mix_kernel,
        out_shape=jax.ShapeDtypeStruct((n, WIDTH), BF16),
        grid=(n // tm,),
        in_specs=[
            pl.BlockSpec((tm, WIDTH), row),
            pl.BlockSpec((tm, WIDTH), row),
            pl.BlockSpec((1, WIDTH), fix2),
            pl.BlockSpec((1, WIDTH), fix2),
            pl.BlockSpec((SGU_GROUPS, SGU_CHUNK, SGU_CHUNK), lambda i: (0, 0, 0)),
            pl.BlockSpec((SGU_CHUNK, SGU_GROUPS), fix2),
        ],
        out_specs=pl.BlockSpec((tm, WIDTH), row),
        scratch_shapes=[pltpu.VMEM((SGU_GROUPS, SGU_CHUNK, SGU_CHUNK), BF16)],
        compiler_params=pltpu.CompilerParams(
            dimension_semantics=("arbitrary",), vmem_limit_bytes=VMEM_LIMIT),
        name="sgu_mix",
    )(uz, vg, ng, nb, ws, bs_t)


def _tail_kernel(y_ref, x_ref, p_ref, wo_ref, g_ref, b_ref, wg_ref, wp_ref, o_ref):
    t = ALPHA * x_ref[...] + _dot(y_ref[...], wo_ref[...])
    xn = _layer_norm(t, g_ref[...], b_ref[...])
    gate = jax.nn.sigmoid(_dot(xn.astype(BF16), wg_ref[...]))
    proj = _dot(p_ref[...].astype(BF16), wp_ref[...])
    o_ref[...] = xn + gate * proj


def _tail(y, x, p, wo, g, b, wg, wp, tm):
    n = x.shape[0]
    row = lambda i: (i, 0)
    fix = lambda i: (0, 0)
    once = pl.Buffered(1)
    return pl.pallas_call(
        _tail_kernel,
        out_shape=jax.ShapeDtypeStruct((n, D_MODEL), F32),
        grid=(n // tm,),
        in_specs=[
            pl.BlockSpec((tm, WIDTH), row),
            pl.BlockSpec((tm, D_MODEL), row),
            pl.BlockSpec((tm, PLE_DIM), row),
            pl.BlockSpec((WIDTH, D_MODEL), fix, pipeline_mode=once),
            pl.BlockSpec((1, D_MODEL), fix),
            pl.BlockSpec((1, D_MODEL), fix),
            pl.BlockSpec((D_MODEL, D_MODEL), fix, pipeline_mode=once),
            pl.BlockSpec((PLE_DIM, D_MODEL), fix, pipeline_mode=once),
        ],
        out_specs=pl.BlockSpec((tm, D_MODEL), row),
        compiler_params=pltpu.CompilerParams(
            dimension_semantics=("arbitrary",), vmem_limit_bytes=VMEM_LIMIT),
        name="layer_tail",
    )(y, x, p, wo, g, b, wg, wp)


def _moba_in_kernel(x_ref, wq_ref, wk_ref, wv_ref, wz_ref,
                    q_ref, k_ref, v_ref, zs_ref, xb_ref):
    @pl.when(pl.program_id(1) == 0)
    def _():
        xb_ref[...] = x_ref[...].astype(BF16)

    xb = xb_ref[...]
    q_ref[...] = (_dot(xb, wq_ref[...]) * (HEAD_DIM ** -0.5)).astype(BF16)
    k_ref[...] = _dot(xb, wk_ref[...]).astype(BF16)
    v_ref[...] = _dot(xb, wv_ref[...]).astype(BF16)
    zs_ref[...] = _silu(_dot(xb, wz_ref[...])).astype(BF16)


def _moba_in(x, w, tm, tn):
    n = x.shape[0]
    nj = WIDTH // tn
    out = jax.ShapeDtypeStruct((n, WIDTH), BF16)

    def wspec(k):
        return pl.BlockSpec((D_MODEL, tn), lambda i, j: (0, j + k * nj))

    ospec = pl.BlockSpec((tm, tn), lambda i, j: (i, j))
    return pl.pallas_call(
        _moba_in_kernel,
        out_shape=(out, out, out, out),
        grid=(n // tm, nj),
        in_specs=[pl.BlockSpec((tm, D_MODEL), lambda i, j: (i, 0)),
                  wspec(0), wspec(1), wspec(2), wspec(3)],
        out_specs=(ospec, ospec, ospec, ospec),
        scratch_shapes=[pltpu.VMEM((tm, D_MODEL), BF16)],
        compiler_params=pltpu.CompilerParams(
            dimension_semantics=("arbitrary", "arbitrary"),
            vmem_limit_bytes=VMEM_LIMIT),
        name="moba_in",
    )(x, w, w, w, w)


def _moba_attn_kernel(q_ref, k_ref, v_ref, zs_ref, y_ref):
    seq = k_ref.shape[0]
    nb = seq // MOBA_BLOCK
    blk = MOBA_BLOCK
    neg_inf = jnp.float32(-jnp.inf)

    kf = k_ref[...].astype(F32)
    k_mean = jnp.mean(kf.reshape(nb, blk, HEAD_DIM), axis=1)

    row = lax.broadcasted_iota(jnp.int32, (blk, blk), 0)
    col = lax.broadcasted_iota(jnp.int32, (blk, blk), 1)
    causal = col <= row

    for own in range(nb):
        r0 = own * blk
        q = q_ref[r0:r0 + blk, :]
        n_keys = (own + 1) * blk
        s = _dot_nt(q, k_ref[0:n_keys, :])

        cnt = None
        if own > MOBA_TOP_K:
            gate = _dot_nt(q.astype(F32), k_mean[0:own, :])
            lane = lax.broadcasted_iota(jnp.int32, (blk, own), 1)
            cnt = jnp.zeros((blk, own), jnp.int32)
            for m in range(own):
                gm = gate[:, m:m + 1]
                beats = (gm > gate) | ((gm == gate) & (m < lane))
                cnt = cnt + jnp.where(beats, 1, 0)

        parts = []
        for n in range(own + 1):
            sn = s[:, n * blk:(n + 1) * blk]
            if n == own:
                sn = jnp.where(causal, sn, neg_inf)
            elif cnt is not None:
                sn = jnp.where(cnt[:, n:n + 1] < MOBA_TOP_K, sn, neg_inf)
            parts.append(sn)

        m_row = jnp.max(parts[0], axis=-1, keepdims=True)
        for sn in parts[1:]:
            m_row = jnp.maximum(m_row, jnp.max(sn, axis=-1, keepdims=True))
        l_row = jnp.zeros((blk, 1), F32)
        o = jnp.zeros((blk, HEAD_DIM), F32)
        for n, sn in enumerate(parts):
            pn = jnp.exp(sn - m_row)
            l_row = l_row + jnp.sum(pn, axis=-1, keepdims=True)
            o = o + _dot(pn.astype(BF16), v_ref[n * blk:(n + 1) * blk, :])
        o = o / l_row
        y_ref[r0:r0 + blk, :] = (o * zs_ref[r0:r0 + blk, :].astype(F32)).astype(BF16)


def _moba_attn(q, k, v, zs, batch, seq):
    spec = pl.BlockSpec((seq, HEAD_DIM), lambda b, h: (b, h))
    return pl.pallas_call(
        _moba_attn_kernel,
        out_shape=jax.ShapeDtypeStruct((batch * seq, WIDTH), BF16),
        grid=(batch, HEADS),
        in_specs=[spec, spec, spec, spec],
        out_specs=spec,
        compiler_params=pltpu.CompilerParams(
            dimension_semantics=("arbitrary", "arbitrary"),
            vmem_limit_bytes=VMEM_LIMIT),
        name="moba_attn",
    )(q, k, v, zs)


def kernel(x, p, w_in_a, sgu_norm_g, sgu_norm_b, w_s, b_s, w_in_b, w_out, ln_g, ln_b,
           w_ple_gate, w_ple_proj):
    batch, seq, d = x.shape
    n = batch * seq
    x2 = x.reshape(n, d)
    p2 = p.reshape(DEPTH, n, PLE_DIM)

    wo = w_out.astype(BF16)
    wg = w_ple_gate.astype(BF16)
    wp = w_ple_proj.astype(BF16)

    uz, vg = _sgu_in(x2, w_in_a[0].astype(BF16), tm=512, tn=512)
    y = _sgu_mix(uz, vg, sgu_norm_g[0][None], sgu_norm_b[0][None], w_s[0], b_s[0].T, tm=256)
    x2 = _tail(y, x2, p2[0], wo[0], ln_g[0][None], ln_b[0][None], wg[0], wp[0], tm=256)

    q, k, v, zs = _moba_in(x2, w_in_b[0].astype(BF16), tm=512, tn=512)
    y = _moba_attn(q, k, v, zs, batch, seq)
    x2 = _tail(y, x2, p2[1], wo[1], ln_g[1][None], ln_b[1][None], wg[1], wp[1], tm=256)

    return x2.reshape(batch, seq, d)
```

```python
from functools import partial

import jax
import jax.numpy as jnp
from jax import lax
from jax.experimental import pallas as pl
from jax.experimental.pallas import tpu as pltpu

D_MODEL = 2048
WIDTH = 2048
SGU_CHUNK = 128
SGU_GROUPS = 16
HEADS = 16
HEAD_DIM = 128
MOBA_BLOCK = 256
MOBA_TOP_K = 3
PLE_DIM = 256
LN_EPS = 1e-5
DEPTH = 2
ALPHA = (2 * DEPTH) ** 0.25
LOG2_E = 1.4426950408889634
ONES_ROWS = 16

BF16 = jnp.bfloat16
F32 = jnp.float32

VMEM_LIMIT = 56 * 1024 * 1024


def _dot(a, b):
    return jnp.dot(a, b, preferred_element_type=F32)


def _dot_nt(a, b):
    return lax.dot_general(a, b, (((1,), (1,)), ((), ())), preferred_element_type=F32)


def _silu(z):
    return z * jax.nn.sigmoid(z)


def _layer_norm(x, g, b):
    mu = jnp.mean(x, axis=-1, keepdims=True)
    xc = x - mu
    var = jnp.mean(xc * xc, axis=-1, keepdims=True)
    return xc * lax.rsqrt(var + LN_EPS) * g + b


def _sgu_in_kernel(x_ref, wu_ref, wv_ref, wz_ref, uz_ref, vg_ref, xb_ref):
    @pl.when(pl.program_id(1) == 0)
    def _():
        xb_ref[...] = x_ref[...].astype(BF16)

    xb = xb_ref[...]
    u = jax.nn.gelu(_dot(xb, wu_ref[...]))
    z = _dot(xb, wz_ref[...])
    uz_ref[...] = (u * _silu(z)).astype(BF16)
    vg_ref[...] = jax.nn.gelu(_dot(xb, wv_ref[...])).astype(BF16)


def _sgu_in(x, w, tm, tn):
    n = x.shape[0]
    nj = WIDTH // tn
    out = jax.ShapeDtypeStruct((n, WIDTH), BF16)
    return pl.pallas_call(
        _sgu_in_kernel,
        out_shape=(out, out),
        grid=(n // tm, nj),
        in_specs=[
            pl.BlockSpec((tm, D_MODEL), lambda i, j: (i, 0)),
            pl.BlockSpec((D_MODEL, tn), lambda i, j: (0, j)),
            pl.BlockSpec((D_MODEL, tn), lambda i, j: (0, j + nj)),
            pl.BlockSpec((D_MODEL, tn), lambda i, j: (0, j + 2 * nj)),
        ],
        out_specs=(
            pl.BlockSpec((tm, tn), lambda i, j: (i, j)),
            pl.BlockSpec((tm, tn), lambda i, j: (i, j)),
        ),
        scratch_shapes=[pltpu.VMEM((tm, D_MODEL), BF16)],
        compiler_params=pltpu.CompilerParams(
            dimension_semantics=("arbitrary", "arbitrary"),
            vmem_limit_bytes=VMEM_LIMIT),
        name="sgu_in",
    )(x, w, w, w)


def _sgu_mix_kernel(uz_ref, vg_ref, ng_ref, nb_ref, ws_ref, bs_ref, y_ref, wc_ref):
    @pl.when(pl.program_id(0) == 0)
    def _():
        row = lax.broadcasted_iota(jnp.int32, (SGU_CHUNK, SGU_CHUNK), 0)
        col = lax.broadcasted_iota(jnp.int32, (SGU_CHUNK, SGU_CHUNK), 1)
        keep = (col <= row)[None]
        wc_ref[...] = jnp.where(keep, ws_ref[...], 0.0).astype(BF16)

    tm = uz_ref.shape[0]
    vn = _layer_norm(vg_ref[...].astype(F32), ng_ref[...], nb_ref[...]).astype(BF16)
    gw = WIDTH // SGU_GROUPS
    for c in range(tm // SGU_CHUNK):
        r0 = c * SGU_CHUNK
        for g in range(SGU_GROUPS):
            c0 = g * gw
            s = _dot(wc_ref[g], vn[r0:r0 + SGU_CHUNK, c0:c0 + gw]) + bs_ref[:, g:g + 1]
            uz = uz_ref[r0:r0 + SGU_CHUNK, c0:c0 + gw].astype(F32)
            y_ref[r0:r0 + SGU_CHUNK, c0:c0 + gw] = (uz * s).astype(BF16)


def _sgu_mix(uz, vg, ng, nb, ws, bs_t, tm):
    n = uz.shape[0]
    row = lambda i: (i, 0)
    fix2 = lambda i: (0, 0)
    return pl.pallas_call(
        _sgu_mix_kernel,
        out_shape=jax.ShapeDtypeStruct((n, WIDTH), BF16),
        grid=(n // tm,),
        in_specs=[
            pl.BlockSpec((tm, WIDTH), row),
            pl.BlockSpec((tm, WIDTH), row),
            pl.BlockSpec((1, WIDTH), fix2),
            pl.BlockSpec((1, WIDTH), fix2),
            pl.BlockSpec((SGU_GROUPS, SGU_CHUNK, SGU_CHUNK), lambda i: (0, 0, 0)),
            pl.BlockSpec((SGU_CHUNK, SGU_GROUPS), fix2),
        ],
        out_specs=pl.BlockSpec((tm, WIDTH), row),
        scratch_shapes=[pltpu.VMEM((SGU_GROUPS, SGU_CHUNK, SGU_CHUNK), BF16)],
        compiler_params=pltpu.CompilerParams(
            dimension_semantics=("arbitrary",), vmem_limit_bytes=VMEM_LIMIT),
        name="sgu_mix",
    )(uz, vg, ng, nb, ws, bs_t)


def _tail_kernel(y_ref, x_ref, p_ref, wo_ref, g_ref, b_ref, wg_ref, wp_ref, o_ref):
    t = ALPHA * x_ref[...] + _dot(y_ref[...], wo_ref[...])
    xn = _layer_norm(t, g_ref[...], b_ref[...])
    gate = jax.nn.sigmoid(_dot(xn.astype(BF16), wg_ref[...]))
    proj = _dot(p_ref[...].astype(BF16), wp_ref[...])
    o_ref[...] = xn + gate * proj


def _tail(y, x, p, wo, g, b, wg, wp, tm):
    n = x.shape[0]
    row = lambda i: (i, 0)
    fix = lambda i: (0, 0)
    once = pl.Buffered(1)
    return pl.pallas_call(
        _tail_kernel,
        out_shape=jax.ShapeDtypeStruct((n, D_MODEL), F32),
        grid=(n // tm,),
        in_specs=[
            pl.BlockSpec((tm, WIDTH), row),
            pl.BlockSpec((tm, D_MODEL), row),
            pl.BlockSpec((tm, PLE_DIM), row),
            pl.BlockSpec((WIDTH, D_MODEL), fix, pipeline_mode=once),
            pl.BlockSpec((1, D_MODEL), fix),
            pl.BlockSpec((1, D_MODEL), fix),
            pl.BlockSpec((D_MODEL, D_MODEL), fix, pipeline_mode=once),
            pl.BlockSpec((PLE_DIM, D_MODEL), fix, pipeline_mode=once),
        ],
        out_specs=pl.BlockSpec((tm, D_MODEL), row),
        compiler_params=pltpu.CompilerParams(
            dimension_semantics=("arbitrary",), vmem_limit_bytes=VMEM_LIMIT),
        name="layer_tail",
    )(y, x, p, wo, g, b, wg, wp)


def _moba_in_kernel(x_ref, wq_ref, wk_ref, wv_ref, wz_ref,
                    q_ref, k_ref, v_ref, zs_ref, xb_ref):
    @pl.when(pl.program_id(1) == 0)
    def _():
        xb_ref[...] = x_ref[...].astype(BF16)

    xb = xb_ref[...]
    q_ref[...] = (_dot(xb, wq_ref[...]) * (HEAD_DIM ** -0.5 * LOG2_E)).astype(BF16)
    k_ref[...] = _dot(xb, wk_ref[...]).astype(BF16)
    v_ref[...] = _dot(xb, wv_ref[...]).astype(BF16)
    zs_ref[...] = _silu(_dot(xb, wz_ref[...])).astype(BF16)


def _moba_in(x, w, tm, tn):
    n = x.shape[0]
    nj = WIDTH // tn
    out = jax.ShapeDtypeStruct((n, WIDTH), BF16)

    def wspec(k):
        return pl.BlockSpec((D_MODEL, tn), lambda i, j: (0, j + k * nj))

    ospec = pl.BlockSpec((tm, tn), lambda i, j: (i, j))
    return pl.pallas_call(
        _moba_in_kernel,
        out_shape=(out, out, out, out),
        grid=(n // tm, nj),
        in_specs=[pl.BlockSpec((tm, D_MODEL), lambda i, j: (i, 0)),
                  wspec(0), wspec(1), wspec(2), wspec(3)],
        out_specs=(ospec, ospec, ospec, ospec),
        scratch_shapes=[pltpu.VMEM((tm, D_MODEL), BF16)],
        compiler_params=pltpu.CompilerParams(
            dimension_semantics=("arbitrary", "arbitrary"),
            vmem_limit_bytes=VMEM_LIMIT),
        name="moba_in",
    )(x, w, w, w, w)


def _moba_attn_kernel(q_ref, k_ref, v_ref, zs_ref, y_ref, vt_ref, s_ref):
    seq = k_ref.shape[0]
    nb = seq // MOBA_BLOCK
    blk = MOBA_BLOCK
    neg_inf = jnp.float32(-jnp.inf)

    def scores(own):
        q = q_ref[own * blk:(own + 1) * blk, :]
        n_keys = (own + 1) * blk
        s_ref[own % 2, 0:n_keys, :] = _dot_nt(k_ref[0:n_keys, :], q)

    scores(nb - 1)

    for n in range(nb):
        vt_ref[0:HEAD_DIM, n * blk:(n + 1) * blk] = (
            v_ref[n * blk:(n + 1) * blk, :].astype(F32).T.astype(BF16))
    vt_ref[HEAD_DIM:HEAD_DIM + ONES_ROWS, :] = jnp.ones((ONES_ROWS, seq), BF16)

    kf = k_ref[...].astype(F32)
    k_mean = jnp.mean(kf.reshape(nb, blk, HEAD_DIM), axis=1)

    key_i = lax.broadcasted_iota(jnp.int32, (blk, blk), 0)
    qry_i = lax.broadcasted_iota(jnp.int32, (blk, blk), 1)
    causal_t = key_i <= qry_i
    blk_i = lax.broadcasted_iota(jnp.int32, (nb, blk), 0)

    for own in reversed(range(nb)):
        r0 = own * blk
        q = q_ref[r0:r0 + blk, :]
        if own > 0:
            scores(own - 1)
        s_t = s_ref.at[own % 2]

        bias = None
        if own > MOBA_TOP_K:
            gate = _dot_nt(k_mean, q.astype(F32))
            gate = jnp.where(blk_i < own, gate, neg_inf)
            cnt = jnp.zeros((nb, blk), jnp.int32)
            for m in range(own):
                gm = gate[m:m + 1, :]
                beats = (gm > gate) | ((gm == gate) & (m < blk_i))
                cnt = cnt + jnp.where(beats, 1, 0)
            bias = jnp.where((cnt < MOBA_TOP_K) & (blk_i < own), 0.0, neg_inf)

        own_rows = slice(own * blk, (own + 1) * blk)
        s_own = jnp.where(causal_t, s_t[own_rows, :], neg_inf)
        s_t[own_rows, :] = s_own
        m_q = jnp.max(s_own, axis=0, keepdims=True)
        for n in range(own):
            bm = jnp.max(s_t[n * blk:(n + 1) * blk, :], axis=0, keepdims=True)
            if bias is not None:
                bm = bm + bias[n:n + 1, :]
            m_q = jnp.maximum(m_q, bm)

        acc = jnp.zeros((HEAD_DIM + ONES_ROWS, blk), F32)
        for n in range(own + 1):
            shift = m_q
            if bias is not None and n < own:
                shift = m_q - bias[n:n + 1, :]
            pn = jnp.exp2(s_t[n * blk:(n + 1) * blk, :] - shift)
            acc = acc + _dot(vt_ref[:, n * blk:(n + 1) * blk], pn.astype(BF16))
        o = (acc[0:HEAD_DIM, :] / acc[HEAD_DIM:HEAD_DIM + 1, :]).T
        y_ref[r0:r0 + blk, :] = (o * zs_ref[r0:r0 + blk, :].astype(F32)).astype(BF16)


def _moba_attn(q, k, v, zs, batch, seq):
    spec = pl.BlockSpec((seq, HEAD_DIM), lambda b, h: (b, h))
    return pl.pallas_call(
        _moba_attn_kernel,
        out_shape=jax.ShapeDtypeStruct((batch * seq, WIDTH), BF16),
        grid=(batch, HEADS),
        in_specs=[spec, spec, spec, spec],
        out_specs=spec,
        scratch_shapes=[pltpu.VMEM((HEAD_DIM + ONES_ROWS, seq), BF16),
                        pltpu.VMEM((2, seq, MOBA_BLOCK), F32)],
        compiler_params=pltpu.CompilerParams(
            dimension_semantics=("arbitrary", "arbitrary"),
            vmem_limit_bytes=VMEM_LIMIT),
        name="moba_attn",
    )(q, k, v, zs)


def kernel(x, p, w_in_a, sgu_norm_g, sgu_norm_b, w_s, b_s, w_in_b, w_out, ln_g, ln_b,
           w_ple_gate, w_ple_proj):
    batch, seq, d = x.shape
    n = batch * seq
    x2 = x.reshape(n, d)
    p2 = p.reshape(DEPTH, n, PLE_DIM)

    wo = w_out.astype(BF16)
    wg = w_ple_gate.astype(BF16)
    wp = w_ple_proj.astype(BF16)

    uz, vg = _sgu_in(x2, w_in_a[0].astype(BF16), tm=512, tn=512)
    y = _sgu_mix(uz, vg, sgu_norm_g[0][None], sgu_norm_b[0][None], w_s[0], b_s[0].T, tm=256)
    x2 = _tail(y, x2, p2[0], wo[0], ln_g[0][None], ln_b[0][None], wg[0], wp[0], tm=256)

    q, k, v, zs = _moba_in(x2, w_in_b[0].astype(BF16), tm=512, tn=512)
    y = _moba_attn(q, k, v, zs, batch, seq)
    x2 = _tail(y, x2, p2[1], wo[1], ln_g[1][None], ln_b[1][None], wg[1], wp[1], tm=256)

    return x2.reshape(batch, seq, d)
```

```python
from functools import partial

import jax
import jax.numpy as jnp
from jax import lax
from jax.experimental import pallas as pl
from jax.experimental.pallas import tpu as pltpu

D_MODEL = 2048
WIDTH = 2048
SGU_CHUNK = 128
SGU_GROUPS = 16
HEADS = 16
HEAD_DIM = 128
MOBA_BLOCK = 256
MOBA_TOP_K = 3
PLE_DIM = 256
LN_EPS = 1e-5
DEPTH = 2
ALPHA = (2 * DEPTH) ** 0.25
LOG2_E = 1.4426950408889634
MXU_COLS = 256
EPILOGUE_ROWS = 512
TAIL_ROWS = 256
ONES_ROWS = 16

BF16 = jnp.bfloat16
F32 = jnp.float32

VMEM_LIMIT = 56 * 1024 * 1024


def _dot(a, b):
    return jnp.dot(a, b, preferred_element_type=F32)


def _dot_nt(a, b):
    return lax.dot_general(a, b, (((1,), (1,)), ((), ())), preferred_element_type=F32)


def _silu(z):
    return z * jax.nn.sigmoid(z)


def _layer_norm(x, g, b):
    mu = jnp.mean(x, axis=-1, keepdims=True)
    xc = x - mu
    var = jnp.mean(xc * xc, axis=-1, keepdims=True)
    return xc * lax.rsqrt(var + LN_EPS) * g + b


def _sgu_in_kernel(x_ref, wu_ref, wv_ref, wz_ref, uz_ref, vg_ref, xb_ref):
    @pl.when(pl.program_id(1) == 0)
    def _():
        xb_ref[...] = x_ref[...].astype(BF16)

    tm, tn = uz_ref.shape
    for c in range(tn // MXU_COLS):
        cols = slice(c * MXU_COLS, (c + 1) * MXU_COLS)
        for r in range(tm // EPILOGUE_ROWS):
            rows = slice(r * EPILOGUE_ROWS, (r + 1) * EPILOGUE_ROWS)
            xb = xb_ref[rows, :]
            u = jax.nn.gelu(_dot(xb, wu_ref[:, cols]))
            z = _dot(xb, wz_ref[:, cols])
            uz_ref[rows, cols] = (u * _silu(z)).astype(BF16)
            vg_ref[rows, cols] = jax.nn.gelu(_dot(xb, wv_ref[:, cols])).astype(BF16)


def _sgu_in(x, w, tm, tn):
    n = x.shape[0]
    nj = WIDTH // tn
    out = jax.ShapeDtypeStruct((n, WIDTH), BF16)
    return pl.pallas_call(
        _sgu_in_kernel,
        out_shape=(out, out),
        grid=(n // tm, nj),
        in_specs=[
            pl.BlockSpec((tm, D_MODEL), lambda i, j: (i, 0)),
            pl.BlockSpec((D_MODEL, tn), lambda i, j: (0, j)),
            pl.BlockSpec((D_MODEL, tn), lambda i, j: (0, j + nj)),
            pl.BlockSpec((D_MODEL, tn), lambda i, j: (0, j + 2 * nj)),
        ],
        out_specs=(
            pl.BlockSpec((tm, tn), lambda i, j: (i, j)),
            pl.BlockSpec((tm, tn), lambda i, j: (i, j)),
        ),
        scratch_shapes=[pltpu.VMEM((tm, D_MODEL), BF16)],
        compiler_params=pltpu.CompilerParams(
            dimension_semantics=("arbitrary", "arbitrary"),
            vmem_limit_bytes=VMEM_LIMIT),
        name="sgu_in",
    )(x, w, w, w)


def _sgu_mix_kernel(uz_ref, vg_ref, ng_ref, nb_ref, ws_ref, bs_ref, y_ref, wc_ref):
    @pl.when(pl.program_id(0) == 0)
    def _():
        row = lax.broadcasted_iota(jnp.int32, (SGU_CHUNK, SGU_CHUNK), 0)
        col = lax.broadcasted_iota(jnp.int32, (SGU_CHUNK, SGU_CHUNK), 1)
        keep = (col <= row)[None]
        wc_ref[...] = jnp.where(keep, ws_ref[...], 0.0).astype(BF16)

    tm = uz_ref.shape[0]
    vn = _layer_norm(vg_ref[...].astype(F32), ng_ref[...], nb_ref[...]).astype(BF16)
    gw = WIDTH // SGU_GROUPS
    for c in range(tm // SGU_CHUNK):
        r0 = c * SGU_CHUNK
        for g in range(SGU_GROUPS):
            c0 = g * gw
            s = _dot(wc_ref[g], vn[r0:r0 + SGU_CHUNK, c0:c0 + gw]) + bs_ref[:, g:g + 1]
            uz = uz_ref[r0:r0 + SGU_CHUNK, c0:c0 + gw].astype(F32)
            y_ref[r0:r0 + SGU_CHUNK, c0:c0 + gw] = (uz * s).astype(BF16)


def _sgu_mix(uz, vg, ng, nb, ws, bs_t, tm):
    n = uz.shape[0]
    row = lambda i: (i, 0)
    fix2 = lambda i: (0, 0)
    return pl.pallas_call(
        _sgu_mix_kernel,
        out_shape=jax.ShapeDtypeStruct((n, WIDTH), BF16),
        grid=(n // tm,),
        in_specs=[
            pl.BlockSpec((tm, WIDTH), row),
            pl.BlockSpec((tm, WIDTH), row),
            pl.BlockSpec((1, WIDTH), fix2),
            pl.BlockSpec((1, WIDTH), fix2),
            pl.BlockSpec((SGU_GROUPS, SGU_CHUNK, SGU_CHUNK), lambda i: (0, 0, 0)),
            pl.BlockSpec((SGU_CHUNK, SGU_GROUPS), fix2),
        ],
        out_specs=pl.BlockSpec((tm, WIDTH), row),
        scratch_shapes=[pltpu.VMEM((SGU_GROUPS, SGU_CHUNK, SGU_CHUNK), BF16)],
        compiler_params=pltpu.CompilerParams(
            dimension_semantics=("arbitrary",), vmem_limit_bytes=VMEM_LIMIT),
        name="sgu_mix",
    )(uz, vg, ng, nb, ws, bs_t)


def _tail_kernel(y_ref, x_ref, p_ref, wo_ref, g_ref, b_ref, wg_ref, wp_ref, o_ref):
    tm, d = o_ref.shape
    n_pieces = tm // TAIL_ROWS

    def pre_norm(r):
        rows = slice(r * TAIL_ROWS, (r + 1) * TAIL_ROWS)
        return ALPHA * x_ref[rows, :] + _dot(y_ref[rows, :], wo_ref[...])

    t = pre_norm(0)
    for r in range(n_pieces):
        rows = slice(r * TAIL_ROWS, (r + 1) * TAIL_ROWS)
        t_next = pre_norm(r + 1) if r + 1 < n_pieces else None
        xn = _layer_norm(t, g_ref[...], b_ref[...])
        xnb = xn.astype(BF16)
        pb = p_ref[rows, :].astype(BF16)
        for c in range(d // MXU_COLS):
            cols = slice(c * MXU_COLS, (c + 1) * MXU_COLS)
            gate = jax.nn.sigmoid(_dot(xnb, wg_ref[:, cols]))
            proj = _dot(pb, wp_ref[:, cols])
            o_ref[rows, cols] = xn[:, cols] + gate * proj
        t = t_next


def _tail(layer, y, x, p, wo, g, b, wg, wp, tm):
    n = x.shape[0]
    row = lambda i: (i, 0)
    fix = lambda i: (layer, 0, 0)
    once = pl.Buffered(1)
    return pl.pallas_call(
        _tail_kernel,
        out_shape=jax.ShapeDtypeStruct((n, D_MODEL), F32),
        grid=(n // tm,),
        in_specs=[
            pl.BlockSpec((tm, WIDTH), row),
            pl.BlockSpec((tm, D_MODEL), row),
            pl.BlockSpec((None, tm, PLE_DIM), lambda i: (layer, i, 0)),
            pl.BlockSpec((None, WIDTH, D_MODEL), fix, pipeline_mode=once),
            pl.BlockSpec((None, 1, D_MODEL), fix),
            pl.BlockSpec((None, 1, D_MODEL), fix),
            pl.BlockSpec((None, D_MODEL, D_MODEL), fix, pipeline_mode=once),
            pl.BlockSpec((None, PLE_DIM, D_MODEL), fix, pipeline_mode=once),
        ],
        out_specs=pl.BlockSpec((tm, D_MODEL), row),
        compiler_params=pltpu.CompilerParams(
            dimension_semantics=("arbitrary",), vmem_limit_bytes=VMEM_LIMIT),
        name="layer_tail",
    )(y, x, p, wo, g, b, wg, wp)


def _moba_in_kernel(x_ref, wq_ref, wk_ref, wv_ref, wz_ref,
                    q_ref, k_ref, v_ref, zs_ref, xb_ref):
    @pl.when(pl.program_id(1) == 0)
    def _():
        xb_ref[...] = x_ref[...].astype(BF16)

    xb = xb_ref[...]
    for c in range(q_ref.shape[1] // MXU_COLS):
        cols = slice(c * MXU_COLS, (c + 1) * MXU_COLS)
        zs_ref[:, cols] = _silu(_dot(xb, wz_ref[:, cols])).astype(BF16)
        q_ref[:, cols] = (_dot(xb, wq_ref[:, cols]) * (HEAD_DIM ** -0.5 * LOG2_E)).astype(BF16)
        k_ref[:, cols] = _dot(xb, wk_ref[:, cols]).astype(BF16)
        v_ref[:, cols] = _dot(xb, wv_ref[:, cols]).astype(BF16)


def _moba_in(x, w, tm, tn):
    n = x.shape[0]
    nj = WIDTH // tn
    out = jax.ShapeDtypeStruct((n, WIDTH), BF16)

    def wspec(k):
        return pl.BlockSpec((D_MODEL, tn), lambda i, j: (0, j + k * nj))

    ospec = pl.BlockSpec((tm, tn), lambda i, j: (i, j))
    return pl.pallas_call(
        _moba_in_kernel,
        out_shape=(out, out, out, out),
        grid=(n // tm, nj),
        in_specs=[pl.BlockSpec((tm, D_MODEL), lambda i, j: (i, 0)),
                  wspec(0), wspec(1), wspec(2), wspec(3)],
        out_specs=(ospec, ospec, ospec, ospec),
        scratch_shapes=[pltpu.VMEM((tm, D_MODEL), BF16)],
        compiler_params=pltpu.CompilerParams(
            dimension_semantics=("arbitrary", "arbitrary"),
            vmem_limit_bytes=VMEM_LIMIT),
        name="moba_in",
    )(x, w, w, w, w)


def _moba_attn_kernel(q_ref, k_ref, v_ref, zs_ref, y_ref, vt_ref, s_ref):
    seq = k_ref.shape[0]
    nb = seq // MOBA_BLOCK
    blk = MOBA_BLOCK
    neg_inf = jnp.float32(-jnp.inf)

    def scores(own):
        q = q_ref[own * blk:(own + 1) * blk, :]
        n_keys = (own + 1) * blk
        s_ref[own % 2, 0:n_keys, :] = _dot_nt(k_ref[0:n_keys, :], q)

    scores(nb - 1)

    for n in range(nb):
        vt_ref[0:HEAD_DIM, n * blk:(n + 1) * blk] = (
            v_ref[n * blk:(n + 1) * blk, :].astype(F32).T.astype(BF16))
    vt_ref[HEAD_DIM:HEAD_DIM + ONES_ROWS, :] = jnp.ones((ONES_ROWS, seq), BF16)

    kf = k_ref[...].astype(F32)
    k_mean = jnp.mean(kf.reshape(nb, blk, HEAD_DIM), axis=1)

    key_i = lax.broadcasted_iota(jnp.int32, (blk, blk), 0)
    qry_i = lax.broadcasted_iota(jnp.int32, (blk, blk), 1)
    causal_t = key_i <= qry_i
    blk_i = lax.broadcasted_iota(jnp.int32, (nb, blk), 0)

    for own in reversed(range(nb)):
        r0 = own * blk
        q = q_ref[r0:r0 + blk, :]
        if own > 0:
            scores(own - 1)
        s_t = s_ref.at[own % 2]

        bias = None
        if own > MOBA_TOP_K:
            gate = _dot_nt(k_mean, q.astype(F32))
            gate = jnp.where(blk_i < own, gate, neg_inf)
            cnt = jnp.zeros((nb, blk), jnp.int32)
            for m in range(own):
                gm = gate[m:m + 1, :]
                beats = (gm > gate) | ((gm == gate) & (m < blk_i))
                cnt = cnt + jnp.where(beats, 1, 0)
            bias = jnp.where((cnt < MOBA_TOP_K) & (blk_i < own), 0.0, neg_inf)

        own_rows = slice(own * blk, (own + 1) * blk)
        s_own = jnp.where(causal_t, s_t[own_rows, :], neg_inf)
        s_t[own_rows, :] = s_own
        m_q = jnp.max(s_own, axis=0, keepdims=True)
        for n in range(own):
            bm = jnp.max(s_t[n * blk:(n + 1) * blk, :], axis=0, keepdims=True)
            if bias is not None:
                bm = bm + bias[n:n + 1, :]
            m_q = jnp.maximum(m_q, bm)

        acc = jnp.zeros((HEAD_DIM + ONES_ROWS, blk), F32)
        for n in range(own + 1):
            shift = m_q
            if bias is not None and n < own:
                shift = m_q - bias[n:n + 1, :]
            pn = jnp.exp2(s_t[n * blk:(n + 1) * blk, :] - shift)
            acc = acc + _dot(vt_ref[:, n * blk:(n + 1) * blk], pn.astype(BF16))
        o = (acc[0:HEAD_DIM, :] / acc[HEAD_DIM:HEAD_DIM + 1, :]).T
        y_ref[r0:r0 + blk, :] = (o * zs_ref[r0:r0 + blk, :].astype(F32)).astype(BF16)


def _moba_attn(q, k, v, zs, batch, seq):
    spec = pl.BlockSpec((seq, HEAD_DIM), lambda b, h: (b, h))
    return pl.pallas_call(
        _moba_attn_kernel,
        out_shape=jax.ShapeDtypeStruct((batch * seq, WIDTH), BF16),
        grid=(batch, HEADS),
        in_specs=[spec, spec, spec, spec],
        out_specs=spec,
        scratch_shapes=[pltpu.VMEM((HEAD_DIM + ONES_ROWS, seq), BF16),
                        pltpu.VMEM((2, seq, MOBA_BLOCK), F32)],
        compiler_params=pltpu.CompilerParams(
            dimension_semantics=("arbitrary", "arbitrary"),
            vmem_limit_bytes=VMEM_LIMIT),
        name="moba_attn",
    )(q, k, v, zs)


def kernel(x, p, w_in_a, sgu_norm_g, sgu_norm_b, w_s, b_s, w_in_b, w_out, ln_g, ln_b,
           w_ple_gate, w_ple_proj):
    batch, seq, d = x.shape
    n = batch * seq
    x2 = x.reshape(n, d)
    p2 = p.reshape(DEPTH, n, PLE_DIM)

    tail_params = (p2, w_out.astype(BF16), ln_g[:, None, :], ln_b[:, None, :],
                   w_ple_gate.astype(BF16), w_ple_proj.astype(BF16))

    uz, vg = _sgu_in(x2, w_in_a[0].astype(BF16), tm=1024, tn=512)
    y = _sgu_mix(uz, vg, sgu_norm_g[0][None], sgu_norm_b[0][None], w_s[0], b_s[0].T, tm=256)
    x2 = _tail(0, y, x2, *tail_params, tm=512)

    q, k, v, zs = _moba_in(x2, w_in_b[0].astype(BF16), tm=1024, tn=512)
    y = _moba_attn(q, k, v, zs, batch, seq)
    x2 = _tail(1, y, x2, *tail_params, tm=512)

    return x2.reshape(batch, seq, d)
```

```python
from functools import partial

import jax
import jax.numpy as jnp
from jax import lax
from jax.experimental import pallas as pl
from jax.experimental.pallas import tpu as pltpu

D_MODEL = 2048
WIDTH = 2048
SGU_CHUNK = 128
SGU_GROUPS = 16
HEADS = 16
HEAD_DIM = 128
MOBA_BLOCK = 256
MOBA_TOP_K = 3
PLE_DIM = 256
LN_EPS = 1e-5
DEPTH = 2
ALPHA = (2 * DEPTH) ** 0.25
LOG2_E = 1.4426950408889634
MXU_COLS = 256
EPILOGUE_ROWS = 512
TAIL_ROWS = 256
ONES_ROWS = 16

BF16 = jnp.bfloat16
F32 = jnp.float32

VMEM_LIMIT = 56 * 1024 * 1024


def _dot(a, b):
    return jnp.dot(a, b, preferred_element_type=F32)


def _dot_nt(a, b):
    return lax.dot_general(a, b, (((1,), (1,)), ((), ())), preferred_element_type=F32)


def _silu(z):
    return z * jax.nn.sigmoid(z)


def _layer_norm(x, g, b):
    mu = jnp.mean(x, axis=-1, keepdims=True)
    xc = x - mu
    var = jnp.mean(xc * xc, axis=-1, keepdims=True)
    return xc * lax.rsqrt(var + LN_EPS) * g + b


def _sgu_kernel(x_ref, w_ref, ng_ref, nb_ref, ws_ref, bs_ref, y_ref, xb_ref, vg_ref, wc_ref):
    @pl.when(pl.program_id(0) == 0)
    def _():
        row = lax.broadcasted_iota(jnp.int32, (SGU_CHUNK, SGU_CHUNK), 0)
        col = lax.broadcasted_iota(jnp.int32, (SGU_CHUNK, SGU_CHUNK), 1)
        keep = (col <= row)[None]
        wc_ref[...] = jnp.where(keep, ws_ref[...], 0.0).astype(BF16)

    tm = x_ref.shape[0]
    n_pieces = WIDTH // MXU_COLS
    gw = WIDTH // SGU_GROUPS
    xb_ref[...] = x_ref[...].astype(BF16)

    row_sum = jnp.zeros((tm, 1), F32)
    for c in range(n_pieces):
        cols = slice(c * MXU_COLS, (c + 1) * MXU_COLS)
        vc = jax.nn.gelu(_dot(xb_ref[...], w_ref[:, WIDTH + c * MXU_COLS:WIDTH + (c + 1) * MXU_COLS]))
        vg_ref[:, cols] = vc
        row_sum = row_sum + jnp.sum(vc, axis=-1, keepdims=True)
    mu = row_sum * (1.0 / WIDTH)
    sq_sum = jnp.zeros((tm, 1), F32)
    for c in range(n_pieces):
        cols = slice(c * MXU_COLS, (c + 1) * MXU_COLS)
        dc = vg_ref[:, cols] - mu
        sq_sum = sq_sum + jnp.sum(dc * dc, axis=-1, keepdims=True)
    rstd = lax.rsqrt(sq_sum * (1.0 / WIDTH) + LN_EPS)

    def uz_piece(c):
        u = jax.nn.gelu(_dot(xb_ref[...], w_ref[:, c * MXU_COLS:(c + 1) * MXU_COLS]))
        z = _dot(xb_ref[...], w_ref[:, 2 * WIDTH + c * MXU_COLS:2 * WIDTH + (c + 1) * MXU_COLS])
        return u * _silu(z)

    uz_next = uz_piece(0)
    for c in range(n_pieces):
        cols = slice(c * MXU_COLS, (c + 1) * MXU_COLS)
        uz = uz_next
        if c + 1 < n_pieces:
            uz_next = uz_piece(c + 1)
        vn = ((vg_ref[:, cols] - mu) * rstd * ng_ref[:, cols] + nb_ref[:, cols]).astype(BF16)
        for k in range(tm // SGU_CHUNK):
            rows = slice(k * SGU_CHUNK, (k + 1) * SGU_CHUNK)
            for gi in range(MXU_COLS // gw):
                g = c * (MXU_COLS // gw) + gi
                gcols = slice(gi * gw, (gi + 1) * gw)
                s = _dot(wc_ref[g], vn[rows, gcols]) + bs_ref[:, g:g + 1]
                y_ref[rows, c * MXU_COLS + gi * gw:c * MXU_COLS + (gi + 1) * gw] = (
                    uz[rows, gcols] * s).astype(BF16)


def _sgu(x, w, ng, nb, ws, bs_t, tm):
    n = x.shape[0]
    row = lambda i: (i, 0)
    fix2 = lambda i: (0, 0)
    return pl.pallas_call(
        _sgu_kernel,
        out_shape=jax.ShapeDtypeStruct((n, WIDTH), BF16),
        grid=(n // tm,),
        in_specs=[
            pl.BlockSpec((tm, D_MODEL), row),
            pl.BlockSpec((D_MODEL, 3 * WIDTH), fix2, pipeline_mode=pl.Buffered(1)),
            pl.BlockSpec((1, WIDTH), fix2),
            pl.BlockSpec((1, WIDTH), fix2),
            pl.BlockSpec((SGU_GROUPS, SGU_CHUNK, SGU_CHUNK), lambda i: (0, 0, 0)),
            pl.BlockSpec((SGU_CHUNK, SGU_GROUPS), fix2),
        ],
        out_specs=pl.BlockSpec((tm, WIDTH), row),
        scratch_shapes=[pltpu.VMEM((tm, D_MODEL), BF16),
                        pltpu.VMEM((tm, WIDTH), F32),
                        pltpu.VMEM((SGU_GROUPS, SGU_CHUNK, SGU_CHUNK), BF16)],
        compiler_params=pltpu.CompilerParams(
            dimension_semantics=("arbitrary",), vmem_limit_bytes=VMEM_LIMIT),
        name="sgu",
    )(x, w, ng, nb, ws, bs_t)


def _tail_kernel(y_ref, x_ref, p_ref, wo_ref, g_ref, b_ref, wg_ref, wp_ref, o_ref):
    tm, d = o_ref.shape
    n_pieces = tm // TAIL_ROWS

    def pre_norm(r):
        rows = slice(r * TAIL_ROWS, (r + 1) * TAIL_ROWS)
        return ALPHA * x_ref[rows, :] + _dot(y_ref[rows, :], wo_ref[...])

    t = pre_norm(0)
    for r in range(n_pieces):
        rows = slice(r * TAIL_ROWS, (r + 1) * TAIL_ROWS)
        t_next = pre_norm(r + 1) if r + 1 < n_pieces else None
        xn = _layer_norm(t, g_ref[...], b_ref[...])
        xnb = xn.astype(BF16)
        pb = p_ref[rows, :].astype(BF16)
        for c in range(d // MXU_COLS):
            cols = slice(c * MXU_COLS, (c + 1) * MXU_COLS)
            gate = jax.nn.sigmoid(_dot(xnb, wg_ref[:, cols]))
            proj = _dot(pb, wp_ref[:, cols])
            o_ref[rows, cols] = xn[:, cols] + gate * proj
        t = t_next


def _tail(layer, y, x, p, wo, g, b, wg, wp, tm):
    n = x.shape[0]
    row = lambda i: (i, 0)
    fix = lambda i: (layer, 0, 0)
    once = pl.Buffered(1)
    return pl.pallas_call(
        _tail_kernel,
        out_shape=jax.ShapeDtypeStruct((n, D_MODEL), F32),
        grid=(n // tm,),
        in_specs=[
            pl.BlockSpec((tm, WIDTH), row),
            pl.BlockSpec((tm, D_MODEL), row),
            pl.BlockSpec((None, tm, PLE_DIM), lambda i: (layer, i, 0)),
            pl.BlockSpec((None, WIDTH, D_MODEL), fix, pipeline_mode=once),
            pl.BlockSpec((None, 1, D_MODEL), fix),
            pl.BlockSpec((None, 1, D_MODEL), fix),
            pl.BlockSpec((None, D_MODEL, D_MODEL), fix, pipeline_mode=once),
            pl.BlockSpec((None, PLE_DIM, D_MODEL), fix, pipeline_mode=once),
        ],
        out_specs=pl.BlockSpec((tm, D_MODEL), row),
        compiler_params=pltpu.CompilerParams(
            dimension_semantics=("arbitrary",), vmem_limit_bytes=VMEM_LIMIT),
        name="layer_tail",
    )(y, x, p, wo, g, b, wg, wp)


def _moba_in_kernel(x_ref, wq_ref, wk_ref, wv_ref, wz_ref,
                    q_ref, k_ref, v_ref, zs_ref, xb_ref):
    @pl.when(pl.program_id(1) == 0)
    def _():
        xb_ref[...] = x_ref[...].astype(BF16)

    xb = xb_ref[...]
    for c in range(q_ref.shape[1] // MXU_COLS):
        cols = slice(c * MXU_COLS, (c + 1) * MXU_COLS)
        zs_ref[:, cols] = _silu(_dot(xb, wz_ref[:, cols])).astype(BF16)
        q_ref[:, cols] = (_dot(xb, wq_ref[:, cols]) * (HEAD_DIM ** -0.5 * LOG2_E)).astype(BF16)
        k_ref[:, cols] = _dot(xb, wk_ref[:, cols]).astype(BF16)
        v_ref[:, cols] = _dot(xb, wv_ref[:, cols]).astype(BF16)


def _moba_in(x, w, tm, tn):
    n = x.shape[0]
    nj = WIDTH // tn
    out = jax.ShapeDtypeStruct((n, WIDTH), BF16)

    def wspec(k):
        return pl.BlockSpec((D_MODEL, tn), lambda i, j: (0, j + k * nj))

    ospec = pl.BlockSpec((tm, tn), lambda i, j: (i, j))
    return pl.pallas_call(
        _moba_in_kernel,
        out_shape=(out, out, out, out),
        grid=(n // tm, nj),
        in_specs=[pl.BlockSpec((tm, D_MODEL), lambda i, j: (i, 0)),
                  wspec(0), wspec(1), wspec(2), wspec(3)],
        out_specs=(ospec, ospec, ospec, ospec),
        scratch_shapes=[pltpu.VMEM((tm, D_MODEL), BF16)],
        compiler_params=pltpu.CompilerParams(
            dimension_semantics=("arbitrary", "arbitrary"),
            vmem_limit_bytes=VMEM_LIMIT),
        name="moba_in",
    )(x, w, w, w, w)


def _moba_attn_kernel(q_ref, k_ref, v_ref, zs_ref, y_ref, vt_ref, s_ref):
    seq = k_ref.shape[0]
    nb = seq // MOBA_BLOCK
    blk = MOBA_BLOCK
    neg_inf = jnp.float32(-jnp.inf)

    def scores(own):
        q = q_ref[own * blk:(own + 1) * blk, :]
        n_keys = (own + 1) * blk
        s_ref[own % 2, 0:n_keys, :] = _dot_nt(k_ref[0:n_keys, :], q)

    scores(nb - 1)

    for n in range(nb):
        vt_ref[0:HEAD_DIM, n * blk:(n + 1) * blk] = (
            v_ref[n * blk:(n + 1) * blk, :].astype(F32).T.astype(BF16))
    vt_ref[HEAD_DIM:HEAD_DIM + ONES_ROWS, :] = jnp.ones((ONES_ROWS, seq), BF16)

    kf = k_ref[...].astype(F32)
    k_mean = jnp.mean(kf.reshape(nb, blk, HEAD_DIM), axis=1)

    key_i = lax.broadcasted_iota(jnp.int32, (blk, blk), 0)
    qry_i = lax.broadcasted_iota(jnp.int32, (blk, blk), 1)
    causal_t = key_i <= qry_i
    blk_i = lax.broadcasted_iota(jnp.int32, (nb, blk), 0)

    for own in reversed(range(nb)):
        r0 = own * blk
        q = q_ref[r0:r0 + blk, :]
        if own > 0:
            scores(own - 1)
        s_t = s_ref.at[own % 2]

        bias = None
        if own > MOBA_TOP_K:
            gate = _dot_nt(k_mean, q.astype(F32))
            gate = jnp.where(blk_i < own, gate, neg_inf)
            cnt = jnp.zeros((nb, blk), jnp.int32)
            for m in range(own):
                gm = gate[m:m + 1, :]
                beats = (gm > gate) | ((gm == gate) & (m < blk_i))
                cnt = cnt + jnp.where(beats, 1, 0)
            bias = jnp.where((cnt < MOBA_TOP_K) & (blk_i < own), 0.0, neg_inf)

        own_rows = slice(own * blk, (own + 1) * blk)
        s_own = jnp.where(causal_t, s_t[own_rows, :], neg_inf)
        s_t[own_rows, :] = s_own
        m_q = jnp.max(s_own, axis=0, keepdims=True)
        for n in range(own):
            bm = jnp.max(s_t[n * blk:(n + 1) * blk, :], axis=0, keepdims=True)
            if bias is not None:
                bm = bm + bias[n:n + 1, :]
            m_q = jnp.maximum(m_q, bm)

        acc = jnp.zeros((HEAD_DIM + ONES_ROWS, blk), F32)
        for n in range(own + 1):
            shift = m_q
            if bias is not None and n < own:
                shift = m_q - bias[n:n + 1, :]
            pn = jnp.exp2(s_t[n * blk:(n + 1) * blk, :] - shift)
            acc = acc + _dot(vt_ref[:, n * blk:(n + 1) * blk], pn.astype(BF16))
        o = (acc[0:HEAD_DIM, :] / acc[HEAD_DIM:HEAD_DIM + 1, :]).T
        y_ref[r0:r0 + blk, :] = (o * zs_ref[r0:r0 + blk, :].astype(F32)).astype(BF16)


def _moba_attn(q, k, v, zs, batch, seq):
    spec = pl.BlockSpec((seq, HEAD_DIM), lambda b, h: (b, h))
    return pl.pallas_call(
        _moba_attn_kernel,
        out_shape=jax.ShapeDtypeStruct((batch * seq, WIDTH), BF16),
        grid=(batch, HEADS),
        in_specs=[spec, spec, spec, spec],
        out_specs=spec,
        scratch_shapes=[pltpu.VMEM((HEAD_DIM + ONES_ROWS, seq), BF16),
                        pltpu.VMEM((2, seq, MOBA_BLOCK), F32)],
        compiler_params=pltpu.CompilerParams(
            dimension_semantics=("arbitrary", "arbitrary"),
            vmem_limit_bytes=VMEM_LIMIT),
        name="moba_attn",
    )(q, k, v, zs)


def kernel(x, p, w_in_a, sgu_norm_g, sgu_norm_b, w_s, b_s, w_in_b, w_out, ln_g, ln_b,
           w_ple_gate, w_ple_proj):
    batch, seq, d = x.shape
    n = batch * seq
    x2 = x.reshape(n, d)
    p2 = p.reshape(DEPTH, n, PLE_DIM)

    tail_params = (p2, w_out.astype(BF16), ln_g[:, None, :], ln_b[:, None, :],
                   w_ple_gate.astype(BF16), w_ple_proj.astype(BF16))

    y = _sgu(x2, w_in_a[0].astype(BF16), sgu_norm_g[0][None], sgu_norm_b[0][None],
             w_s[0], b_s[0].T, tm=512)
    x2 = _tail(0, y, x2, *tail_params, tm=512)

    q, k, v, zs = _moba_in(x2, w_in_b[0].astype(BF16), tm=1024, tn=512)
    y = _moba_attn(q, k, v, zs, batch, seq)
    x2 = _tail(1, y, x2, *tail_params, tm=512)

    return x2.reshape(batch, seq, d)
```

```python
from functools import partial

import jax
import jax.numpy as jnp
from jax import lax
from jax.experimental import pallas as pl
from jax.experimental.pallas import tpu as pltpu

D_MODEL = 2048
WIDTH = 2048
SGU_CHUNK = 128
SGU_GROUPS = 16
HEADS = 16
HEAD_DIM = 128
MOBA_BLOCK = 256
MOBA_TOP_K = 3
PLE_DIM = 256
LN_EPS = 1e-5
DEPTH = 2
ALPHA = (2 * DEPTH) ** 0.25
LOG2_E = 1.4426950408889634
MXU_COLS = 256
EPILOGUE_ROWS = 512
TAIL_ROWS = 256
ONES_ROWS = 16

BF16 = jnp.bfloat16
F32 = jnp.float32

VMEM_LIMIT = 56 * 1024 * 1024


def _dot(a, b):
    return jnp.dot(a, b, preferred_element_type=F32)


def _dot_nt(a, b):
    return lax.dot_general(a, b, (((1,), (1,)), ((), ())), preferred_element_type=F32)


def _silu(z):
    return z * jax.nn.sigmoid(z)


def _layer_norm(x, g, b):
    mu = jnp.mean(x, axis=-1, keepdims=True)
    xc = x - mu
    var = jnp.mean(xc * xc, axis=-1, keepdims=True)
    return xc * lax.rsqrt(var + LN_EPS) * g + b


def _sgu_kernel(x_ref, w_ref, ng_ref, nb_ref, ws_ref, bs_ref, y_ref, xb_ref, vg_ref, wc_ref):
    @pl.when(pl.program_id(0) == 0)
    def _():
        row = lax.broadcasted_iota(jnp.int32, (SGU_CHUNK, SGU_CHUNK), 0)
        col = lax.broadcasted_iota(jnp.int32, (SGU_CHUNK, SGU_CHUNK), 1)
        keep = (col <= row)[None]
        wc_ref[...] = jnp.where(keep, ws_ref[...], 0.0).astype(BF16)

    tm = x_ref.shape[0]
    n_pieces = WIDTH // MXU_COLS
    gw = WIDTH // SGU_GROUPS
    xb_ref[...] = x_ref[...].astype(BF16)

    row_sum = jnp.zeros((tm, 1), F32)
    for c in range(n_pieces):
        cols = slice(c * MXU_COLS, (c + 1) * MXU_COLS)
        vc = jax.nn.gelu(_dot(xb_ref[...], w_ref[:, WIDTH + c * MXU_COLS:WIDTH + (c + 1) * MXU_COLS]))
        vg_ref[:, cols] = vc
        row_sum = row_sum + jnp.sum(vc, axis=-1, keepdims=True)
    mu = row_sum * (1.0 / WIDTH)
    sq_sum = jnp.zeros((tm, 1), F32)
    for c in range(n_pieces):
        cols = slice(c * MXU_COLS, (c + 1) * MXU_COLS)
        dc = vg_ref[:, cols] - mu
        sq_sum = sq_sum + jnp.sum(dc * dc, axis=-1, keepdims=True)
    rstd = lax.rsqrt(sq_sum * (1.0 / WIDTH) + LN_EPS)

    def uz_piece(c):
        u = jax.nn.gelu(_dot(xb_ref[...], w_ref[:, c * MXU_COLS:(c + 1) * MXU_COLS]))
        z = _dot(xb_ref[...], w_ref[:, 2 * WIDTH + c * MXU_COLS:2 * WIDTH + (c + 1) * MXU_COLS])
        return u * _silu(z)

    uz_next = uz_piece(0)
    for c in range(n_pieces):
        cols = slice(c * MXU_COLS, (c + 1) * MXU_COLS)
        uz = uz_next
        if c + 1 < n_pieces:
            uz_next = uz_piece(c + 1)
        vn = ((vg_ref[:, cols] - mu) * rstd * ng_ref[:, cols] + nb_ref[:, cols]).astype(BF16)
        for k in range(tm // SGU_CHUNK):
            rows = slice(k * SGU_CHUNK, (k + 1) * SGU_CHUNK)
            for gi in range(MXU_COLS // gw):
                g = c * (MXU_COLS // gw) + gi
                gcols = slice(gi * gw, (gi + 1) * gw)
                s = _dot(wc_ref[g], vn[rows, gcols]) + bs_ref[:, g:g + 1]
                y_ref[rows, c * MXU_COLS + gi * gw:c * MXU_COLS + (gi + 1) * gw] = (
                    uz[rows, gcols] * s).astype(BF16)


def _sgu(x, w, ng, nb, ws, bs_t, tm):
    n = x.shape[0]
    row = lambda i: (i, 0)
    fix2 = lambda i: (0, 0)
    return pl.pallas_call(
        _sgu_kernel,
        out_shape=jax.ShapeDtypeStruct((n, WIDTH), BF16),
        grid=(n // tm,),
        in_specs=[
            pl.BlockSpec((tm, D_MODEL), row),
            pl.BlockSpec((D_MODEL, 3 * WIDTH), fix2, pipeline_mode=pl.Buffered(1)),
            pl.BlockSpec((1, WIDTH), fix2),
            pl.BlockSpec((1, WIDTH), fix2),
            pl.BlockSpec((SGU_GROUPS, SGU_CHUNK, SGU_CHUNK), lambda i: (0, 0, 0)),
            pl.BlockSpec((SGU_CHUNK, SGU_GROUPS), fix2),
        ],
        out_specs=pl.BlockSpec((tm, WIDTH), row),
        scratch_shapes=[pltpu.VMEM((tm, D_MODEL), BF16),
                        pltpu.VMEM((tm, WIDTH), F32),
                        pltpu.VMEM((SGU_GROUPS, SGU_CHUNK, SGU_CHUNK), BF16)],
        compiler_params=pltpu.CompilerParams(
            dimension_semantics=("arbitrary",), vmem_limit_bytes=VMEM_LIMIT),
        name="sgu",
    )(x, w, ng, nb, ws, bs_t)


def _tail_kernel(y_ref, x_ref, p_ref, wo_ref, g_ref, b_ref, wg_ref, wp_ref, o_ref):
    tm, d = o_ref.shape
    n_pieces = tm // TAIL_ROWS

    def pre_norm(r):
        rows = slice(r * TAIL_ROWS, (r + 1) * TAIL_ROWS)
        return ALPHA * x_ref[rows, :] + _dot(y_ref[rows, :], wo_ref[...])

    t = pre_norm(0)
    for r in range(n_pieces):
        rows = slice(r * TAIL_ROWS, (r + 1) * TAIL_ROWS)
        t_next = pre_norm(r + 1) if r + 1 < n_pieces else None
        xn = _layer_norm(t, g_ref[...], b_ref[...])
        xnb = xn.astype(BF16)
        pb = p_ref[rows, :].astype(BF16)
        for c in range(d // MXU_COLS):
            cols = slice(c * MXU_COLS, (c + 1) * MXU_COLS)
            gate = jax.nn.sigmoid(_dot(xnb, wg_ref[:, cols]))
            proj = _dot(pb, wp_ref[:, cols])
            o_ref[rows, cols] = xn[:, cols] + gate * proj
        t = t_next


def _tail(layer, y, x, p, wo, g, b, wg, wp, tm):
    n = x.shape[0]
    row = lambda i: (i, 0)
    fix = lambda i: (layer, 0, 0)
    once = pl.Buffered(1)
    return pl.pallas_call(
        _tail_kernel,
        out_shape=jax.ShapeDtypeStruct((n, D_MODEL), F32),
        grid=(n // tm,),
        in_specs=[
            pl.BlockSpec((tm, WIDTH), row),
            pl.BlockSpec((tm, D_MODEL), row),
            pl.BlockSpec((None, tm, PLE_DIM), lambda i: (layer, i, 0)),
            pl.BlockSpec((None, WIDTH, D_MODEL), fix, pipeline_mode=once),
            pl.BlockSpec((None, 1, D_MODEL), fix),
            pl.BlockSpec((None, 1, D_MODEL), fix),
            pl.BlockSpec((None, D_MODEL, D_MODEL), fix, pipeline_mode=once),
            pl.BlockSpec((None, PLE_DIM, D_MODEL), fix, pipeline_mode=once),
        ],
        out_specs=pl.BlockSpec((tm, D_MODEL), row),
        compiler_params=pltpu.CompilerParams(
            dimension_semantics=("arbitrary",), vmem_limit_bytes=VMEM_LIMIT),
        name="layer_tail",
    )(y, x, p, wo, g, b, wg, wp)


def _moba_in_kernel(x_ref, wq_ref, wk_ref, wv_ref, wz_ref,
                    q_ref, k_ref, v_ref, zs_ref, xb_ref):
    @pl.when(pl.program_id(1) == 0)
    def _():
        xb_ref[...] = x_ref[...].astype(BF16)

    xb = xb_ref[...]
    for c in range(q_ref.shape[1] // MXU_COLS):
        cols = slice(c * MXU_COLS, (c + 1) * MXU_COLS)
        zs_ref[:, cols] = _silu(_dot(xb, wz_ref[:, cols])).astype(BF16)
        q_ref[:, cols] = (_dot(xb, wq_ref[:, cols]) * (HEAD_DIM ** -0.5 * LOG2_E)).astype(BF16)
        k_ref[:, cols] = _dot(xb, wk_ref[:, cols]).astype(BF16)
        v_ref[:, cols] = _dot(xb, wv_ref[:, cols]).astype(BF16)


def _moba_in(x, w, tm, tn):
    n = x.shape[0]
    nj = WIDTH // tn
    out = jax.ShapeDtypeStruct((n, WIDTH), BF16)

    def wspec(k):
        return pl.BlockSpec((D_MODEL, tn), lambda i, j: (0, j + k * nj))

    ospec = pl.BlockSpec((tm, tn), lambda i, j: (i, j))
    return pl.pallas_call(
        _moba_in_kernel,
        out_shape=(out, out, out, out),
        grid=(n // tm, nj),
        in_specs=[pl.BlockSpec((tm, D_MODEL), lambda i, j: (i, 0)),
                  wspec(0), wspec(1), wspec(2), wspec(3)],
        out_specs=(ospec, ospec, ospec, ospec),
        scratch_shapes=[pltpu.VMEM((tm, D_MODEL), BF16)],
        compiler_params=pltpu.CompilerParams(
            dimension_semantics=("arbitrary", "arbitrary"),
            vmem_limit_bytes=VMEM_LIMIT),
        name="moba_in",
    )(x, w, w, w, w)


def _moba_attn_kernel(q_ref, k_ref, v_ref, zs_ref, y_ref, vt_ref, s_ref):
    seq = k_ref.shape[0]
    n_heads = k_ref.shape[1] // HEAD_DIM
    nb = seq // MOBA_BLOCK
    blk = MOBA_BLOCK
    neg_inf = jnp.float32(-jnp.inf)

    def hcols(h):
        return slice(h * HEAD_DIM, (h + 1) * HEAD_DIM)

    def qblock(own, h):
        return q_ref[own * blk:(own + 1) * blk, hcols(h)]

    def scores(own, h):
        n_keys = (own + 1) * blk
        s_ref[h, own % 2, 0:n_keys, :] = _dot_nt(k_ref[0:n_keys, hcols(h)], qblock(own, h))

    def stats(own, h):
        s_t = s_ref.at[h, own % 2]
        bias = None
        if own > MOBA_TOP_K:
            gate = _dot_nt(k_means[h], qblock(own, h).astype(F32))
            gate = jnp.where(blk_i < own, gate, neg_inf)
            cnt = jnp.zeros((nb, blk), jnp.int32)
            for m in range(own):
                gm = gate[m:m + 1, :]
                beats = (gm > gate) | ((gm == gate) & (m < blk_i))
                cnt = cnt + jnp.where(beats, 1, 0)
            bias = jnp.where((cnt < MOBA_TOP_K) & (blk_i < own), 0.0, neg_inf)

        own_rows = slice(own * blk, (own + 1) * blk)
        s_own = jnp.where(causal_t, s_t[own_rows, :], neg_inf)
        s_t[own_rows, :] = s_own
        m_q = jnp.max(s_own, axis=0, keepdims=True)
        for n in range(own):
            bm = jnp.max(s_t[n * blk:(n + 1) * blk, :], axis=0, keepdims=True)
            if bias is not None:
                bm = bm + bias[n:n + 1, :]
            m_q = jnp.maximum(m_q, bm)
        return m_q, bias

    def exp_pv(own, h, m_q, bias):
        s_t = s_ref.at[h, own % 2]
        acc = jnp.zeros((HEAD_DIM + ONES_ROWS, blk), F32)
        for n in range(own + 1):
            shift = m_q
            if bias is not None and n < own:
                shift = m_q - bias[n:n + 1, :]
            pn = jnp.exp2(s_t[n * blk:(n + 1) * blk, :] - shift)
            acc = acc + _dot(vt_ref[h, :, n * blk:(n + 1) * blk], pn.astype(BF16))
        o = (acc[0:HEAD_DIM, :] / acc[HEAD_DIM:HEAD_DIM + 1, :]).T
        rows = slice(own * blk, (own + 1) * blk)
        y_ref[rows, hcols(h)] = (o * zs_ref[rows, hcols(h)].astype(F32)).astype(BF16)

    assert n_heads == 2
    units = [(own, h) for own in reversed(range(nb)) for h in range(n_heads)]
    scores(*units[0])
    scores(*units[1])

    k_means = []
    for h in range(n_heads):
        for n in range(nb):
            vt_ref[h, 0:HEAD_DIM, n * blk:(n + 1) * blk] = (
                v_ref[n * blk:(n + 1) * blk, hcols(h)].astype(F32).T.astype(BF16))
        vt_ref[h, HEAD_DIM:HEAD_DIM + ONES_ROWS, :] = jnp.ones((ONES_ROWS, seq), BF16)
        kf = k_ref[:, hcols(h)].astype(F32)
        k_means.append(jnp.mean(kf.reshape(nb, blk, HEAD_DIM), axis=1))

    key_i = lax.broadcasted_iota(jnp.int32, (blk, blk), 0)
    qry_i = lax.broadcasted_iota(jnp.int32, (blk, blk), 1)
    causal_t = key_i <= qry_i
    blk_i = lax.broadcasted_iota(jnp.int32, (nb, blk), 0)

    st = stats(*units[0])
    for i, unit in enumerate(units):
        if i + 2 < len(units):
            scores(*units[i + 2])
        st_next = stats(*units[i + 1]) if i + 1 < len(units) else None
        exp_pv(*unit, *st)
        st = st_next


def _moba_attn(q, k, v, zs, batch, seq, heads_per_step):
    spec = pl.BlockSpec((seq, heads_per_step * HEAD_DIM), lambda b, h: (b, h))
    return pl.pallas_call(
        _moba_attn_kernel,
        out_shape=jax.ShapeDtypeStruct((batch * seq, WIDTH), BF16),
        grid=(batch, HEADS // heads_per_step),
        in_specs=[spec, spec, spec, spec],
        out_specs=spec,
        scratch_shapes=[pltpu.VMEM((heads_per_step, HEAD_DIM + ONES_ROWS, seq), BF16),
                        pltpu.VMEM((heads_per_step, 2, seq, MOBA_BLOCK), F32)],
        compiler_params=pltpu.CompilerParams(
            dimension_semantics=("arbitrary", "arbitrary"),
            vmem_limit_bytes=VMEM_LIMIT),
        name="moba_attn",
    )(q, k, v, zs)


def kernel(x, p, w_in_a, sgu_norm_g, sgu_norm_b, w_s, b_s, w_in_b, w_out, ln_g, ln_b,
           w_ple_gate, w_ple_proj):
    batch, seq, d = x.shape
    n = batch * seq
    x2 = x.reshape(n, d)
    p2 = p.reshape(DEPTH, n, PLE_DIM)

    tail_params = (p2, w_out.astype(BF16), ln_g[:, None, :], ln_b[:, None, :],
                   w_ple_gate.astype(BF16), w_ple_proj.astype(BF16))

    y = _sgu(x2, w_in_a[0].astype(BF16), sgu_norm_g[0][None], sgu_norm_b[0][None],
             w_s[0], b_s[0].T, tm=512)
    x2 = _tail(0, y, x2, *tail_params, tm=512)

    q, k, v, zs = _moba_in(x2, w_in_b[0].astype(BF16), tm=1024, tn=512)
    y = _moba_attn(q, k, v, zs, batch, seq, heads_per_step=2)
    x2 = _tail(1, y, x2, *tail_params, tm=512)

    return x2.reshape(batch, seq, d)
```

```python
from functools import partial

import jax
import jax.numpy as jnp
from jax import lax
from jax.experimental import pallas as pl
from jax.experimental.pallas import tpu as pltpu

D_MODEL = 2048
WIDTH = 2048
SGU_CHUNK = 128
SGU_GROUPS = 16
HEADS = 16
HEAD_DIM = 128
MOBA_BLOCK = 256
MOBA_TOP_K = 3
PLE_DIM = 256
LN_EPS = 1e-5
DEPTH = 2
ALPHA = (2 * DEPTH) ** 0.25
LOG2_E = 1.4426950408889634
MXU_COLS = 256
EPILOGUE_ROWS = 512
TAIL_ROWS = 256
ONES_ROWS = 16

BF16 = jnp.bfloat16
F32 = jnp.float32

VMEM_LIMIT = 56 * 1024 * 1024


def _dot(a, b):
    return jnp.dot(a, b, preferred_element_type=F32)


def _dot_nt(a, b):
    return lax.dot_general(a, b, (((1,), (1,)), ((), ())), preferred_element_type=F32)


def _silu(z):
    return z * jax.nn.sigmoid(z)


def _layer_norm(x, g, b):
    mu = jnp.mean(x, axis=-1, keepdims=True)
    xc = x - mu
    var = jnp.mean(xc * xc, axis=-1, keepdims=True)
    return xc * lax.rsqrt(var + LN_EPS) * g + b


def _sgu_kernel(x_ref, w_ref, ng_ref, nb_ref, ws_ref, bs_ref, y_ref, xb_ref, vg_ref, wc_ref):
    @pl.when(pl.program_id(0) == 0)
    def _():
        row = lax.broadcasted_iota(jnp.int32, (SGU_CHUNK, SGU_CHUNK), 0)
        col = lax.broadcasted_iota(jnp.int32, (SGU_CHUNK, SGU_CHUNK), 1)
        keep = (col <= row)[None]
        wc_ref[...] = jnp.where(keep, ws_ref[...], 0.0).astype(BF16)

    tm = x_ref.shape[0]
    n_pieces = WIDTH // MXU_COLS
    gw = WIDTH // SGU_GROUPS
    xb_ref[...] = x_ref[...].astype(BF16)

    row_sum = jnp.zeros((tm, 1), F32)
    for c in range(n_pieces):
        cols = slice(c * MXU_COLS, (c + 1) * MXU_COLS)
        vc = jax.nn.gelu(_dot(xb_ref[...], w_ref[:, WIDTH + c * MXU_COLS:WIDTH + (c + 1) * MXU_COLS]))
        vg_ref[:, cols] = vc
        row_sum = row_sum + jnp.sum(vc, axis=-1, keepdims=True)
    mu = row_sum * (1.0 / WIDTH)
    sq_sum = jnp.zeros((tm, 1), F32)
    for c in range(n_pieces):
        cols = slice(c * MXU_COLS, (c + 1) * MXU_COLS)
        dc = vg_ref[:, cols] - mu
        sq_sum = sq_sum + jnp.sum(dc * dc, axis=-1, keepdims=True)
    rstd = lax.rsqrt(sq_sum * (1.0 / WIDTH) + LN_EPS)

    def uz_piece(c):
        u = jax.nn.gelu(_dot(xb_ref[...], w_ref[:, c * MXU_COLS:(c + 1) * MXU_COLS]))
        z = _dot(xb_ref[...], w_ref[:, 2 * WIDTH + c * MXU_COLS:2 * WIDTH + (c + 1) * MXU_COLS])
        return u * _silu(z)

    uz_next = uz_piece(0)
    for c in range(n_pieces):
        cols = slice(c * MXU_COLS, (c + 1) * MXU_COLS)
        uz = uz_next
        if c + 1 < n_pieces:
            uz_next = uz_piece(c + 1)
        vn = ((vg_ref[:, cols] - mu) * rstd * ng_ref[:, cols] + nb_ref[:, cols]).astype(BF16)
        for k in range(tm // SGU_CHUNK):
            rows = slice(k * SGU_CHUNK, (k + 1) * SGU_CHUNK)
            for gi in range(MXU_COLS // gw):
                g = c * (MXU_COLS // gw) + gi
                gcols = slice(gi * gw, (gi + 1) * gw)
                s = _dot(wc_ref[g], vn[rows, gcols]) + bs_ref[:, g:g + 1]
                y_ref[rows, c * MXU_COLS + gi * gw:c * MXU_COLS + (gi + 1) * gw] = (
                    uz[rows, gcols] * s).astype(BF16)


def _sgu(x, w, ng, nb, ws, bs_t, tm):
    n = x.shape[0]
    row = lambda i: (i, 0)
    fix2 = lambda i: (0, 0)
    return pl.pallas_call(
        _sgu_kernel,
        out_shape=jax.ShapeDtypeStruct((n, WIDTH), BF16),
        grid=(n // tm,),
        in_specs=[
            pl.BlockSpec((tm, D_MODEL), row),
            pl.BlockSpec((D_MODEL, 3 * WIDTH), fix2, pipeline_mode=pl.Buffered(1)),
            pl.BlockSpec((1, WIDTH), fix2),
            pl.BlockSpec((1, WIDTH), fix2),
            pl.BlockSpec((SGU_GROUPS, SGU_CHUNK, SGU_CHUNK), lambda i: (0, 0, 0)),
            pl.BlockSpec((SGU_CHUNK, SGU_GROUPS), fix2),
        ],
        out_specs=pl.BlockSpec((tm, WIDTH), row),
        scratch_shapes=[pltpu.VMEM((tm, D_MODEL), BF16),
                        pltpu.VMEM((tm, WIDTH), F32),
                        pltpu.VMEM((SGU_GROUPS, SGU_CHUNK, SGU_CHUNK), BF16)],
        compiler_params=pltpu.CompilerParams(
            dimension_semantics=("arbitrary",), vmem_limit_bytes=VMEM_LIMIT),
        name="sgu",
    )(x, w, ng, nb, ws, bs_t)


def _tail_kernel(y_ref, x_ref, p_ref, wo_ref, g_ref, b_ref, wg_ref, wp_ref, o_ref,
                 maybe_ob_ref=None):
    tm, d = o_ref.shape
    n_pieces = tm // TAIL_ROWS

    def pre_norm(r):
        rows = slice(r * TAIL_ROWS, (r + 1) * TAIL_ROWS)
        return ALPHA * x_ref[rows, :] + _dot(y_ref[rows, :], wo_ref[...])

    t = pre_norm(0)
    for r in range(n_pieces):
        rows = slice(r * TAIL_ROWS, (r + 1) * TAIL_ROWS)
        t_next = pre_norm(r + 1) if r + 1 < n_pieces else None
        xn = _layer_norm(t, g_ref[...], b_ref[...])
        xnb = xn.astype(BF16)
        pb = p_ref[rows, :].astype(BF16)
        for c in range(d // MXU_COLS):
            cols = slice(c * MXU_COLS, (c + 1) * MXU_COLS)
            gate = jax.nn.sigmoid(_dot(xnb, wg_ref[:, cols]))
            proj = _dot(pb, wp_ref[:, cols])
            out = xn[:, cols] + gate * proj
            o_ref[rows, cols] = out
            if maybe_ob_ref is not None:
                maybe_ob_ref[rows, cols] = out.astype(BF16)
        t = t_next


def _tail(layer, y, x, p, wo, g, b, wg, wp, tm, with_bf16_copy):
    n = x.shape[0]
    row = lambda i: (i, 0)
    fix = lambda i: (layer, 0, 0)
    once = pl.Buffered(1)
    out_shape = [jax.ShapeDtypeStruct((n, D_MODEL), F32)]
    out_specs = [pl.BlockSpec((tm, D_MODEL), row)]
    if with_bf16_copy:
        out_shape.append(jax.ShapeDtypeStruct((n, D_MODEL), BF16))
        out_specs.append(pl.BlockSpec((tm, D_MODEL), row))
    return pl.pallas_call(
        _tail_kernel,
        out_shape=out_shape,
        grid=(n // tm,),
        in_specs=[
            pl.BlockSpec((tm, WIDTH), row),
            pl.BlockSpec((tm, D_MODEL), row),
            pl.BlockSpec((None, tm, PLE_DIM), lambda i: (layer, i, 0)),
            pl.BlockSpec((None, WIDTH, D_MODEL), fix, pipeline_mode=once),
            pl.BlockSpec((None, 1, D_MODEL), fix),
            pl.BlockSpec((None, 1, D_MODEL), fix),
            pl.BlockSpec((None, D_MODEL, D_MODEL), fix, pipeline_mode=once),
            pl.BlockSpec((None, PLE_DIM, D_MODEL), fix, pipeline_mode=once),
        ],
        out_specs=out_specs,
        compiler_params=pltpu.CompilerParams(
            dimension_semantics=("arbitrary",), vmem_limit_bytes=VMEM_LIMIT),
        name="layer_tail",
    )(y, x, p, wo, g, b, wg, wp)


def _moba_in_kernel(x_ref, wq_ref, wk_ref, wv_ref, wz_ref,
                    q_ref, k_ref, v_ref, zs_ref, wb_ref):
    @pl.when(pl.program_id(1) == 0)
    def _():
        for k, w_ref in enumerate((wq_ref, wk_ref, wv_ref, wz_ref)):
            wb_ref[k] = w_ref[...].astype(BF16)

    tm = x_ref.shape[0]
    for r in range(tm // EPILOGUE_ROWS):
        rows = slice(r * EPILOGUE_ROWS, (r + 1) * EPILOGUE_ROWS)
        xb = x_ref[rows, :]
        zs_ref[rows, :] = _silu(_dot(xb, wb_ref[3])).astype(BF16)
        q_ref[rows, :] = (_dot(xb, wb_ref[0]) * (HEAD_DIM ** -0.5 * LOG2_E)).astype(BF16)
        k_ref[rows, :] = _dot(xb, wb_ref[1]).astype(BF16)
        v_ref[rows, :] = _dot(xb, wb_ref[2]).astype(BF16)


def _moba_in(xb, w, tm):
    n = xb.shape[0]
    tn = MXU_COLS
    nj = WIDTH // tn
    out = jax.ShapeDtypeStruct((n, WIDTH), BF16)

    def wspec(k):
        return pl.BlockSpec((D_MODEL, tn), lambda j, i: (0, j + k * nj))

    ospec = pl.BlockSpec((tm, tn), lambda j, i: (i, j))
    return pl.pallas_call(
        _moba_in_kernel,
        out_shape=(out, out, out, out),
        grid=(nj, n // tm),
        in_specs=[pl.BlockSpec((tm, D_MODEL), lambda j, i: (i, 0)),
                  wspec(0), wspec(1), wspec(2), wspec(3)],
        out_specs=(ospec, ospec, ospec, ospec),
        scratch_shapes=[pltpu.VMEM((4, D_MODEL, tn), BF16)],
        compiler_params=pltpu.CompilerParams(
            dimension_semantics=("arbitrary", "arbitrary"),
            vmem_limit_bytes=VMEM_LIMIT),
        name="moba_in",
    )(xb, w, w, w, w)


def _moba_attn_kernel(q_ref, k_ref, v_ref, zs_ref, y_ref, vt_ref, s_ref):
    seq = k_ref.shape[0]
    n_heads = k_ref.shape[1] // HEAD_DIM
    nb = seq // MOBA_BLOCK
    blk = MOBA_BLOCK
    neg_inf = jnp.float32(-jnp.inf)

    def hcols(h):
        return slice(h * HEAD_DIM, (h + 1) * HEAD_DIM)

    def qblock(own, h):
        return q_ref[own * blk:(own + 1) * blk, hcols(h)]

    def scores(own, h):
        n_keys = (own + 1) * blk
        s_ref[h, own % 2, 0:n_keys, :] = _dot_nt(k_ref[0:n_keys, hcols(h)], qblock(own, h))

    def stats(own, h):
        s_t = s_ref.at[h, own % 2]
        bias = None
        if own > MOBA_TOP_K:
            gate = _dot_nt(k_means[h], qblock(own, h).astype(F32))
            gate = jnp.where(blk_i < own, gate, neg_inf)
            cnt = jnp.zeros((nb, blk), jnp.int32)
            for m in range(own):
                gm = gate[m:m + 1, :]
                beats = (gm > gate) | ((gm == gate) & (m < blk_i))
                cnt = cnt + jnp.where(beats, 1, 0)
            bias = jnp.where((cnt < MOBA_TOP_K) & (blk_i < own), 0.0, neg_inf)

        own_rows = slice(own * blk, (own + 1) * blk)
        s_own = jnp.where(causal_t, s_t[own_rows, :], neg_inf)
        s_t[own_rows, :] = s_own
        m_q = jnp.max(s_own, axis=0, keepdims=True)
        for n in range(own):
            bm = jnp.max(s_t[n * blk:(n + 1) * blk, :], axis=0, keepdims=True)
            if bias is not None:
                bm = bm + bias[n:n + 1, :]
            m_q = jnp.maximum(m_q, bm)
        return m_q, bias

    def exp_pv(own, h, m_q, bias):
        s_t = s_ref.at[h, own % 2]
        acc = jnp.zeros((HEAD_DIM + ONES_ROWS, blk), F32)
        for n in range(own + 1):
            shift = m_q
            if bias is not None and n < own:
                shift = m_q - bias[n:n + 1, :]
            pn = jnp.exp2(s_t[n * blk:(n + 1) * blk, :] - shift)
            acc = acc + _dot(vt_ref[h, :, n * blk:(n + 1) * blk], pn.astype(BF16))
        o = (acc[0:HEAD_DIM, :] / acc[HEAD_DIM:HEAD_DIM + 1, :]).T
        rows = slice(own * blk, (own + 1) * blk)
        y_ref[rows, hcols(h)] = (o * zs_ref[rows, hcols(h)].astype(F32)).astype(BF16)

    assert n_heads == 2
    units = [(own, h) for own in reversed(range(nb)) for h in range(n_heads)]
    scores(*units[0])
    scores(*units[1])

    k_means = []
    for h in range(n_heads):
        for n in range(nb):
            vt_ref[h, 0:HEAD_DIM, n * blk:(n + 1) * blk] = (
                v_ref[n * blk:(n + 1) * blk, hcols(h)].astype(F32).T.astype(BF16))
        vt_ref[h, HEAD_DIM:HEAD_DIM + ONES_ROWS, :] = jnp.ones((ONES_ROWS, seq), BF16)
        kf = k_ref[:, hcols(h)].astype(F32)
        k_means.append(jnp.mean(kf.reshape(nb, blk, HEAD_DIM), axis=1))

    key_i = lax.broadcasted_iota(jnp.int32, (blk, blk), 0)
    qry_i = lax.broadcasted_iota(jnp.int32, (blk, blk), 1)
    causal_t = key_i <= qry_i
    blk_i = lax.broadcasted_iota(jnp.int32, (nb, blk), 0)

    st = stats(*units[0])
    for i, unit in enumerate(units):
        if i + 2 < len(units):
            scores(*units[i + 2])
        st_next = stats(*units[i + 1]) if i + 1 < len(units) else None
        exp_pv(*unit, *st)
        st = st_next


def _moba_attn(q, k, v, zs, batch, seq, heads_per_step):
    spec = pl.BlockSpec((seq, heads_per_step * HEAD_DIM), lambda b, h: (b, h))
    return pl.pallas_call(
        _moba_attn_kernel,
        out_shape=jax.ShapeDtypeStruct((batch * seq, WIDTH), BF16),
        grid=(batch, HEADS // heads_per_step),
        in_specs=[spec, spec, spec, spec],
        out_specs=spec,
        scratch_shapes=[pltpu.VMEM((heads_per_step, HEAD_DIM + ONES_ROWS, seq), BF16),
                        pltpu.VMEM((heads_per_step, 2, seq, MOBA_BLOCK), F32)],
        compiler_params=pltpu.CompilerParams(
            dimension_semantics=("arbitrary", "arbitrary"),
            vmem_limit_bytes=VMEM_LIMIT),
        name="moba_attn",
    )(q, k, v, zs)


def kernel(x, p, w_in_a, sgu_norm_g, sgu_norm_b, w_s, b_s, w_in_b, w_out, ln_g, ln_b,
           w_ple_gate, w_ple_proj):
    batch, seq, d = x.shape
    n = batch * seq
    x2 = x.reshape(n, d)
    p2 = p.reshape(DEPTH, n, PLE_DIM)

    tail_params = (p2, w_out.astype(BF16), ln_g[:, None, :], ln_b[:, None, :],
                   w_ple_gate.astype(BF16), w_ple_proj.astype(BF16))

    y = _sgu(x2, w_in_a[0].astype(BF16), sgu_norm_g[0][None], sgu_norm_b[0][None],
             w_s[0], b_s[0].T, tm=512)
    x2, x2b = _tail(0, y, x2, *tail_params, tm=512, with_bf16_copy=True)

    q, k, v, zs = _moba_in(x2b, w_in_b[0], tm=1024)
    y = _moba_attn(q, k, v, zs, batch, seq, heads_per_step=2)
    (x2,) = _tail(1, y, x2, *tail_params, tm=512, with_bf16_copy=False)

    return x2.reshape(batch, seq, d)
```

```python
from functools import partial

import jax
import jax.numpy as jnp
from jax import lax
from jax.experimental import pallas as pl
from jax.experimental.pallas import tpu as pltpu

D_MODEL = 2048
WIDTH = 2048
SGU_CHUNK = 128
SGU_GROUPS = 16
HEADS = 16
HEAD_DIM = 128
MOBA_BLOCK = 256
MOBA_TOP_K = 3
PLE_DIM = 256
LN_EPS = 1e-5
DEPTH = 2
ALPHA = (2 * DEPTH) ** 0.25
LOG2_E = 1.4426950408889634
MXU_COLS = 256
EPILOGUE_ROWS = 512
TAIL_ROWS = 256
BF16_ROWS = 16
ONES_ROWS = BF16_ROWS

BF16 = jnp.bfloat16
F32 = jnp.float32

VMEM_LIMIT = 56 * 1024 * 1024


def _dot(a, b):
    return jnp.dot(a, b, preferred_element_type=F32)


def _dot_nt(a, b):
    return lax.dot_general(a, b, (((1,), (1,)), ((), ())), preferred_element_type=F32)


def _silu(z):
    return z * jax.nn.sigmoid(z)


def _layer_norm(x, g, b):
    mu = jnp.mean(x, axis=-1, keepdims=True)
    xc = x - mu
    var = jnp.mean(xc * xc, axis=-1, keepdims=True)
    return xc * lax.rsqrt(var + LN_EPS) * g + b


def _side_cast_specs(sources, n_steps, step_index):
    in_specs, out_specs, out_shapes = [], [], []
    for arr, layer in sources:
        rows, cols = arr.shape[-2:]
        slab = rows // n_steps
        assert slab * n_steps == rows and slab % BF16_ROWS == 0, (rows, n_steps)
        if layer is None:
            in_specs.append(pl.BlockSpec((slab, cols), lambda *g: (step_index(*g), 0)))
        else:
            in_specs.append(pl.BlockSpec(
                (None, slab, cols), lambda *g, layer=layer: (layer, step_index(*g), 0)))
        out_specs.append(pl.BlockSpec((slab, cols), lambda *g: (step_index(*g), 0)))
        out_shapes.append(jax.ShapeDtypeStruct((rows, cols), BF16))
    return in_specs, out_specs, out_shapes


def _side_cast(side_in_refs, side_out_refs):
    for src_ref, dst_ref in zip(side_in_refs, side_out_refs):
        dst_ref[...] = src_ref[...].astype(BF16)


def _sgu_kernel(n_side, x_ref, w_ref, ng_ref, nb_ref, ws_ref, bs_ref, *refs):
    side_in, (y_ref, *side_out), (xb_ref, vg_ref, wc_ref) = (
        refs[:n_side], refs[n_side:2 * n_side + 1], refs[2 * n_side + 1:])

    @pl.when(pl.program_id(0) == 0)
    def _():
        row = lax.broadcasted_iota(jnp.int32, (SGU_CHUNK, SGU_CHUNK), 0)
        col = lax.broadcasted_iota(jnp.int32, (SGU_CHUNK, SGU_CHUNK), 1)
        keep = (col <= row)[None]
        wc_ref[...] = jnp.where(keep, ws_ref[...], 0.0).astype(BF16)

    tm = x_ref.shape[0]
    n_pieces = WIDTH // MXU_COLS
    gw = WIDTH // SGU_GROUPS
    xb_ref[...] = x_ref[...].astype(BF16)

    row_sum = jnp.zeros((tm, 1), F32)
    for c in range(n_pieces):
        cols = slice(c * MXU_COLS, (c + 1) * MXU_COLS)
        vc = jax.nn.gelu(_dot(xb_ref[...], w_ref[:, WIDTH + c * MXU_COLS:WIDTH + (c + 1) * MXU_COLS]))
        vg_ref[:, cols] = vc
        row_sum = row_sum + jnp.sum(vc, axis=-1, keepdims=True)
    _side_cast(side_in, side_out)
    mu = row_sum * (1.0 / WIDTH)
    sq_sum = jnp.zeros((tm, 1), F32)
    for c in range(n_pieces):
        cols = slice(c * MXU_COLS, (c + 1) * MXU_COLS)
        dc = vg_ref[:, cols] - mu
        sq_sum = sq_sum + jnp.sum(dc * dc, axis=-1, keepdims=True)
    rstd = lax.rsqrt(sq_sum * (1.0 / WIDTH) + LN_EPS)

    def uz_piece(c):
        u = jax.nn.gelu(_dot(xb_ref[...], w_ref[:, c * MXU_COLS:(c + 1) * MXU_COLS]))
        z = _dot(xb_ref[...], w_ref[:, 2 * WIDTH + c * MXU_COLS:2 * WIDTH + (c + 1) * MXU_COLS])
        return u * _silu(z)

    uz_next = uz_piece(0)
    for c in range(n_pieces):
        cols = slice(c * MXU_COLS, (c + 1) * MXU_COLS)
        uz = uz_next
        if c + 1 < n_pieces:
            uz_next = uz_piece(c + 1)
        vn = ((vg_ref[:, cols] - mu) * rstd * ng_ref[:, cols] + nb_ref[:, cols]).astype(BF16)
        for k in range(tm // SGU_CHUNK):
            rows = slice(k * SGU_CHUNK, (k + 1) * SGU_CHUNK)
            for gi in range(MXU_COLS // gw):
                g = c * (MXU_COLS // gw) + gi
                gcols = slice(gi * gw, (gi + 1) * gw)
                s = _dot(wc_ref[g], vn[rows, gcols]) + bs_ref[:, g:g + 1]
                y_ref[rows, c * MXU_COLS + gi * gw:c * MXU_COLS + (gi + 1) * gw] = (
                    uz[rows, gcols] * s).astype(BF16)


def _sgu(x, w, ng, nb, ws, bs_t, tm, side_sources):
    n = x.shape[0]
    row = lambda i: (i, 0)
    fix2 = lambda i: (0, 0)
    side_in_specs, side_out_specs, side_out_shapes = _side_cast_specs(
        side_sources, n // tm, lambda i: i)
    return pl.pallas_call(
        partial(_sgu_kernel, len(side_sources)),
        out_shape=[jax.ShapeDtypeStruct((n, WIDTH), BF16)] + side_out_shapes,
        grid=(n // tm,),
        in_specs=[
            pl.BlockSpec((tm, D_MODEL), row),
            pl.BlockSpec((D_MODEL, 3 * WIDTH), fix2, pipeline_mode=pl.Buffered(1)),
            pl.BlockSpec((1, WIDTH), fix2),
            pl.BlockSpec((1, WIDTH), fix2),
            pl.BlockSpec((SGU_GROUPS, SGU_CHUNK, SGU_CHUNK), lambda i: (0, 0, 0)),
            pl.BlockSpec((SGU_CHUNK, SGU_GROUPS), fix2),
        ] + side_in_specs,
        out_specs=[pl.BlockSpec((tm, WIDTH), row)] + side_out_specs,
        scratch_shapes=[pltpu.VMEM((tm, D_MODEL), BF16),
                        pltpu.VMEM((tm, WIDTH), F32),
                        pltpu.VMEM((SGU_GROUPS, SGU_CHUNK, SGU_CHUNK), BF16)],
        compiler_params=pltpu.CompilerParams(
            dimension_semantics=("arbitrary",), vmem_limit_bytes=VMEM_LIMIT),
        name="sgu",
    )(x, w, ng, nb, ws, bs_t, *[arr for arr, _ in side_sources])


def _tail_kernel(y_ref, x_ref, p_ref, wo_ref, g_ref, b_ref, wg_ref, wp_ref, o_ref,
                 maybe_ob_ref=None):
    tm, d = o_ref.shape
    n_pieces = tm // TAIL_ROWS

    def pre_norm(r):
        rows = slice(r * TAIL_ROWS, (r + 1) * TAIL_ROWS)
        return ALPHA * x_ref[rows, :] + _dot(y_ref[rows, :], wo_ref[...])

    t = pre_norm(0)
    for r in range(n_pieces):
        rows = slice(r * TAIL_ROWS, (r + 1) * TAIL_ROWS)
        t_next = pre_norm(r + 1) if r + 1 < n_pieces else None
        xn = _layer_norm(t, g_ref[...], b_ref[...])
        xnb = xn.astype(BF16)
        pb = p_ref[rows, :].astype(BF16)
        for c in range(d // MXU_COLS):
            cols = slice(c * MXU_COLS, (c + 1) * MXU_COLS)
            gate = jax.nn.sigmoid(_dot(xnb, wg_ref[:, cols]))
            proj = _dot(pb, wp_ref[:, cols])
            out = xn[:, cols] + gate * proj
            o_ref[rows, cols] = out
            if maybe_ob_ref is not None:
                maybe_ob_ref[rows, cols] = out.astype(BF16)
        t = t_next


def _tail(layer, y, x, p, wo, g, b, wg, wp, tm, with_bf16_copy):
    n = x.shape[0]
    row = lambda i: (i, 0)
    fix = lambda i: (layer, 0, 0)
    fix2 = lambda i: (0, 0)
    once = pl.Buffered(1)
    out_shape = [jax.ShapeDtypeStruct((n, D_MODEL), F32)]
    out_specs = [pl.BlockSpec((tm, D_MODEL), row)]
    if with_bf16_copy:
        out_shape.append(jax.ShapeDtypeStruct((n, D_MODEL), BF16))
        out_specs.append(pl.BlockSpec((tm, D_MODEL), row))
    return pl.pallas_call(
        _tail_kernel,
        out_shape=out_shape,
        grid=(n // tm,),
        in_specs=[
            pl.BlockSpec((tm, WIDTH), row),
            pl.BlockSpec((tm, D_MODEL), row),
            pl.BlockSpec((None, tm, PLE_DIM), lambda i: (layer, i, 0)),
            pl.BlockSpec((WIDTH, D_MODEL), fix2, pipeline_mode=once),
            pl.BlockSpec((None, 1, D_MODEL), fix),
            pl.BlockSpec((None, 1, D_MODEL), fix),
            pl.BlockSpec((D_MODEL, D_MODEL), fix2, pipeline_mode=once),
            pl.BlockSpec((None, PLE_DIM, D_MODEL), fix, pipeline_mode=once),
        ],
        out_specs=out_specs,
        compiler_params=pltpu.CompilerParams(
            dimension_semantics=("arbitrary",), vmem_limit_bytes=VMEM_LIMIT),
        name="layer_tail",
    )(y, x, p, wo, g, b, wg, wp)


def _moba_in_kernel(n_side, x_ref, wq_ref, wk_ref, wv_ref, wz_ref, *refs):
    side_in, (q_ref, k_ref, v_ref, zs_ref, *side_out), (wb_ref,) = (
        refs[:n_side], refs[n_side:2 * n_side + 4], refs[2 * n_side + 4:])

    @pl.when(pl.program_id(1) == 0)
    def _():
        for k, w_ref in enumerate((wq_ref, wk_ref, wv_ref, wz_ref)):
            wb_ref[k] = w_ref[...].astype(BF16)

    tm = x_ref.shape[0]
    for r in range(tm // EPILOGUE_ROWS):
        rows = slice(r * EPILOGUE_ROWS, (r + 1) * EPILOGUE_ROWS)
        xb = x_ref[rows, :]
        zs_ref[rows, :] = _silu(_dot(xb, wb_ref[3])).astype(BF16)
        q_ref[rows, :] = (_dot(xb, wb_ref[0]) * (HEAD_DIM ** -0.5 * LOG2_E)).astype(BF16)
        k_ref[rows, :] = _dot(xb, wb_ref[1]).astype(BF16)
        v_ref[rows, :] = _dot(xb, wb_ref[2]).astype(BF16)
        if r == 0:
            _side_cast(side_in, side_out)


def _moba_in(xb, w, tm, side_sources):
    n = xb.shape[0]
    tn = MXU_COLS
    nj, ni = WIDTH // tn, n // tm
    out = jax.ShapeDtypeStruct((n, WIDTH), BF16)
    side_in_specs, side_out_specs, side_out_shapes = _side_cast_specs(
        side_sources, nj * ni, lambda j, i: j * ni + i)

    def wspec(k):
        return pl.BlockSpec((D_MODEL, tn), lambda j, i: (0, j + k * nj))

    ospec = pl.BlockSpec((tm, tn), lambda j, i: (i, j))
    return pl.pallas_call(
        partial(_moba_in_kernel, len(side_sources)),
        out_shape=[out, out, out, out] + side_out_shapes,
        grid=(nj, ni),
        in_specs=[pl.BlockSpec((tm, D_MODEL), lambda j, i: (i, 0)),
                  wspec(0), wspec(1), wspec(2), wspec(3)] + side_in_specs,
        out_specs=[ospec, ospec, ospec, ospec] + side_out_specs,
        scratch_shapes=[pltpu.VMEM((4, D_MODEL, tn), BF16)],
        compiler_params=pltpu.CompilerParams(
            dimension_semantics=("arbitrary", "arbitrary"),
            vmem_limit_bytes=VMEM_LIMIT),
        name="moba_in",
    )(xb, w, w, w, w, *[arr for arr, _ in side_sources])


def _moba_attn_kernel(q_ref, k_ref, v_ref, zs_ref, y_ref, vt_ref, s_ref):
    seq = k_ref.shape[0]
    n_heads = k_ref.shape[1] // HEAD_DIM
    nb = seq // MOBA_BLOCK
    blk = MOBA_BLOCK
    neg_inf = jnp.float32(-jnp.inf)

    def hcols(h):
        return slice(h * HEAD_DIM, (h + 1) * HEAD_DIM)

    def qblock(own, h):
        return q_ref[own * blk:(own + 1) * blk, hcols(h)]

    def scores(own, h):
        n_keys = (own + 1) * blk
        s_ref[h, own % 2, 0:n_keys, :] = _dot_nt(k_ref[0:n_keys, hcols(h)], qblock(own, h))

    def stats(own, h):
        s_t = s_ref.at[h, own % 2]
        bias = None
        if own > MOBA_TOP_K:
            gate = _dot_nt(k_means[h], qblock(own, h).astype(F32))
            gate = jnp.where(blk_i < own, gate, neg_inf)
            cnt = jnp.zeros((nb, blk), jnp.int32)
            for m in range(own):
                gm = gate[m:m + 1, :]
                beats = (gm > gate) | ((gm == gate) & (m < blk_i))
                cnt = cnt + jnp.where(beats, 1, 0)
            bias = jnp.where((cnt < MOBA_TOP_K) & (blk_i < own), 0.0, neg_inf)

        own_rows = slice(own * blk, (own + 1) * blk)
        s_own = jnp.where(causal_t, s_t[own_rows, :], neg_inf)
        s_t[own_rows, :] = s_own
        m_q = jnp.max(s_own, axis=0, keepdims=True)
        for n in range(own):
            bm = jnp.max(s_t[n * blk:(n + 1) * blk, :], axis=0, keepdims=True)
            if bias is not None:
                bm = bm + bias[n:n + 1, :]
            m_q = jnp.maximum(m_q, bm)
        return m_q, bias

    def exp_pv(own, h, m_q, bias):
        s_t = s_ref.at[h, own % 2]
        acc = jnp.zeros((HEAD_DIM + ONES_ROWS, blk), F32)
        for n in range(own + 1):
            shift = m_q
            if bias is not None and n < own:
                shift = m_q - bias[n:n + 1, :]
            pn = jnp.exp2(s_t[n * blk:(n + 1) * blk, :] - shift)
            acc = acc + _dot(vt_ref[h, :, n * blk:(n + 1) * blk], pn.astype(BF16))
        o = (acc[0:HEAD_DIM, :] / acc[HEAD_DIM:HEAD_DIM + 1, :]).T
        rows = slice(own * blk, (own + 1) * blk)
        y_ref[rows, hcols(h)] = (o * zs_ref[rows, hcols(h)].astype(F32)).astype(BF16)

    assert n_heads == 2
    units = [(own, h) for own in reversed(range(nb)) for h in range(n_heads)]
    scores(*units[0])
    scores(*units[1])

    k_means = []
    for h in range(n_heads):
        for n in range(nb):
            vt_ref[h, 0:HEAD_DIM, n * blk:(n + 1) * blk] = (
                v_ref[n * blk:(n + 1) * blk, hcols(h)].astype(F32).T.astype(BF16))
        vt_ref[h, HEAD_DIM:HEAD_DIM + ONES_ROWS, :] = jnp.ones((ONES_ROWS, seq), BF16)
        kf = k_ref[:, hcols(h)].astype(F32)
        k_means.append(jnp.mean(kf.reshape(nb, blk, HEAD_DIM), axis=1))

    key_i = lax.broadcasted_iota(jnp.int32, (blk, blk), 0)
    qry_i = lax.broadcasted_iota(jnp.int32, (blk, blk), 1)
    causal_t = key_i <= qry_i
    blk_i = lax.broadcasted_iota(jnp.int32, (nb, blk), 0)

    st = stats(*units[0])
    for i, unit in enumerate(units):
        if i + 2 < len(units):
            scores(*units[i + 2])
        st_next = stats(*units[i + 1]) if i + 1 < len(units) else None
        exp_pv(*unit, *st)
        st = st_next


def _moba_attn(q, k, v, zs, batch, seq, heads_per_step):
    spec = pl.BlockSpec((seq, heads_per_step * HEAD_DIM), lambda b, h: (b, h))
    return pl.pallas_call(
        _moba_attn_kernel,
        out_shape=jax.ShapeDtypeStruct((batch * seq, WIDTH), BF16),
        grid=(batch, HEADS // heads_per_step),
        in_specs=[spec, spec, spec, spec],
        out_specs=spec,
        scratch_shapes=[pltpu.VMEM((heads_per_step, HEAD_DIM + ONES_ROWS, seq), BF16),
                        pltpu.VMEM((heads_per_step, 2, seq, MOBA_BLOCK), F32)],
        compiler_params=pltpu.CompilerParams(
            dimension_semantics=("arbitrary", "arbitrary"),
            vmem_limit_bytes=VMEM_LIMIT),
        name="moba_attn",
    )(q, k, v, zs)


def kernel(x, p, w_in_a, sgu_norm_g, sgu_norm_b, w_s, b_s, w_in_b, w_out, ln_g, ln_b,
           w_ple_gate, w_ple_proj):
    batch, seq, d = x.shape
    n = batch * seq
    x2 = x.reshape(n, d)
    p2 = p.reshape(DEPTH, n, PLE_DIM)

    ln_g3, ln_b3 = ln_g[:, None, :], ln_b[:, None, :]

    y, wo0, wg0, wp_flat = _sgu(
        x2, w_in_a[0].astype(BF16), sgu_norm_g[0][None], sgu_norm_b[0][None], w_s[0], b_s[0].T,
        tm=512,
        side_sources=[(w_out, 0), (w_ple_gate, 0),
                      (w_ple_proj.reshape(DEPTH * PLE_DIM, d), None)])
    wp = wp_flat.reshape(DEPTH, PLE_DIM, d)
    x2, x2b = _tail(0, y, x2, p2, wo0, ln_g3, ln_b3, wg0, wp, tm=512, with_bf16_copy=True)

    q, k, v, zs, wo1, wg1 = _moba_in(
        x2b, w_in_b[0], tm=2048, side_sources=[(w_out, 1), (w_ple_gate, 1)])
    y = _moba_attn(q, k, v, zs, batch, seq, heads_per_step=2)
    (x2,) = _tail(1, y, x2, p2, wo1, ln_g3, ln_b3, wg1, wp, tm=512, with_bf16_copy=False)

    return x2.reshape(batch, seq, d)
```

```python
from functools import partial

import jax
import jax.numpy as jnp
from jax import lax
from jax.experimental import pallas as pl
from jax.experimental.pallas import tpu as pltpu

D_MODEL = 2048
WIDTH = 2048
SGU_CHUNK = 128
SGU_GROUPS = 16
HEADS = 16
HEAD_DIM = 128
MOBA_BLOCK = 256
MOBA_TOP_K = 3
PLE_DIM = 256
LN_EPS = 1e-5
DEPTH = 2
ALPHA = (2 * DEPTH) ** 0.25
LOG2_E = 1.4426950408889634
MXU_COLS = 256
EPILOGUE_ROWS = 512
TAIL_ROWS = 256
BF16_ROWS = 16
ONES_ROWS = BF16_ROWS

BF16 = jnp.bfloat16
F32 = jnp.float32

VMEM_LIMIT = 56 * 1024 * 1024


def _dot(a, b):
    return jnp.dot(a, b, preferred_element_type=F32)


def _dot_nt(a, b):
    return lax.dot_general(a, b, (((1,), (1,)), ((), ())), preferred_element_type=F32)


def _silu(z):
    return z * jax.nn.sigmoid(z)


def _layer_norm(x, g, b):
    mu = jnp.mean(x, axis=-1, keepdims=True)
    xc = x - mu
    var = jnp.mean(xc * xc, axis=-1, keepdims=True)
    return xc * lax.rsqrt(var + LN_EPS) * g + b


def _side_cast_specs(sources, n_steps, step_index):
    in_specs, out_specs, out_shapes = [], [], []
    for arr, layer in sources:
        rows, cols = arr.shape[-2:]
        slab = rows // n_steps
        assert slab * n_steps == rows and slab % BF16_ROWS == 0, (rows, n_steps)
        if layer is None:
            in_specs.append(pl.BlockSpec((slab, cols), lambda *g: (step_index(*g), 0)))
        else:
            in_specs.append(pl.BlockSpec(
                (None, slab, cols), lambda *g, layer=layer: (layer, step_index(*g), 0)))
        out_specs.append(pl.BlockSpec((slab, cols), lambda *g: (step_index(*g), 0)))
        out_shapes.append(jax.ShapeDtypeStruct((rows, cols), BF16))
    return in_specs, out_specs, out_shapes


def _side_cast(side_in_refs, side_out_refs):
    for src_ref, dst_ref in zip(side_in_refs, side_out_refs):
        dst_ref[...] = src_ref[...].astype(BF16)


def _sgu_kernel(n_side, x_ref, w_ref, ng_ref, nb_ref, ws_ref, bs_ref, *refs):
    side_in, (y_ref, *side_out), (xb_ref, vg_ref, wc_ref) = (
        refs[:n_side], refs[n_side:2 * n_side + 1], refs[2 * n_side + 1:])

    @pl.when(pl.program_id(0) == 0)
    def _():
        row = lax.broadcasted_iota(jnp.int32, (SGU_CHUNK, SGU_CHUNK), 0)
        col = lax.broadcasted_iota(jnp.int32, (SGU_CHUNK, SGU_CHUNK), 1)
        keep = (col <= row)[None]
        wc_ref[...] = jnp.where(keep, ws_ref[...], 0.0).astype(BF16)

    tm = x_ref.shape[0]
    n_pieces = WIDTH // MXU_COLS
    gw = WIDTH // SGU_GROUPS
    xb_ref[...] = x_ref[...].astype(BF16)

    row_sum = jnp.zeros((tm, 1), F32)
    for c in range(n_pieces):
        cols = slice(c * MXU_COLS, (c + 1) * MXU_COLS)
        vc = jax.nn.gelu(_dot(xb_ref[...], w_ref[:, WIDTH + c * MXU_COLS:WIDTH + (c + 1) * MXU_COLS]))
        vg_ref[:, cols] = vc
        row_sum = row_sum + jnp.sum(vc, axis=-1, keepdims=True)
    _side_cast(side_in, side_out)
    mu = row_sum * (1.0 / WIDTH)
    sq_sum = jnp.zeros((tm, 1), F32)
    for c in range(n_pieces):
        cols = slice(c * MXU_COLS, (c + 1) * MXU_COLS)
        dc = vg_ref[:, cols] - mu
        sq_sum = sq_sum + jnp.sum(dc * dc, axis=-1, keepdims=True)
    rstd = lax.rsqrt(sq_sum * (1.0 / WIDTH) + LN_EPS)

    def uz_piece(c):
        u = jax.nn.gelu(_dot(xb_ref[...], w_ref[:, c * MXU_COLS:(c + 1) * MXU_COLS]))
        z = _dot(xb_ref[...], w_ref[:, 2 * WIDTH + c * MXU_COLS:2 * WIDTH + (c + 1) * MXU_COLS])
        return u * _silu(z)

    uz_next = uz_piece(0)
    for c in range(n_pieces):
        cols = slice(c * MXU_COLS, (c + 1) * MXU_COLS)
        uz = uz_next
        if c + 1 < n_pieces:
            uz_next = uz_piece(c + 1)
        vn = ((vg_ref[:, cols] - mu) * rstd * ng_ref[:, cols] + nb_ref[:, cols]).astype(BF16)
        for k in range(tm // SGU_CHUNK):
            rows = slice(k * SGU_CHUNK, (k + 1) * SGU_CHUNK)
            for gi in range(MXU_COLS // gw):
                g = c * (MXU_COLS // gw) + gi
                gcols = slice(gi * gw, (gi + 1) * gw)
                s = _dot(wc_ref[g], vn[rows, gcols]) + bs_ref[:, g:g + 1]
                y_ref[rows, c * MXU_COLS + gi * gw:c * MXU_COLS + (gi + 1) * gw] = (
                    uz[rows, gcols] * s).astype(BF16)


def _sgu(x, w, ng, nb, ws, bs_t, tm, side_sources):
    n = x.shape[0]
    row = lambda i: (i, 0)
    fix2 = lambda i: (0, 0)
    side_in_specs, side_out_specs, side_out_shapes = _side_cast_specs(
        side_sources, n // tm, lambda i: i)
    return pl.pallas_call(
        partial(_sgu_kernel, len(side_sources)),
        out_shape=[jax.ShapeDtypeStruct((n, WIDTH), BF16)] + side_out_shapes,
        grid=(n // tm,),
        in_specs=[
            pl.BlockSpec((tm, D_MODEL), row),
            pl.BlockSpec((D_MODEL, 3 * WIDTH), fix2, pipeline_mode=pl.Buffered(1)),
            pl.BlockSpec((1, WIDTH), fix2),
            pl.BlockSpec((1, WIDTH), fix2),
            pl.BlockSpec((SGU_GROUPS, SGU_CHUNK, SGU_CHUNK), lambda i: (0, 0, 0)),
            pl.BlockSpec((SGU_CHUNK, SGU_GROUPS), fix2),
        ] + side_in_specs,
        out_specs=[pl.BlockSpec((tm, WIDTH), row)] + side_out_specs,
        scratch_shapes=[pltpu.VMEM((tm, D_MODEL), BF16),
                        pltpu.VMEM((tm, WIDTH), F32),
                        pltpu.VMEM((SGU_GROUPS, SGU_CHUNK, SGU_CHUNK), BF16)],
        compiler_params=pltpu.CompilerParams(
            dimension_semantics=("arbitrary",), vmem_limit_bytes=VMEM_LIMIT),
        name="sgu",
    )(x, w, ng, nb, ws, bs_t, *[arr for arr, _ in side_sources])


def _tail_kernel(y_ref, x_ref, p_ref, wo_ref, g_ref, b_ref, wg_ref, wp_ref, o_ref,
                 maybe_ob_ref=None):
    tm, d = o_ref.shape
    n_pieces = tm // TAIL_ROWS

    def pre_norm(r):
        rows = slice(r * TAIL_ROWS, (r + 1) * TAIL_ROWS)
        return ALPHA * x_ref[rows, :] + _dot(y_ref[rows, :], wo_ref[...])

    t = pre_norm(0)
    for r in range(n_pieces):
        rows = slice(r * TAIL_ROWS, (r + 1) * TAIL_ROWS)
        t_next = pre_norm(r + 1) if r + 1 < n_pieces else None
        xn = _layer_norm(t, g_ref[...], b_ref[...])
        xnb = xn.astype(BF16)
        pb = p_ref[rows, :].astype(BF16)
        for c in range(d // MXU_COLS):
            cols = slice(c * MXU_COLS, (c + 1) * MXU_COLS)
            gate = jax.nn.sigmoid(_dot(xnb, wg_ref[:, cols]))
            proj = _dot(pb, wp_ref[:, cols])
            out = xn[:, cols] + gate * proj
            o_ref[rows, cols] = out
            if maybe_ob_ref is not None:
                maybe_ob_ref[rows, cols] = out.astype(BF16)
        t = t_next


def _tail(layer, y, x, p, wo, g, b, wg, wp, tm, with_bf16_copy):
    n = x.shape[0]
    row = lambda i: (i, 0)
    fix = lambda i: (layer, 0, 0)
    fix2 = lambda i: (0, 0)
    once = pl.Buffered(1)
    out_shape = [jax.ShapeDtypeStruct((n, D_MODEL), F32)]
    out_specs = [pl.BlockSpec((tm, D_MODEL), row)]
    if with_bf16_copy:
        out_shape.append(jax.ShapeDtypeStruct((n, D_MODEL), BF16))
        out_specs.append(pl.BlockSpec((tm, D_MODEL), row))
    return pl.pallas_call(
        _tail_kernel,
        out_shape=out_shape,
        grid=(n // tm,),
        in_specs=[
            pl.BlockSpec((tm, WIDTH), row),
            pl.BlockSpec((tm, D_MODEL), row),
            pl.BlockSpec((None, tm, PLE_DIM), lambda i: (layer, i, 0)),
            pl.BlockSpec((WIDTH, D_MODEL), fix2, pipeline_mode=once),
            pl.BlockSpec((None, 1, D_MODEL), fix),
            pl.BlockSpec((None, 1, D_MODEL), fix),
            pl.BlockSpec((D_MODEL, D_MODEL), fix2, pipeline_mode=once),
            pl.BlockSpec((None, PLE_DIM, D_MODEL), fix, pipeline_mode=once),
        ],
        out_specs=out_specs,
        compiler_params=pltpu.CompilerParams(
            dimension_semantics=("arbitrary",), vmem_limit_bytes=VMEM_LIMIT),
        name="layer_tail",
    )(y, x, p, wo, g, b, wg, wp)


def _moba_in_kernel(n_side, x_ref, wq_ref, wk_ref, wv_ref, wz_ref, *refs):
    side_in, (q_ref, k_ref, v_ref, zs_ref, *side_out), (wb_ref,) = (
        refs[:n_side], refs[n_side:2 * n_side + 4], refs[2 * n_side + 4:])

    @pl.when(pl.program_id(1) == 0)
    def _():
        for k, w_ref in enumerate((wq_ref, wk_ref, wv_ref, wz_ref)):
            wb_ref[k] = w_ref[...].astype(BF16)

    tm = x_ref.shape[0]
    for r in range(tm // EPILOGUE_ROWS):
        rows = slice(r * EPILOGUE_ROWS, (r + 1) * EPILOGUE_ROWS)
        xb = x_ref[rows, :]
        zs_ref[rows, :] = _silu(_dot(xb, wb_ref[3])).astype(BF16)
        q_ref[rows, :] = (_dot(xb, wb_ref[0]) * (HEAD_DIM ** -0.5 * LOG2_E)).astype(BF16)
        k_ref[rows, :] = _dot(xb, wb_ref[1]).astype(BF16)
        v_ref[rows, :] = _dot(xb, wb_ref[2]).astype(BF16)
        if r == 0:
            _side_cast(side_in, side_out)


def _moba_in(xb, w, tm, side_sources):
    n = xb.shape[0]
    tn = MXU_COLS
    nj, ni = WIDTH // tn, n // tm
    out = jax.ShapeDtypeStruct((n, WIDTH), BF16)
    side_in_specs, side_out_specs, side_out_shapes = _side_cast_specs(
        side_sources, nj * ni, lambda j, i: j * ni + i)

    def wspec(k):
        return pl.BlockSpec((D_MODEL, tn), lambda j, i: (0, j + k * nj))

    ospec = pl.BlockSpec((tm, tn), lambda j, i: (i, j))
    return pl.pallas_call(
        partial(_moba_in_kernel, len(side_sources)),
        out_shape=[out, out, out, out] + side_out_shapes,
        grid=(nj, ni),
        in_specs=[pl.BlockSpec((tm, D_MODEL), lambda j, i: (i, 0)),
                  wspec(0), wspec(1), wspec(2), wspec(3)] + side_in_specs,
        out_specs=[ospec, ospec, ospec, ospec] + side_out_specs,
        scratch_shapes=[pltpu.VMEM((4, D_MODEL, tn), BF16)],
        compiler_params=pltpu.CompilerParams(
            dimension_semantics=("arbitrary", "arbitrary"),
            vmem_limit_bytes=VMEM_LIMIT),
        name="moba_in",
    )(xb, w, w, w, w, *[arr for arr, _ in side_sources])


def _moba_attn_kernel(q_ref, k_ref, v_ref, zs_ref, y_ref, vt_ref, s_ref):
    seq = k_ref.shape[0]
    n_heads = k_ref.shape[1] // HEAD_DIM
    nb = seq // MOBA_BLOCK
    blk = MOBA_BLOCK
    neg_inf = jnp.float32(-jnp.inf)

    key_i = lax.broadcasted_iota(jnp.int32, (blk, blk), 0)
    qry_i = lax.broadcasted_iota(jnp.int32, (blk, blk), 1)
    causal_t = key_i <= qry_i
    blk_i = lax.broadcasted_iota(jnp.int32, (nb, blk), 0)

    def hcols(h):
        return slice(h * HEAD_DIM, (h + 1) * HEAD_DIM)

    def qblock(own, h):
        return q_ref[own * blk:(own + 1) * blk, hcols(h)]

    def scores(own, h):
        q = qblock(own, h)
        for n in range(own + 1):
            rows = slice(n * blk, (n + 1) * blk)
            s_ref[h, own % 2, rows, :] = _dot_nt(k_ref[rows, hcols(h)], q)

    def stats(own, h):
        s_t = s_ref.at[h, own % 2]
        bias = None
        if own > MOBA_TOP_K:
            gate = _dot_nt(k_means[h], qblock(own, h).astype(F32))
            gate = jnp.where(blk_i < own, gate, neg_inf)
            cnt = jnp.zeros((nb, blk), jnp.int32)
            for m in range(own):
                gm = gate[m:m + 1, :]
                beats = (gm > gate) | ((gm == gate) & (m < blk_i))
                cnt = cnt + jnp.where(beats, 1, 0)
            bias = jnp.where((cnt < MOBA_TOP_K) & (blk_i < own), 0.0, neg_inf)

        own_rows = slice(own * blk, (own + 1) * blk)
        s_own = jnp.where(causal_t, s_t[own_rows, :], neg_inf)
        s_t[own_rows, :] = s_own
        m_q = jnp.max(s_own, axis=0, keepdims=True)
        for n in range(own):
            bm = jnp.max(s_t[n * blk:(n + 1) * blk, :], axis=0, keepdims=True)
            if bias is not None:
                bm = bm + bias[n:n + 1, :]
            m_q = jnp.maximum(m_q, bm)
        return m_q, bias

    def exp_pv(own, h, m_q, bias):
        s_t = s_ref.at[h, own % 2]
        acc = jnp.zeros((HEAD_DIM + ONES_ROWS, blk), F32)
        for n in range(own + 1):
            shift = m_q
            if bias is not None and n < own:
                shift = m_q - bias[n:n + 1, :]
            pn = jnp.exp2(s_t[n * blk:(n + 1) * blk, :] - shift)
            acc = acc + _dot(vt_ref[h, :, n * blk:(n + 1) * blk], pn.astype(BF16))
        o = (acc[0:HEAD_DIM, :] / acc[HEAD_DIM:HEAD_DIM + 1, :]).T
        rows = slice(own * blk, (own + 1) * blk)
        y_ref[rows, hcols(h)] = (o * zs_ref[rows, hcols(h)].astype(F32)).astype(BF16)

    assert n_heads >= 2
    units = [(own, h) for own in reversed(range(nb)) for h in range(n_heads)]
    scores(*units[0])
    scores(*units[1])

    k_means = []
    for h in range(n_heads):
        for n in range(nb):
            vt_ref[h, 0:HEAD_DIM, n * blk:(n + 1) * blk] = (
                v_ref[n * blk:(n + 1) * blk, hcols(h)].astype(F32).T.astype(BF16))
        vt_ref[h, HEAD_DIM:HEAD_DIM + ONES_ROWS, :] = jnp.ones((ONES_ROWS, seq), BF16)
        kf = k_ref[:, hcols(h)].astype(F32)
        k_means.append(jnp.mean(kf.reshape(nb, blk, HEAD_DIM), axis=1))

    st = stats(*units[0])
    for i, unit in enumerate(units):
        if i + 2 < len(units):
            scores(*units[i + 2])
        st_next = stats(*units[i + 1]) if i + 1 < len(units) else None
        exp_pv(*unit, *st)
        st = st_next


def _moba_attn(q, k, v, zs, batch, seq, heads_per_step):
    spec = pl.BlockSpec((seq, heads_per_step * HEAD_DIM), lambda b, h: (b, h))
    return pl.pallas_call(
        _moba_attn_kernel,
        out_shape=jax.ShapeDtypeStruct((batch * seq, WIDTH), BF16),
        grid=(batch, HEADS // heads_per_step),
        in_specs=[spec, spec, spec, spec],
        out_specs=spec,
        scratch_shapes=[pltpu.VMEM((heads_per_step, HEAD_DIM + ONES_ROWS, seq), BF16),
                        pltpu.VMEM((heads_per_step, 2, seq, MOBA_BLOCK), F32)],
        compiler_params=pltpu.CompilerParams(
            dimension_semantics=("arbitrary", "arbitrary"),
            vmem_limit_bytes=VMEM_LIMIT),
        name="moba_attn",
    )(q, k, v, zs)


def kernel(x, p, w_in_a, sgu_norm_g, sgu_norm_b, w_s, b_s, w_in_b, w_out, ln_g, ln_b,
           w_ple_gate, w_ple_proj):
    batch, seq, d = x.shape
    n = batch * seq
    x2 = x.reshape(n, d)
    p2 = p.reshape(DEPTH, n, PLE_DIM)

    ln_g3, ln_b3 = ln_g[:, None, :], ln_b[:, None, :]

    y, wo0, wg0, wp_flat = _sgu(
        x2, w_in_a[0].astype(BF16), sgu_norm_g[0][None], sgu_norm_b[0][None], w_s[0], b_s[0].T,
        tm=512,
        side_sources=[(w_out, 0), (w_ple_gate, 0),
                      (w_ple_proj.reshape(DEPTH * PLE_DIM, d), None)])
    wp = wp_flat.reshape(DEPTH, PLE_DIM, d)
    x2, x2b = _tail(0, y, x2, p2, wo0, ln_g3, ln_b3, wg0, wp, tm=512, with_bf16_copy=True)

    q, k, v, zs, wo1, wg1 = _moba_in(
        x2b, w_in_b[0], tm=2048, side_sources=[(w_out, 1), (w_ple_gate, 1)])
    y = _moba_attn(q, k, v, zs, batch, seq, heads_per_step=4)
    (x2,) = _tail(1, y, x2, p2, wo1, ln_g3, ln_b3, wg1, wp, tm=512, with_bf16_copy=False)

    return x2.reshape(batch, seq, d)
```

```python
from functools import partial

import jax
import jax.numpy as jnp
from jax import lax
from jax.experimental import pallas as pl
from jax.experimental.pallas import tpu as pltpu

D_MODEL = 2048
WIDTH = 2048
SGU_CHUNK = 128
SGU_GROUPS = 16
HEADS = 16
HEAD_DIM = 128
MOBA_BLOCK = 256
MOBA_TOP_K = 3
PLE_DIM = 256
LN_EPS = 1e-5
DEPTH = 2
ALPHA = (2 * DEPTH) ** 0.25
LOG2_E = 1.4426950408889634
MXU_COLS = 256
EPILOGUE_ROWS = 512
TAIL_ROWS = 256
SGU_ROWS = 512
BF16_ROWS = 16
ONES_ROWS = BF16_ROWS

BF16 = jnp.bfloat16
F32 = jnp.float32

VMEM_LIMIT = 56 * 1024 * 1024


def _dot(a, b):
    return jnp.dot(a, b, preferred_element_type=F32)


def _dot_nt(a, b):
    return lax.dot_general(a, b, (((1,), (1,)), ((), ())), preferred_element_type=F32)


def _silu(z):
    return z * jax.nn.sigmoid(z)


def _layer_norm(x, g, b):
    mu = jnp.mean(x, axis=-1, keepdims=True)
    xc = x - mu
    var = jnp.mean(xc * xc, axis=-1, keepdims=True)
    return xc * lax.rsqrt(var + LN_EPS) * g + b


def _side_cast_specs(sources, n_steps, step_index):
    in_specs, out_specs, out_shapes = [], [], []
    for arr, layer in sources:
        rows, cols = arr.shape[-2:]
        slab = rows // n_steps
        assert slab * n_steps == rows and slab % BF16_ROWS == 0, (rows, n_steps)
        if layer is None:
            in_specs.append(pl.BlockSpec((slab, cols), lambda *g: (step_index(*g), 0)))
        else:
            in_specs.append(pl.BlockSpec(
                (None, slab, cols), lambda *g, layer=layer: (layer, step_index(*g), 0)))
        out_specs.append(pl.BlockSpec((slab, cols), lambda *g: (step_index(*g), 0)))
        out_shapes.append(jax.ShapeDtypeStruct((rows, cols), BF16))
    return in_specs, out_specs, out_shapes


def _side_cast(side_in_refs, side_out_refs):
    for src_ref, dst_ref in zip(side_in_refs, side_out_refs):
        dst_ref[...] = src_ref[...].astype(BF16)


def _sgu_kernel(n_side, x_ref, w_ref, ng_ref, nb_ref, ws_ref, bs_ref, *refs):
    side_in, (y_ref, *side_out), (xb_ref, vg_ref, wc_ref) = (
        refs[:n_side], refs[n_side:2 * n_side + 1], refs[2 * n_side + 1:])

    @pl.when(pl.program_id(0) == 0)
    def _():
        row = lax.broadcasted_iota(jnp.int32, (SGU_CHUNK, SGU_CHUNK), 0)
        col = lax.broadcasted_iota(jnp.int32, (SGU_CHUNK, SGU_CHUNK), 1)
        keep = (col <= row)[None]
        wc_ref[...] = jnp.where(keep, ws_ref[...], 0.0).astype(BF16)

    tm = x_ref.shape[0]
    n_pieces = WIDTH // MXU_COLS
    gw = WIDTH // SGU_GROUPS
    xb_ref[...] = x_ref[...].astype(BF16)

    def in_proj(path, c, epilogue):
        w_cols = slice(path * WIDTH + c * MXU_COLS, path * WIDTH + (c + 1) * MXU_COLS)
        parts = [epilogue(_dot(xb_ref[r * SGU_ROWS:(r + 1) * SGU_ROWS, :], w_ref[:, w_cols]))
                 for r in range(tm // SGU_ROWS)]
        return jnp.concatenate(parts, axis=0)

    row_sum = jnp.zeros((tm, 1), F32)
    for c in range(n_pieces):
        cols = slice(c * MXU_COLS, (c + 1) * MXU_COLS)
        vc = in_proj(1, c, jax.nn.gelu)
        vg_ref[:, cols] = vc
        row_sum = row_sum + jnp.sum(vc, axis=-1, keepdims=True)
    _side_cast(side_in, side_out)
    mu = row_sum * (1.0 / WIDTH)
    sq_sum = jnp.zeros((tm, 1), F32)
    for c in range(n_pieces):
        cols = slice(c * MXU_COLS, (c + 1) * MXU_COLS)
        dc = vg_ref[:, cols] - mu
        sq_sum = sq_sum + jnp.sum(dc * dc, axis=-1, keepdims=True)
    rstd = lax.rsqrt(sq_sum * (1.0 / WIDTH) + LN_EPS)

    def uz_piece(c):
        return in_proj(0, c, jax.nn.gelu) * in_proj(2, c, _silu)

    uz_next = uz_piece(0)
    for c in range(n_pieces):
        cols = slice(c * MXU_COLS, (c + 1) * MXU_COLS)
        uz = uz_next
        if c + 1 < n_pieces:
            uz_next = uz_piece(c + 1)
        vn = ((vg_ref[:, cols] - mu) * rstd * ng_ref[:, cols] + nb_ref[:, cols]).astype(BF16)
        for k in range(tm // SGU_CHUNK):
            rows = slice(k * SGU_CHUNK, (k + 1) * SGU_CHUNK)
            for gi in range(MXU_COLS // gw):
                g = c * (MXU_COLS // gw) + gi
                gcols = slice(gi * gw, (gi + 1) * gw)
                s = _dot(wc_ref[g], vn[rows, gcols]) + bs_ref[:, g:g + 1]
                y_ref[rows, c * MXU_COLS + gi * gw:c * MXU_COLS + (gi + 1) * gw] = (
                    uz[rows, gcols] * s).astype(BF16)


def _sgu(x, w, ng, nb, ws, bs_t, tm, side_sources):
    n = x.shape[0]
    row = lambda i: (i, 0)
    fix2 = lambda i: (0, 0)
    side_in_specs, side_out_specs, side_out_shapes = _side_cast_specs(
        side_sources, n // tm, lambda i: i)
    return pl.pallas_call(
        partial(_sgu_kernel, len(side_sources)),
        out_shape=[jax.ShapeDtypeStruct((n, WIDTH), BF16)] + side_out_shapes,
        grid=(n // tm,),
        in_specs=[
            pl.BlockSpec((tm, D_MODEL), row),
            pl.BlockSpec((D_MODEL, 3 * WIDTH), fix2, pipeline_mode=pl.Buffered(1)),
            pl.BlockSpec((1, WIDTH), fix2),
            pl.BlockSpec((1, WIDTH), fix2),
            pl.BlockSpec((SGU_GROUPS, SGU_CHUNK, SGU_CHUNK), lambda i: (0, 0, 0)),
            pl.BlockSpec((SGU_CHUNK, SGU_GROUPS), fix2),
        ] + side_in_specs,
        out_specs=[pl.BlockSpec((tm, WIDTH), row)] + side_out_specs,
        scratch_shapes=[pltpu.VMEM((tm, D_MODEL), BF16),
                        pltpu.VMEM((tm, WIDTH), F32),
                        pltpu.VMEM((SGU_GROUPS, SGU_CHUNK, SGU_CHUNK), BF16)],
        compiler_params=pltpu.CompilerParams(
            dimension_semantics=("arbitrary",), vmem_limit_bytes=VMEM_LIMIT),
        name="sgu",
    )(x, w, ng, nb, ws, bs_t, *[arr for arr, _ in side_sources])


def _tail_kernel(y_ref, x_ref, p_ref, wo_ref, g_ref, b_ref, wg_ref, wp_ref, o_ref,
                 maybe_ob_ref=None):
    tm, d = o_ref.shape
    n_pieces = tm // TAIL_ROWS

    def pre_norm(r):
        rows = slice(r * TAIL_ROWS, (r + 1) * TAIL_ROWS)
        for c in range(d // MXU_COLS):
            cols = slice(c * MXU_COLS, (c + 1) * MXU_COLS)
            o_ref[rows, cols] = ALPHA * x_ref[rows, cols] + _dot(y_ref[rows, :], wo_ref[:, cols])

    pre_norm(0)
    for r in range(n_pieces):
        rows = slice(r * TAIL_ROWS, (r + 1) * TAIL_ROWS)
        if r + 1 < n_pieces:
            pre_norm(r + 1)
        xn = _layer_norm(o_ref[rows, :], g_ref[...], b_ref[...])
        xnb = xn.astype(BF16)
        pb = p_ref[rows, :].astype(BF16)
        for c in range(d // MXU_COLS):
            cols = slice(c * MXU_COLS, (c + 1) * MXU_COLS)
            gate = jax.nn.sigmoid(_dot(xnb, wg_ref[:, cols]))
            proj = _dot(pb, wp_ref[:, cols])
            out = xn[:, cols] + gate * proj
            o_ref[rows, cols] = out
            if maybe_ob_ref is not None:
                maybe_ob_ref[rows, cols] = out.astype(BF16)


def _tail(layer, y, x, p, wo, g, b, wg, wp, tm, with_bf16_copy):
    n = x.shape[0]
    row = lambda i: (i, 0)
    fix = lambda i: (layer, 0, 0)
    fix2 = lambda i: (0, 0)
    once = pl.Buffered(1)
    out_shape = [jax.ShapeDtypeStruct((n, D_MODEL), F32)]
    out_specs = [pl.BlockSpec((tm, D_MODEL), row)]
    if with_bf16_copy:
        out_shape.append(jax.ShapeDtypeStruct((n, D_MODEL), BF16))
        out_specs.append(pl.BlockSpec((tm, D_MODEL), row))
    return pl.pallas_call(
        _tail_kernel,
        out_shape=out_shape,
        grid=(n // tm,),
        in_specs=[
            pl.BlockSpec((tm, WIDTH), row),
            pl.BlockSpec((tm, D_MODEL), row),
            pl.BlockSpec((None, tm, PLE_DIM), lambda i: (layer, i, 0)),
            pl.BlockSpec((WIDTH, D_MODEL), fix2, pipeline_mode=once),
            pl.BlockSpec((None, 1, D_MODEL), fix),
            pl.BlockSpec((None, 1, D_MODEL), fix),
            pl.BlockSpec((D_MODEL, D_MODEL), fix2, pipeline_mode=once),
            pl.BlockSpec((None, PLE_DIM, D_MODEL), fix, pipeline_mode=once),
        ],
        out_specs=out_specs,
        compiler_params=pltpu.CompilerParams(
            dimension_semantics=("arbitrary",), vmem_limit_bytes=VMEM_LIMIT),
        name="layer_tail",
    )(y, x, p, wo, g, b, wg, wp)


def _moba_in_kernel(n_side, x_ref, wq_ref, wk_ref, wv_ref, wz_ref, *refs):
    side_in, (q_ref, k_ref, v_ref, zs_ref, *side_out), (wb_ref,) = (
        refs[:n_side], refs[n_side:2 * n_side + 4], refs[2 * n_side + 4:])

    @pl.when(pl.program_id(1) == 0)
    def _():
        for k, w_ref in enumerate((wq_ref, wk_ref, wv_ref, wz_ref)):
            wb_ref[k] = w_ref[...].astype(BF16)

    tm = x_ref.shape[0]
    for r in range(tm // EPILOGUE_ROWS):
        rows = slice(r * EPILOGUE_ROWS, (r + 1) * EPILOGUE_ROWS)
        xb = x_ref[rows, :]
        zs_ref[rows, :] = _silu(_dot(xb, wb_ref[3])).astype(BF16)
        q_ref[rows, :] = (_dot(xb, wb_ref[0]) * (HEAD_DIM ** -0.5 * LOG2_E)).astype(BF16)
        k_ref[rows, :] = _dot(xb, wb_ref[1]).astype(BF16)
        v_ref[rows, :] = _dot(xb, wb_ref[2]).astype(BF16)
        if r == 0:
            _side_cast(side_in, side_out)


def _moba_in(xb, w, tm, side_sources):
    n = xb.shape[0]
    tn = MXU_COLS
    nj, ni = WIDTH // tn, n // tm
    out = jax.ShapeDtypeStruct((n, WIDTH), BF16)
    side_in_specs, side_out_specs, side_out_shapes = _side_cast_specs(
        side_sources, nj * ni, lambda j, i: j * ni + i)

    def wspec(k):
        return pl.BlockSpec((D_MODEL, tn), lambda j, i: (0, j + k * nj))

    ospec = pl.BlockSpec((tm, tn), lambda j, i: (i, j))
    return pl.pallas_call(
        partial(_moba_in_kernel, len(side_sources)),
        out_shape=[out, out, out, out] + side_out_shapes,
        grid=(nj, ni),
        in_specs=[pl.BlockSpec((tm, D_MODEL), lambda j, i: (i, 0)),
                  wspec(0), wspec(1), wspec(2), wspec(3)] + side_in_specs,
        out_specs=[ospec, ospec, ospec, ospec] + side_out_specs,
        scratch_shapes=[pltpu.VMEM((4, D_MODEL, tn), BF16)],
        compiler_params=pltpu.CompilerParams(
            dimension_semantics=("arbitrary", "arbitrary"),
            vmem_limit_bytes=VMEM_LIMIT),
        name="moba_in",
    )(xb, w, w, w, w, *[arr for arr, _ in side_sources])


def _moba_attn_kernel(q_ref, k_ref, v_ref, zs_ref, y_ref, vt_ref, s_ref):
    def s_buf(own, h):
        return s_ref.at[h, own % 2]

    seq = k_ref.shape[0]
    n_heads = k_ref.shape[1] // HEAD_DIM
    nb = seq // MOBA_BLOCK
    blk = MOBA_BLOCK
    neg_inf = jnp.float32(-jnp.inf)

    key_i = lax.broadcasted_iota(jnp.int32, (blk, blk), 0)
    qry_i = lax.broadcasted_iota(jnp.int32, (blk, blk), 1)
    causal_t = key_i <= qry_i
    blk_i = lax.broadcasted_iota(jnp.int32, (nb, blk), 0)

    def hcols(h):
        return slice(h * HEAD_DIM, (h + 1) * HEAD_DIM)

    def qblock(own, h):
        return q_ref[own * blk:(own + 1) * blk, hcols(h)]

    def scores(own, h):
        q = qblock(own, h)
        for n in range(own + 1):
            rows = slice(n * blk, (n + 1) * blk)
            s_buf(own, h)[rows, :] = _dot_nt(k_ref[rows, hcols(h)], q)

    def stats(own, h):
        s_t = s_buf(own, h)
        bias = None
        if own > MOBA_TOP_K:
            gate = _dot_nt(k_means[h], qblock(own, h).astype(F32))
            gate = jnp.where(blk_i < own, gate, neg_inf)
            cnt = jnp.zeros((nb, blk), jnp.int32)
            for m in range(own):
                gm = gate[m:m + 1, :]
                beats = (gm > gate) | ((gm == gate) & (m < blk_i))
                cnt = cnt + jnp.where(beats, 1, 0)
            bias = jnp.where((cnt < MOBA_TOP_K) & (blk_i < own), 0.0, neg_inf)

        own_rows = slice(own * blk, (own + 1) * blk)
        s_own = jnp.where(causal_t, s_t[own_rows, :], neg_inf)
        s_t[own_rows, :] = s_own
        m_q = jnp.max(s_own, axis=0, keepdims=True)
        for n in range(own):
            bm = jnp.max(s_t[n * blk:(n + 1) * blk, :], axis=0, keepdims=True)
            if bias is not None:
                bm = bm + bias[n:n + 1, :]
            m_q = jnp.maximum(m_q, bm)
        return m_q, bias

    def exp_pv(own, h, m_q, bias):
        s_t = s_buf(own, h)
        acc = jnp.zeros((HEAD_DIM + ONES_ROWS, blk), F32)
        for n in range(own + 1):
            shift = m_q
            if bias is not None and n < own:
                shift = m_q - bias[n:n + 1, :]
            pn = jnp.exp2(s_t[n * blk:(n + 1) * blk, :] - shift)
            acc = acc + _dot(vt_ref[h, :, n * blk:(n + 1) * blk], pn.astype(BF16))
        o = (acc[0:HEAD_DIM, :] / acc[HEAD_DIM:HEAD_DIM + 1, :]).T
        rows = slice(own * blk, (own + 1) * blk)
        y_ref[rows, hcols(h)] = (o * zs_ref[rows, hcols(h)].astype(F32)).astype(BF16)

    assert n_heads >= 2
    units = [(own, h) for own in reversed(range(nb)) for h in range(n_heads)]
    scores(*units[0])
    scores(*units[1])

    k_means = []
    for h in range(n_heads):
        for n in range(nb):
            vt_ref[h, 0:HEAD_DIM, n * blk:(n + 1) * blk] = (
                v_ref[n * blk:(n + 1) * blk, hcols(h)].astype(F32).T.astype(BF16))
        vt_ref[h, HEAD_DIM:HEAD_DIM + ONES_ROWS, :] = jnp.ones((ONES_ROWS, seq), BF16)
        kf = k_ref[:, hcols(h)].astype(F32)
        k_means.append(jnp.mean(kf.reshape(nb, blk, HEAD_DIM), axis=1))

    st = stats(*units[0])
    for i, unit in enumerate(units):
        if i + 2 < len(units):
            scores(*units[i + 2])
        st_next = stats(*units[i + 1]) if i + 1 < len(units) else None
        exp_pv(*unit, *st)
        st = st_next


def _moba_attn(q, k, v, zs, batch, seq, heads_per_step):
    spec = pl.BlockSpec((seq, heads_per_step * HEAD_DIM), lambda b, h: (b, h))
    return pl.pallas_call(
        _moba_attn_kernel,
        out_shape=jax.ShapeDtypeStruct((batch * seq, WIDTH), BF16),
        grid=(batch, HEADS // heads_per_step),
        in_specs=[spec, spec, spec, spec],
        out_specs=spec,
        scratch_shapes=[pltpu.VMEM((heads_per_step, HEAD_DIM + ONES_ROWS, seq), BF16),
                        pltpu.VMEM((heads_per_step, 2, seq, MOBA_BLOCK), F32)],
        compiler_params=pltpu.CompilerParams(
            dimension_semantics=("arbitrary", "arbitrary"),
            vmem_limit_bytes=VMEM_LIMIT),
        name="moba_attn",
    )(q, k, v, zs)


def kernel(x, p, w_in_a, sgu_norm_g, sgu_norm_b, w_s, b_s, w_in_b, w_out, ln_g, ln_b,
           w_ple_gate, w_ple_proj):
    batch, seq, d = x.shape
    n = batch * seq
    x2 = x.reshape(n, d)
    p2 = p.reshape(DEPTH, n, PLE_DIM)

    ln_g3, ln_b3 = ln_g[:, None, :], ln_b[:, None, :]

    y, wo0, wg0, wp_flat = _sgu(
        x2, w_in_a[0].astype(BF16), sgu_norm_g[0][None], sgu_norm_b[0][None], w_s[0], b_s[0].T,
        tm=512,
        side_sources=[(w_out, 0), (w_ple_gate, 0),
                      (w_ple_proj.reshape(DEPTH * PLE_DIM, d), None)])
    wp = wp_flat.reshape(DEPTH, PLE_DIM, d)
    x2, x2b = _tail(0, y, x2, p2, wo0, ln_g3, ln_b3, wg0, wp, tm=512, with_bf16_copy=True)

    q, k, v, zs, wo1, wg1 = _moba_in(
        x2b, w_in_b[0], tm=2048, side_sources=[(w_out, 1), (w_ple_gate, 1)])
    y = _moba_attn(q, k, v, zs, batch, seq, heads_per_step=4)
    (x2,) = _tail(1, y, x2, p2, wo1, ln_g3, ln_b3, wg1, wp, tm=512, with_bf16_copy=False)

    return x2.reshape(batch, seq, d)
```

```python
from functools import partial

import jax
import jax.numpy as jnp
from jax import lax
from jax.experimental import pallas as pl
from jax.experimental.pallas import tpu as pltpu

D_MODEL = 2048
WIDTH = 2048
SGU_CHUNK = 128
SGU_GROUPS = 16
HEADS = 16
HEAD_DIM = 128
MOBA_BLOCK = 256
MOBA_TOP_K = 3
PLE_DIM = 256
LN_EPS = 1e-5
DEPTH = 2
ALPHA = (2 * DEPTH) ** 0.25
LOG2_E = 1.4426950408889634
MXU_COLS = 256
EPILOGUE_ROWS = 512
TAIL_ROWS = 256
SGU_ROWS = 512
BF16_ROWS = 16
ONES_ROWS = BF16_ROWS

BF16 = jnp.bfloat16
F32 = jnp.float32

VMEM_LIMIT = 56 * 1024 * 1024


def _dot(a, b):
    return jnp.dot(a, b, preferred_element_type=F32)


def _dot_nt(a, b):
    return lax.dot_general(a, b, (((1,), (1,)), ((), ())), preferred_element_type=F32)


def _silu(z):
    return z * jax.nn.sigmoid(z)


def _layer_norm(x, g, b):
    mu = jnp.mean(x, axis=-1, keepdims=True)
    xc = x - mu
    var = jnp.mean(xc * xc, axis=-1, keepdims=True)
    return xc * lax.rsqrt(var + LN_EPS) * g + b


def _side_cast_specs(sources, n_steps, step_index):
    in_specs, out_specs, out_shapes = [], [], []
    for arr, layer in sources:
        rows, cols = arr.shape[-2:]
        slab = rows // n_steps
        assert slab * n_steps == rows and slab % BF16_ROWS == 0, (rows, n_steps)
        if layer is None:
            in_specs.append(pl.BlockSpec((slab, cols), lambda *g: (step_index(*g), 0)))
        else:
            in_specs.append(pl.BlockSpec(
                (None, slab, cols), lambda *g, layer=layer: (layer, step_index(*g), 0)))
        out_specs.append(pl.BlockSpec((slab, cols), lambda *g: (step_index(*g), 0)))
        out_shapes.append(jax.ShapeDtypeStruct((rows, cols), BF16))
    return in_specs, out_specs, out_shapes


def _side_cast(side_in_refs, side_out_refs):
    for src_ref, dst_ref in zip(side_in_refs, side_out_refs):
        dst_ref[...] = src_ref[...].astype(BF16)


def _sgu_kernel(n_side, x_ref, w_ref, ng_ref, nb_ref, ws_ref, bs_ref, *refs):
    side_in, (y_ref, *side_out), (xb_ref, vg_ref, wc_ref) = (
        refs[:n_side], refs[n_side:2 * n_side + 1], refs[2 * n_side + 1:])

    @pl.when(pl.program_id(0) == 0)
    def _():
        row = lax.broadcasted_iota(jnp.int32, (SGU_CHUNK, SGU_CHUNK), 0)
        col = lax.broadcasted_iota(jnp.int32, (SGU_CHUNK, SGU_CHUNK), 1)
        keep = (col <= row)[None]
        wc_ref[...] = jnp.where(keep, ws_ref[...], 0.0).astype(BF16)

    tm = x_ref.shape[0]
    n_pieces = WIDTH // MXU_COLS
    gw = WIDTH // SGU_GROUPS
    xb_ref[...] = x_ref[...].astype(BF16)

    def in_proj(path, c, epilogue):
        w_cols = slice(path * WIDTH + c * MXU_COLS, path * WIDTH + (c + 1) * MXU_COLS)
        parts = [epilogue(_dot(xb_ref[r * SGU_ROWS:(r + 1) * SGU_ROWS, :], w_ref[:, w_cols]))
                 for r in range(tm // SGU_ROWS)]
        return jnp.concatenate(parts, axis=0)

    row_sum = jnp.zeros((tm, 1), F32)
    for c in range(n_pieces):
        cols = slice(c * MXU_COLS, (c + 1) * MXU_COLS)
        vc = in_proj(1, c, jax.nn.gelu)
        vg_ref[:, cols] = vc
        row_sum = row_sum + jnp.sum(vc, axis=-1, keepdims=True)
    _side_cast(side_in, side_out)
    mu = row_sum * (1.0 / WIDTH)
    sq_sum = jnp.zeros((tm, 1), F32)
    for c in range(n_pieces):
        cols = slice(c * MXU_COLS, (c + 1) * MXU_COLS)
        dc = vg_ref[:, cols] - mu
        sq_sum = sq_sum + jnp.sum(dc * dc, axis=-1, keepdims=True)
    rstd = lax.rsqrt(sq_sum * (1.0 / WIDTH) + LN_EPS)

    def uz_piece(c):
        return in_proj(0, c, jax.nn.gelu) * in_proj(2, c, _silu)

    uz_next = uz_piece(0)
    for c in range(n_pieces):
        cols = slice(c * MXU_COLS, (c + 1) * MXU_COLS)
        uz = uz_next
        if c + 1 < n_pieces:
            uz_next = uz_piece(c + 1)
        vn = ((vg_ref[:, cols] - mu) * rstd * ng_ref[:, cols] + nb_ref[:, cols]).astype(BF16)
        for k in range(tm // SGU_CHUNK):
            rows = slice(k * SGU_CHUNK, (k + 1) * SGU_CHUNK)
            for gi in range(MXU_COLS // gw):
                g = c * (MXU_COLS // gw) + gi
                gcols = slice(gi * gw, (gi + 1) * gw)
                s = _dot(wc_ref[g], vn[rows, gcols]) + bs_ref[:, g:g + 1]
                y_ref[rows, c * MXU_COLS + gi * gw:c * MXU_COLS + (gi + 1) * gw] = (
                    uz[rows, gcols] * s).astype(BF16)


def _sgu(x, w, ng, nb, ws, bs_t, tm, side_sources):
    n = x.shape[0]
    row = lambda i: (i, 0)
    fix2 = lambda i: (0, 0)
    side_in_specs, side_out_specs, side_out_shapes = _side_cast_specs(
        side_sources, n // tm, lambda i: i)
    return pl.pallas_call(
        partial(_sgu_kernel, len(side_sources)),
        out_shape=[jax.ShapeDtypeStruct((n, WIDTH), BF16)] + side_out_shapes,
        grid=(n // tm,),
        in_specs=[
            pl.BlockSpec((tm, D_MODEL), row),
            pl.BlockSpec((D_MODEL, 3 * WIDTH), fix2, pipeline_mode=pl.Buffered(1)),
            pl.BlockSpec((1, WIDTH), fix2),
            pl.BlockSpec((1, WIDTH), fix2),
            pl.BlockSpec((SGU_GROUPS, SGU_CHUNK, SGU_CHUNK), lambda i: (0, 0, 0)),
            pl.BlockSpec((SGU_CHUNK, SGU_GROUPS), fix2),
        ] + side_in_specs,
        out_specs=[pl.BlockSpec((tm, WIDTH), row)] + side_out_specs,
        scratch_shapes=[pltpu.VMEM((tm, D_MODEL), BF16),
                        pltpu.VMEM((tm, WIDTH), F32),
                        pltpu.VMEM((SGU_GROUPS, SGU_CHUNK, SGU_CHUNK), BF16)],
        compiler_params=pltpu.CompilerParams(
            dimension_semantics=("arbitrary",), vmem_limit_bytes=VMEM_LIMIT),
        name="sgu",
    )(x, w, ng, nb, ws, bs_t, *[arr for arr, _ in side_sources])


def _sgu_v_kernel(x_ref, wv_ref, vg_ref, xb_ref, mu_ref, rstd_ref, wvb_ref):
    @pl.when(pl.program_id(0) == 0)
    def _():
        wvb_ref[...] = wv_ref[...].astype(BF16)

    for r in range(x_ref.shape[0] // TAIL_ROWS):
        rows = slice(r * TAIL_ROWS, (r + 1) * TAIL_ROWS)
        xb = x_ref[rows, :].astype(BF16)
        xb_ref[rows, :] = xb
        pilot = d_sum = sq_sum = None
        for c in range(WIDTH // MXU_COLS):
            cols = slice(c * MXU_COLS, (c + 1) * MXU_COLS)
            vc = jax.nn.gelu(_dot(xb, wvb_ref[:, cols]))
            vg_ref[rows, cols] = vc.astype(BF16)
            if c == 0:
                pilot = jnp.mean(vc, axis=-1, keepdims=True)
            dc = vc - pilot
            d_part = jnp.sum(dc, axis=-1, keepdims=True)
            sq_part = jnp.sum(dc * dc, axis=-1, keepdims=True)
            d_sum = d_part if c == 0 else d_sum + d_part
            sq_sum = sq_part if c == 0 else sq_sum + sq_part
        d_mean = d_sum * (1.0 / WIDTH)
        mu_ref[rows, :] = pilot + d_mean
        rstd_ref[rows, :] = lax.rsqrt(sq_sum * (1.0 / WIDTH) - d_mean * d_mean + LN_EPS)


def _sgu_v(x, w, tm):
    n = x.shape[0]
    row = lambda i: (i, 0)
    stat = jax.ShapeDtypeStruct((n, 1), F32)
    return pl.pallas_call(
        _sgu_v_kernel,
        out_shape=[jax.ShapeDtypeStruct((n, WIDTH), BF16), jax.ShapeDtypeStruct((n, D_MODEL), BF16),
                   stat, stat],
        grid=(n // tm,),
        in_specs=[
            pl.BlockSpec((tm, D_MODEL), row),
            pl.BlockSpec((D_MODEL, WIDTH), lambda i: (0, 1), pipeline_mode=pl.Buffered(1)),
        ],
        out_specs=[pl.BlockSpec((tm, WIDTH), row), pl.BlockSpec((tm, D_MODEL), row),
                   pl.BlockSpec((tm, 1), row), pl.BlockSpec((tm, 1), row)],
        scratch_shapes=[pltpu.VMEM((D_MODEL, WIDTH), BF16)],
        compiler_params=pltpu.CompilerParams(
            dimension_semantics=("arbitrary",), vmem_limit_bytes=VMEM_LIMIT),
        name="sgu_v",
    )(x, w)


def _sgu_uz_kernel(n_side, xb_ref, wu_ref, wz_ref, vg_ref, mu_ref, rstd_ref, ng_ref, nb_ref,
                   ws_ref, bs_ref, *refs):
    side_in, (y_ref, *side_out), (wb_ref, wc_ref) = (
        refs[:n_side], refs[n_side:2 * n_side + 1], refs[2 * n_side + 1:])

    @pl.when(pl.program_id(1) == 0)
    def _():
        wb_ref[0] = wu_ref[...].astype(BF16)
        wb_ref[1] = wz_ref[...].astype(BF16)
        row = lax.broadcasted_iota(jnp.int32, (SGU_CHUNK, SGU_CHUNK), 0)
        col = lax.broadcasted_iota(jnp.int32, (SGU_CHUNK, SGU_CHUNK), 1)
        wc_ref[...] = jnp.where((col <= row)[None], ws_ref[...], 0.0).astype(BF16)

    tm = xb_ref.shape[0]
    gw = WIDTH // SGU_GROUPS
    n_pieces = tm // EPILOGUE_ROWS

    def uz_piece(r):
        xb = xb_ref[r * EPILOGUE_ROWS:(r + 1) * EPILOGUE_ROWS, :]
        return jax.nn.gelu(_dot(xb, wb_ref[0])) * _silu(_dot(xb, wb_ref[1]))

    uz_next = uz_piece(0)
    for r in range(n_pieces):
        rows = slice(r * EPILOGUE_ROWS, (r + 1) * EPILOGUE_ROWS)
        uz = uz_next
        if r + 1 < n_pieces:
            uz_next = uz_piece(r + 1)
        vn = ((vg_ref[rows, :].astype(F32) - mu_ref[rows, :]) * rstd_ref[rows, :] * ng_ref[...]
              + nb_ref[...]).astype(BF16)
        for k in range(EPILOGUE_ROWS // SGU_CHUNK):
            krows = slice(k * SGU_CHUNK, (k + 1) * SGU_CHUNK)
            for gi in range(MXU_COLS // gw):
                gcols = slice(gi * gw, (gi + 1) * gw)
                s = _dot(wc_ref[gi], vn[krows, gcols]) + bs_ref[:, gi:gi + 1]
                y_ref[r * EPILOGUE_ROWS + k * SGU_CHUNK:r * EPILOGUE_ROWS + (k + 1) * SGU_CHUNK,
                      gcols] = (uz[krows, gcols] * s).astype(BF16)
        if r == 0:
            _side_cast(side_in, side_out)


def _sgu_uz(xb, w, vg, mu, rstd, ng, nb, ws, bs_tiles, tm, side_sources):
    n = xb.shape[0]
    tn = MXU_COLS
    nj, ni = WIDTH // tn, n // tm
    groups_per_tile = tn // (WIDTH // SGU_GROUPS)
    side_in_specs, side_out_specs, side_out_shapes = _side_cast_specs(
        side_sources, nj * ni, lambda j, i: j * ni + i)
    return pl.pallas_call(
        partial(_sgu_uz_kernel, len(side_sources)),
        out_shape=[jax.ShapeDtypeStruct((n, WIDTH), BF16)] + side_out_shapes,
        grid=(nj, ni),
        in_specs=[
            pl.BlockSpec((tm, D_MODEL), lambda j, i: (i, 0)),
            pl.BlockSpec((D_MODEL, tn), lambda j, i: (0, j)),
            pl.BlockSpec((D_MODEL, tn), lambda j, i: (0, j + 2 * nj)),
            pl.BlockSpec((tm, tn), lambda j, i: (i, j)),
            pl.BlockSpec((tm, 1), lambda j, i: (i, 0)),
            pl.BlockSpec((tm, 1), lambda j, i: (i, 0)),
            pl.BlockSpec((1, tn), lambda j, i: (0, j)),
            pl.BlockSpec((1, tn), lambda j, i: (0, j)),
            pl.BlockSpec((groups_per_tile, SGU_CHUNK, SGU_CHUNK), lambda j, i: (j, 0, 0)),
            pl.BlockSpec((None, SGU_CHUNK, groups_per_tile), lambda j, i: (j, 0, 0)),
        ] + side_in_specs,
        out_specs=[pl.BlockSpec((tm, tn), lambda j, i: (i, j))] + side_out_specs,
        scratch_shapes=[pltpu.VMEM((2, D_MODEL, tn), BF16),
                        pltpu.VMEM((groups_per_tile, SGU_CHUNK, SGU_CHUNK), BF16)],
        compiler_params=pltpu.CompilerParams(
            dimension_semantics=("arbitrary", "arbitrary"),
            vmem_limit_bytes=VMEM_LIMIT),
        name="sgu_uz",
    )(xb, w, w, vg, mu, rstd, ng, nb, ws, bs_tiles, *[arr for arr, _ in side_sources])


def _tail_kernel(y_ref, x_ref, p_ref, wo_ref, g_ref, b_ref, wg_ref, wp_ref, o_ref,
                 maybe_ob_ref=None):
    tm, d = o_ref.shape
    n_pieces = tm // TAIL_ROWS

    def pre_norm(r):
        rows = slice(r * TAIL_ROWS, (r + 1) * TAIL_ROWS)
        for c in range(d // MXU_COLS):
            cols = slice(c * MXU_COLS, (c + 1) * MXU_COLS)
            o_ref[rows, cols] = ALPHA * x_ref[rows, cols] + _dot(y_ref[rows, :], wo_ref[:, cols])

    pre_norm(0)
    for r in range(n_pieces):
        rows = slice(r * TAIL_ROWS, (r + 1) * TAIL_ROWS)
        if r + 1 < n_pieces:
            pre_norm(r + 1)
        xn = _layer_norm(o_ref[rows, :], g_ref[...], b_ref[...])
        xnb = xn.astype(BF16)
        pb = p_ref[rows, :].astype(BF16)
        for c in range(d // MXU_COLS):
            cols = slice(c * MXU_COLS, (c + 1) * MXU_COLS)
            gate = jax.nn.sigmoid(_dot(xnb, wg_ref[:, cols]))
            proj = _dot(pb, wp_ref[:, cols])
            out = xn[:, cols] + gate * proj
            o_ref[rows, cols] = out
            if maybe_ob_ref is not None:
                maybe_ob_ref[rows, cols] = out.astype(BF16)


def _tail(layer, y, x, p, wo, g, b, wg, wp, tm, with_bf16_copy):
    n = x.shape[0]
    row = lambda i: (i, 0)
    fix = lambda i: (layer, 0, 0)
    fix2 = lambda i: (0, 0)
    once = pl.Buffered(1)
    out_shape = [jax.ShapeDtypeStruct((n, D_MODEL), F32)]
    out_specs = [pl.BlockSpec((tm, D_MODEL), row)]
    if with_bf16_copy:
        out_shape.append(jax.ShapeDtypeStruct((n, D_MODEL), BF16))
        out_specs.append(pl.BlockSpec((tm, D_MODEL), row))
    return pl.pallas_call(
        _tail_kernel,
        out_shape=out_shape,
        grid=(n // tm,),
        in_specs=[
            pl.BlockSpec((tm, WIDTH), row),
            pl.BlockSpec((tm, D_MODEL), row),
            pl.BlockSpec((None, tm, PLE_DIM), lambda i: (layer, i, 0)),
            pl.BlockSpec((WIDTH, D_MODEL), fix2, pipeline_mode=once),
            pl.BlockSpec((None, 1, D_MODEL), fix),
            pl.BlockSpec((None, 1, D_MODEL), fix),
            pl.BlockSpec((D_MODEL, D_MODEL), fix2, pipeline_mode=once),
            pl.BlockSpec((None, PLE_DIM, D_MODEL), fix, pipeline_mode=once),
        ],
        out_specs=out_specs,
        compiler_params=pltpu.CompilerParams(
            dimension_semantics=("arbitrary",), vmem_limit_bytes=VMEM_LIMIT),
        name="layer_tail",
    )(y, x, p, wo, g, b, wg, wp)


def _moba_in_kernel(n_side, x_ref, wq_ref, wk_ref, wv_ref, wz_ref, *refs):
    side_in, (q_ref, k_ref, v_ref, zs_ref, *side_out), (wb_ref,) = (
        refs[:n_side], refs[n_side:2 * n_side + 4], refs[2 * n_side + 4:])

    @pl.when(pl.program_id(1) == 0)
    def _():
        for k, w_ref in enumerate((wq_ref, wk_ref, wv_ref, wz_ref)):
            wb_ref[k] = w_ref[...].astype(BF16)

    tm = x_ref.shape[0]
    for r in range(tm // EPILOGUE_ROWS):
        rows = slice(r * EPILOGUE_ROWS, (r + 1) * EPILOGUE_ROWS)
        xb = x_ref[rows, :]
        zs_ref[rows, :] = _silu(_dot(xb, wb_ref[3])).astype(BF16)
        q_ref[rows, :] = (_dot(xb, wb_ref[0]) * (HEAD_DIM ** -0.5 * LOG2_E)).astype(BF16)
        k_ref[rows, :] = _dot(xb, wb_ref[1]).astype(BF16)
        v_ref[rows, :] = _dot(xb, wb_ref[2]).astype(BF16)
        if r == 0:
            _side_cast(side_in, side_out)


def _moba_in(xb, w, tm, side_sources):
    n = xb.shape[0]
    tn = MXU_COLS
    nj, ni = WIDTH // tn, n // tm
    out = jax.ShapeDtypeStruct((n, WIDTH), BF16)
    side_in_specs, side_out_specs, side_out_shapes = _side_cast_specs(
        side_sources, nj * ni, lambda j, i: j * ni + i)

    def wspec(k):
        return pl.BlockSpec((D_MODEL, tn), lambda j, i: (0, j + k * nj))

    ospec = pl.BlockSpec((tm, tn), lambda j, i: (i, j))
    return pl.pallas_call(
        partial(_moba_in_kernel, len(side_sources)),
        out_shape=[out, out, out, out] + side_out_shapes,
        grid=(nj, ni),
        in_specs=[pl.BlockSpec((tm, D_MODEL), lambda j, i: (i, 0)),
                  wspec(0), wspec(1), wspec(2), wspec(3)] + side_in_specs,
        out_specs=[ospec, ospec, ospec, ospec] + side_out_specs,
        scratch_shapes=[pltpu.VMEM((4, D_MODEL, tn), BF16)],
        compiler_params=pltpu.CompilerParams(
            dimension_semantics=("arbitrary", "arbitrary"),
            vmem_limit_bytes=VMEM_LIMIT),
        name="moba_in",
    )(xb, w, w, w, w, *[arr for arr, _ in side_sources])


def _moba_attn_kernel(q_ref, k_ref, v_ref, zs_ref, y_ref, vt_ref):
    seq = k_ref.shape[0]
    n_heads = k_ref.shape[1] // HEAD_DIM
    nb = seq // MOBA_BLOCK
    blk = MOBA_BLOCK
    neg_inf = jnp.float32(-jnp.inf)

    key_i = lax.broadcasted_iota(jnp.int32, (blk, blk), 0)
    qry_i = lax.broadcasted_iota(jnp.int32, (blk, blk), 1)
    causal_t = key_i <= qry_i
    blk_i = lax.broadcasted_iota(jnp.int32, (nb, blk), 0)

    def hcols(h):
        return slice(h * HEAD_DIM, (h + 1) * HEAD_DIM)

    def qblock(own, h):
        return q_ref[own * blk:(own + 1) * blk, hcols(h)]

    def scores(own, h):
        q = qblock(own, h)
        return [_dot_nt(k_ref[n * blk:(n + 1) * blk, hcols(h)], q) for n in range(own + 1)]

    def stats(own, h, s_blocks):
        bias = None
        if own > MOBA_TOP_K:
            gate = _dot_nt(k_means[h], qblock(own, h).astype(F32))
            gate = jnp.where(blk_i < own, gate, neg_inf)
            cnt = jnp.zeros((nb, blk), jnp.int32)
            for m in range(own):
                gm = gate[m:m + 1, :]
                beats = (gm > gate) | ((gm == gate) & (m < blk_i))
                cnt = cnt + jnp.where(beats, 1, 0)
            bias = jnp.where((cnt < MOBA_TOP_K) & (blk_i < own), 0.0, neg_inf)

        s_blocks = s_blocks[:own] + [jnp.where(causal_t, s_blocks[own], neg_inf)]
        m_q = jnp.max(s_blocks[own], axis=0, keepdims=True)
        for n in range(own):
            bm = jnp.max(s_blocks[n], axis=0, keepdims=True)
            if bias is not None:
                bm = bm + bias[n:n + 1, :]
            m_q = jnp.maximum(m_q, bm)
        return s_blocks, m_q, bias

    def exp_pv(own, h, s_blocks, m_q, bias):
        acc = jnp.zeros((HEAD_DIM + ONES_ROWS, blk), F32)
        for n in range(own + 1):
            shift = m_q
            if bias is not None and n < own:
                shift = m_q - bias[n:n + 1, :]
            pn = jnp.exp2(s_blocks[n] - shift)
            acc = acc + _dot(vt_ref[h, :, n * blk:(n + 1) * blk], pn.astype(BF16))
        o = (acc[0:HEAD_DIM, :] / acc[HEAD_DIM:HEAD_DIM + 1, :]).T
        rows = slice(own * blk, (own + 1) * blk)
        y_ref[rows, hcols(h)] = (o * zs_ref[rows, hcols(h)].astype(F32)).astype(BF16)

    units = [(own, h) for own in reversed(range(nb)) for h in range(n_heads)]
    pending = [scores(*units[0]), scores(*units[1])]

    k_means = []
    for h in range(n_heads):
        for n in range(nb):
            vt_ref[h, 0:HEAD_DIM, n * blk:(n + 1) * blk] = (
                v_ref[n * blk:(n + 1) * blk, hcols(h)].astype(F32).T.astype(BF16))
        vt_ref[h, HEAD_DIM:HEAD_DIM + ONES_ROWS, :] = jnp.ones((ONES_ROWS, seq), BF16)
        kf = k_ref[:, hcols(h)].astype(F32)
        k_means.append(jnp.mean(kf.reshape(nb, blk, HEAD_DIM), axis=1))

    st = stats(*units[0], pending.pop(0))
    for i, unit in enumerate(units):
        if i + 2 < len(units):
            pending.append(scores(*units[i + 2]))
        st_next = stats(*units[i + 1], pending.pop(0)) if i + 1 < len(units) else None
        exp_pv(*unit, *st)
        st = st_next


def _moba_attn(q, k, v, zs, batch, seq, heads_per_step):
    spec = pl.BlockSpec((seq, heads_per_step * HEAD_DIM), lambda b, h: (b, h))
    return pl.pallas_call(
        _moba_attn_kernel,
        out_shape=jax.ShapeDtypeStruct((batch * seq, WIDTH), BF16),
        grid=(batch, HEADS // heads_per_step),
        in_specs=[spec, spec, spec, spec],
        out_specs=spec,
        scratch_shapes=[pltpu.VMEM((heads_per_step, HEAD_DIM + ONES_ROWS, seq), BF16)],
        compiler_params=pltpu.CompilerParams(
            dimension_semantics=("arbitrary", "arbitrary"),
            vmem_limit_bytes=VMEM_LIMIT),
        name="moba_attn",
    )(q, k, v, zs)


def kernel(x, p, w_in_a, sgu_norm_g, sgu_norm_b, w_s, b_s, w_in_b, w_out, ln_g, ln_b,
           w_ple_gate, w_ple_proj):
    batch, seq, d = x.shape
    n = batch * seq
    x2 = x.reshape(n, d)
    p2 = p.reshape(DEPTH, n, PLE_DIM)

    ln_g3, ln_b3 = ln_g[:, None, :], ln_b[:, None, :]

    vg, x0b, mu, rstd = _sgu_v(x2, w_in_a[0], tm=512)
    groups_per_tile = MXU_COLS // (WIDTH // SGU_GROUPS)
    bs_tiles = b_s[0].reshape(SGU_GROUPS // groups_per_tile, groups_per_tile, SGU_CHUNK)
    y, wo0, wg0, wp_flat = _sgu_uz(
        x0b, w_in_a[0], vg, mu, rstd, sgu_norm_g[0][None], sgu_norm_b[0][None], w_s[0],
        bs_tiles.transpose(0, 2, 1), tm=2048,
        side_sources=[(w_out, 0), (w_ple_gate, 0),
                      (w_ple_proj.reshape(DEPTH * PLE_DIM, d), None)])
    wp = wp_flat.reshape(DEPTH, PLE_DIM, d)
    x2, x2b = _tail(0, y, x2, p2, wo0, ln_g3, ln_b3, wg0, wp, tm=512, with_bf16_copy=True)

    q, k, v, zs, wo1, wg1 = _moba_in(
        x2b, w_in_b[0], tm=2048, side_sources=[(w_out, 1), (w_ple_gate, 1)])
    y = _moba_attn(q, k, v, zs, batch, seq, heads_per_step=4)
    (x2,) = _tail(1, y, x2, p2, wo1, ln_g3, ln_b3, wg1, wp, tm=512, with_bf16_copy=False)

    return x2.reshape(batch, seq, d)
```

```python
from functools import partial

import jax
import jax.numpy as jnp
from jax import lax
from jax.experimental import pallas as pl
from jax.experimental.pallas import tpu as pltpu

D_MODEL = 2048
WIDTH = 2048
SGU_CHUNK = 128
SGU_GROUPS = 16
HEADS = 16
HEAD_DIM = 128
MOBA_BLOCK = 256
MOBA_TOP_K = 3
PLE_DIM = 256
LN_EPS = 1e-5
DEPTH = 2
ALPHA = (2 * DEPTH) ** 0.25
LOG2_E = 1.4426950408889634
MXU_COLS = 256
LANES = 128
EPILOGUE_ROWS = 512
TAIL_ROWS = 256
SGU_ROWS = 512
BF16_ROWS = 16
ONES_ROWS = BF16_ROWS

BF16 = jnp.bfloat16
F32 = jnp.float32

VMEM_LIMIT = 56 * 1024 * 1024


def _dot(a, b):
    return jnp.dot(a, b, preferred_element_type=F32)


def _dot_nt(a, b):
    return lax.dot_general(a, b, (((1,), (1,)), ((), ())), preferred_element_type=F32)


def _silu(z):
    return z * jax.nn.sigmoid(z)


def _layer_norm(x, g, b):
    mu = jnp.mean(x, axis=-1, keepdims=True)
    xc = x - mu
    var = jnp.mean(xc * xc, axis=-1, keepdims=True)
    return xc * lax.rsqrt(var + LN_EPS) * g + b


def _side_cast_specs(sources, n_steps, step_index):
    in_specs, out_specs, out_shapes = [], [], []
    for arr, layer in sources:
        rows, cols = arr.shape[-2:]
        slab = rows // n_steps
        assert slab * n_steps == rows and slab % BF16_ROWS == 0, (rows, n_steps)
        if layer is None:
            in_specs.append(pl.BlockSpec((slab, cols), lambda *g: (step_index(*g), 0)))
        else:
            in_specs.append(pl.BlockSpec(
                (None, slab, cols), lambda *g, layer=layer: (layer, step_index(*g), 0)))
        out_specs.append(pl.BlockSpec((slab, cols), lambda *g: (step_index(*g), 0)))
        out_shapes.append(jax.ShapeDtypeStruct((rows, cols), BF16))
    return in_specs, out_specs, out_shapes


def _side_cast(side_in_refs, side_out_refs):
    for src_ref, dst_ref in zip(side_in_refs, side_out_refs):
        dst_ref[...] = src_ref[...].astype(BF16)


def _sgu_kernel(n_side, x_ref, w_ref, ng_ref, nb_ref, ws_ref, bs_ref, *refs):
    side_in, (y_ref, *side_out), (xb_ref, vg_ref, wc_ref) = (
        refs[:n_side], refs[n_side:2 * n_side + 1], refs[2 * n_side + 1:])

    @pl.when(pl.program_id(0) == 0)
    def _():
        row = lax.broadcasted_iota(jnp.int32, (SGU_CHUNK, SGU_CHUNK), 0)
        col = lax.broadcasted_iota(jnp.int32, (SGU_CHUNK, SGU_CHUNK), 1)
        keep = (col <= row)[None]
        wc_ref[...] = jnp.where(keep, ws_ref[...], 0.0).astype(BF16)

    tm = x_ref.shape[0]
    n_pieces = WIDTH // MXU_COLS
    gw = WIDTH // SGU_GROUPS
    xb_ref[...] = x_ref[...].astype(BF16)

    def in_proj(path, c, epilogue):
        w_cols = slice(path * WIDTH + c * MXU_COLS, path * WIDTH + (c + 1) * MXU_COLS)
        parts = [epilogue(_dot(xb_ref[r * SGU_ROWS:(r + 1) * SGU_ROWS, :], w_ref[:, w_cols]))
                 for r in range(tm // SGU_ROWS)]
        return jnp.concatenate(parts, axis=0)

    row_sum = jnp.zeros((tm, 1), F32)
    for c in range(n_pieces):
        cols = slice(c * MXU_COLS, (c + 1) * MXU_COLS)
        vc = in_proj(1, c, jax.nn.gelu)
        vg_ref[:, cols] = vc
        row_sum = row_sum + jnp.sum(vc, axis=-1, keepdims=True)
    _side_cast(side_in, side_out)
    mu = row_sum * (1.0 / WIDTH)
    sq_sum = jnp.zeros((tm, 1), F32)
    for c in range(n_pieces):
        cols = slice(c * MXU_COLS, (c + 1) * MXU_COLS)
        dc = vg_ref[:, cols] - mu
        sq_sum = sq_sum + jnp.sum(dc * dc, axis=-1, keepdims=True)
    rstd = lax.rsqrt(sq_sum * (1.0 / WIDTH) + LN_EPS)

    def uz_piece(c):
        return in_proj(0, c, jax.nn.gelu) * in_proj(2, c, _silu)

    uz_next = uz_piece(0)
    for c in range(n_pieces):
        cols = slice(c * MXU_COLS, (c + 1) * MXU_COLS)
        uz = uz_next
        if c + 1 < n_pieces:
            uz_next = uz_piece(c + 1)
        vn = ((vg_ref[:, cols] - mu) * rstd * ng_ref[:, cols] + nb_ref[:, cols]).astype(BF16)
        for k in range(tm // SGU_CHUNK):
            rows = slice(k * SGU_CHUNK, (k + 1) * SGU_CHUNK)
            for gi in range(MXU_COLS // gw):
                g = c * (MXU_COLS // gw) + gi
                gcols = slice(gi * gw, (gi + 1) * gw)
                s = _dot(wc_ref[g], vn[rows, gcols]) + bs_ref[:, g:g + 1]
                y_ref[rows, c * MXU_COLS + gi * gw:c * MXU_COLS + (gi + 1) * gw] = (
                    uz[rows, gcols] * s).astype(BF16)


def _sgu(x, w, ng, nb, ws, bs_t, tm, side_sources):
    n = x.shape[0]
    row = lambda i: (i, 0)
    fix2 = lambda i: (0, 0)
    side_in_specs, side_out_specs, side_out_shapes = _side_cast_specs(
        side_sources, n // tm, lambda i: i)
    return pl.pallas_call(
        partial(_sgu_kernel, len(side_sources)),
        out_shape=[jax.ShapeDtypeStruct((n, WIDTH), BF16)] + side_out_shapes,
        grid=(n // tm,),
        in_specs=[
            pl.BlockSpec((tm, D_MODEL), row),
            pl.BlockSpec((D_MODEL, 3 * WIDTH), fix2, pipeline_mode=pl.Buffered(1)),
            pl.BlockSpec((1, WIDTH), fix2),
            pl.BlockSpec((1, WIDTH), fix2),
            pl.BlockSpec((SGU_GROUPS, SGU_CHUNK, SGU_CHUNK), lambda i: (0, 0, 0)),
            pl.BlockSpec((SGU_CHUNK, SGU_GROUPS), fix2),
        ] + side_in_specs,
        out_specs=[pl.BlockSpec((tm, WIDTH), row)] + side_out_specs,
        scratch_shapes=[pltpu.VMEM((tm, D_MODEL), BF16),
                        pltpu.VMEM((tm, WIDTH), F32),
                        pltpu.VMEM((SGU_GROUPS, SGU_CHUNK, SGU_CHUNK), BF16)],
        compiler_params=pltpu.CompilerParams(
            dimension_semantics=("arbitrary",), vmem_limit_bytes=VMEM_LIMIT),
        name="sgu",
    )(x, w, ng, nb, ws, bs_t, *[arr for arr, _ in side_sources])


def _sgu_v_kernel(x_ref, wv_ref, vg_ref, xb_ref, mu_ref, rstd_ref, wvb_ref):
    @pl.when(pl.program_id(0) == 0)
    def _():
        wvb_ref[...] = wv_ref[...].astype(BF16)

    for r in range(x_ref.shape[0] // TAIL_ROWS):
        rows = slice(r * TAIL_ROWS, (r + 1) * TAIL_ROWS)
        xb = x_ref[rows, :].astype(BF16)
        xb_ref[rows, :] = xb
        pilot = d_sum = sq_sum = None
        for c in range(WIDTH // MXU_COLS):
            cols = slice(c * MXU_COLS, (c + 1) * MXU_COLS)
            vc = jax.nn.gelu(_dot(xb, wvb_ref[:, cols]))
            vg_ref[rows, cols] = vc.astype(BF16)
            if c == 0:
                pilot = jnp.mean(vc, axis=-1, keepdims=True)
            dc = vc - pilot
            d_part = jnp.sum(dc, axis=-1, keepdims=True)
            sq_part = jnp.sum(dc * dc, axis=-1, keepdims=True)
            d_sum = d_part if c == 0 else d_sum + d_part
            sq_sum = sq_part if c == 0 else sq_sum + sq_part
        d_mean = d_sum * (1.0 / WIDTH)
        stat_shape = (TAIL_ROWS, LANES)
        mu_ref[rows, :] = jnp.broadcast_to(pilot + d_mean, stat_shape)
        rstd_ref[rows, :] = jnp.broadcast_to(
            lax.rsqrt(sq_sum * (1.0 / WIDTH) - d_mean * d_mean + LN_EPS), stat_shape)


def _sgu_v(x, w, tm):
    n = x.shape[0]
    row = lambda i: (i, 0)
    stat = jax.ShapeDtypeStruct((n, LANES), F32)
    return pl.pallas_call(
        _sgu_v_kernel,
        out_shape=[jax.ShapeDtypeStruct((n, WIDTH), BF16), jax.ShapeDtypeStruct((n, D_MODEL), BF16),
                   stat, stat],
        grid=(n // tm,),
        in_specs=[
            pl.BlockSpec((tm, D_MODEL), row),
            pl.BlockSpec((D_MODEL, WIDTH), lambda i: (0, 1), pipeline_mode=pl.Buffered(1)),
        ],
        out_specs=[pl.BlockSpec((tm, WIDTH), row), pl.BlockSpec((tm, D_MODEL), row),
                   pl.BlockSpec((tm, LANES), row), pl.BlockSpec((tm, LANES), row)],
        scratch_shapes=[pltpu.VMEM((D_MODEL, WIDTH), BF16)],
        compiler_params=pltpu.CompilerParams(
            dimension_semantics=("arbitrary",), vmem_limit_bytes=VMEM_LIMIT),
        name="sgu_v",
    )(x, w)


def _sgu_uz_kernel(n_side, xb_ref, wu_ref, wz_ref, vg_ref, mu_ref, rstd_ref, ng_ref, nb_ref,
                   ws_ref, bs_ref, *refs):
    side_in, (y_ref, *side_out), (wb_ref, wc_ref) = (
        refs[:n_side], refs[n_side:2 * n_side + 1], refs[2 * n_side + 1:])

    @pl.when(pl.program_id(1) == 0)
    def _():
        wb_ref[0] = wu_ref[...].astype(BF16)
        wb_ref[1] = wz_ref[...].astype(BF16)
        row = lax.broadcasted_iota(jnp.int32, (SGU_CHUNK, SGU_CHUNK), 0)
        col = lax.broadcasted_iota(jnp.int32, (SGU_CHUNK, SGU_CHUNK), 1)
        wc_ref[...] = jnp.where((col <= row)[None], ws_ref[...], 0.0).astype(BF16)

    tm = xb_ref.shape[0]
    gw = WIDTH // SGU_GROUPS
    n_pieces = tm // EPILOGUE_ROWS

    def uz_piece(r):
        xb = xb_ref[r * EPILOGUE_ROWS:(r + 1) * EPILOGUE_ROWS, :]
        return jax.nn.gelu(_dot(xb, wb_ref[0])) * _silu(_dot(xb, wb_ref[1]))

    uz_next = uz_piece(0)
    for r in range(n_pieces):
        rows = slice(r * EPILOGUE_ROWS, (r + 1) * EPILOGUE_ROWS)
        uz = uz_next
        if r + 1 < n_pieces:
            uz_next = uz_piece(r + 1)
        vn = ((vg_ref[rows, :].astype(F32) - mu_ref[rows, 0:1]) * rstd_ref[rows, 0:1] * ng_ref[...]
              + nb_ref[...]).astype(BF16)
        for k in range(EPILOGUE_ROWS // SGU_CHUNK):
            krows = slice(k * SGU_CHUNK, (k + 1) * SGU_CHUNK)
            for gi in range(MXU_COLS // gw):
                gcols = slice(gi * gw, (gi + 1) * gw)
                s = _dot(wc_ref[gi], vn[krows, gcols]) + bs_ref[:, gi:gi + 1]
                y_ref[r * EPILOGUE_ROWS + k * SGU_CHUNK:r * EPILOGUE_ROWS + (k + 1) * SGU_CHUNK,
                      gcols] = (uz[krows, gcols] * s).astype(BF16)
        if r == 0:
            _side_cast(side_in, side_out)


def _sgu_uz(xb, w, vg, mu, rstd, ng, nb, ws, bs_tiles, tm, side_sources):
    n = xb.shape[0]
    tn = MXU_COLS
    nj, ni = WIDTH // tn, n // tm
    groups_per_tile = tn // (WIDTH // SGU_GROUPS)
    side_in_specs, side_out_specs, side_out_shapes = _side_cast_specs(
        side_sources, nj * ni, lambda j, i: j * ni + i)
    return pl.pallas_call(
        partial(_sgu_uz_kernel, len(side_sources)),
        out_shape=[jax.ShapeDtypeStruct((n, WIDTH), BF16)] + side_out_shapes,
        grid=(nj, ni),
        in_specs=[
            pl.BlockSpec((tm, D_MODEL), lambda j, i: (i, 0)),
            pl.BlockSpec((D_MODEL, tn), lambda j, i: (0, j)),
            pl.BlockSpec((D_MODEL, tn), lambda j, i: (0, j + 2 * nj)),
            pl.BlockSpec((tm, tn), lambda j, i: (i, j)),
            pl.BlockSpec((tm, LANES), lambda j, i: (i, 0)),
            pl.BlockSpec((tm, LANES), lambda j, i: (i, 0)),
            pl.BlockSpec((1, tn), lambda j, i: (0, j)),
            pl.BlockSpec((1, tn), lambda j, i: (0, j)),
            pl.BlockSpec((groups_per_tile, SGU_CHUNK, SGU_CHUNK), lambda j, i: (j, 0, 0)),
            pl.BlockSpec((None, SGU_CHUNK, groups_per_tile), lambda j, i: (j, 0, 0)),
        ] + side_in_specs,
        out_specs=[pl.BlockSpec((tm, tn), lambda j, i: (i, j))] + side_out_specs,
        scratch_shapes=[pltpu.VMEM((2, D_MODEL, tn), BF16),
                        pltpu.VMEM((groups_per_tile, SGU_CHUNK, SGU_CHUNK), BF16)],
        compiler_params=pltpu.CompilerParams(
            dimension_semantics=("arbitrary", "arbitrary"),
            vmem_limit_bytes=VMEM_LIMIT),
        name="sgu_uz",
    )(xb, w, w, vg, mu, rstd, ng, nb, ws, bs_tiles, *[arr for arr, _ in side_sources])


def _tail_kernel(y_ref, x_ref, p_ref, wo_ref, g_ref, b_ref, wg_ref, wp_ref, o_ref,
                 maybe_ob_ref=None):
    tm, d = o_ref.shape
    n_pieces = tm // TAIL_ROWS

    def pre_norm(r):
        rows = slice(r * TAIL_ROWS, (r + 1) * TAIL_ROWS)
        for c in range(d // MXU_COLS):
            cols = slice(c * MXU_COLS, (c + 1) * MXU_COLS)
            o_ref[rows, cols] = ALPHA * x_ref[rows, cols] + _dot(y_ref[rows, :], wo_ref[:, cols])

    pre_norm(0)
    for r in range(n_pieces):
        rows = slice(r * TAIL_ROWS, (r + 1) * TAIL_ROWS)
        if r + 1 < n_pieces:
            pre_norm(r + 1)
        xn = _layer_norm(o_ref[rows, :], g_ref[...], b_ref[...])
        xnb = xn.astype(BF16)
        pb = p_ref[rows, :].astype(BF16)
        for c in range(d // MXU_COLS):
            cols = slice(c * MXU_COLS, (c + 1) * MXU_COLS)
            gate = jax.nn.sigmoid(_dot(xnb, wg_ref[:, cols]))
            proj = _dot(pb, wp_ref[:, cols])
            out = xn[:, cols] + gate * proj
            o_ref[rows, cols] = out
            if maybe_ob_ref is not None:
                maybe_ob_ref[rows, cols] = out.astype(BF16)


def _tail(layer, y, x, p, wo, g, b, wg, wp, tm, with_bf16_copy):
    n = x.shape[0]
    row = lambda i: (i, 0)
    fix = lambda i: (layer, 0, 0)
    fix2 = lambda i: (0, 0)
    once = pl.Buffered(1)
    out_shape = [jax.ShapeDtypeStruct((n, D_MODEL), F32)]
    out_specs = [pl.BlockSpec((tm, D_MODEL), row)]
    if with_bf16_copy:
        out_shape.append(jax.ShapeDtypeStruct((n, D_MODEL), BF16))
        out_specs.append(pl.BlockSpec((tm, D_MODEL), row))
    return pl.pallas_call(
        _tail_kernel,
        out_shape=out_shape,
        grid=(n // tm,),
        in_specs=[
            pl.BlockSpec((tm, WIDTH), row),
            pl.BlockSpec((tm, D_MODEL), row),
            pl.BlockSpec((None, tm, PLE_DIM), lambda i: (layer, i, 0)),
            pl.BlockSpec((WIDTH, D_MODEL), fix2, pipeline_mode=once),
            pl.BlockSpec((None, 1, D_MODEL), fix),
            pl.BlockSpec((None, 1, D_MODEL), fix),
            pl.BlockSpec((D_MODEL, D_MODEL), fix2, pipeline_mode=once),
            pl.BlockSpec((None, PLE_DIM, D_MODEL), fix, pipeline_mode=once),
        ],
        out_specs=out_specs,
        compiler_params=pltpu.CompilerParams(
            dimension_semantics=("arbitrary",), vmem_limit_bytes=VMEM_LIMIT),
        name="layer_tail",
    )(y, x, p, wo, g, b, wg, wp)


def _moba_in_kernel(n_side, x_ref, wq_ref, wk_ref, wv_ref, wz_ref, *refs):
    side_in, (q_ref, k_ref, v_ref, zs_ref, *side_out), (wb_ref,) = (
        refs[:n_side], refs[n_side:2 * n_side + 4], refs[2 * n_side + 4:])

    @pl.when(pl.program_id(1) == 0)
    def _():
        for k, w_ref in enumerate((wq_ref, wk_ref, wv_ref, wz_ref)):
            wb_ref[k] = w_ref[...].astype(BF16)

    tm = x_ref.shape[0]
    for r in range(tm // EPILOGUE_ROWS):
        rows = slice(r * EPILOGUE_ROWS, (r + 1) * EPILOGUE_ROWS)
        xb = x_ref[rows, :]
        zs_ref[rows, :] = _silu(_dot(xb, wb_ref[3])).astype(BF16)
        q_ref[rows, :] = (_dot(xb, wb_ref[0]) * (HEAD_DIM ** -0.5 * LOG2_E)).astype(BF16)
        k_ref[rows, :] = _dot(xb, wb_ref[1]).astype(BF16)
        v_ref[rows, :] = _dot(xb, wb_ref[2]).astype(BF16)
        if r == 0:
            _side_cast(side_in, side_out)


def _moba_in(xb, w, tm, side_sources):
    n = xb.shape[0]
    tn = MXU_COLS
    nj, ni = WIDTH // tn, n // tm
    out = jax.ShapeDtypeStruct((n, WIDTH), BF16)
    side_in_specs, side_out_specs, side_out_shapes = _side_cast_specs(
        side_sources, nj * ni, lambda j, i: j * ni + i)

    def wspec(k):
        return pl.BlockSpec((D_MODEL, tn), lambda j, i: (0, j + k * nj))

    ospec = pl.BlockSpec((tm, tn), lambda j, i: (i, j))
    return pl.pallas_call(
        partial(_moba_in_kernel, len(side_sources)),
        out_shape=[out, out, out, out] + side_out_shapes,
        grid=(nj, ni),
        in_specs=[pl.BlockSpec((tm, D_MODEL), lambda j, i: (i, 0)),
                  wspec(0), wspec(1), wspec(2), wspec(3)] + side_in_specs,
        out_specs=[ospec, ospec, ospec, ospec] + side_out_specs,
        scratch_shapes=[pltpu.VMEM((4, D_MODEL, tn), BF16)],
        compiler_params=pltpu.CompilerParams(
            dimension_semantics=("arbitrary", "arbitrary"),
            vmem_limit_bytes=VMEM_LIMIT),
        name="moba_in",
    )(xb, w, w, w, w, *[arr for arr, _ in side_sources])


def _moba_attn_kernel(q_ref, k_ref, v_ref, zs_ref, y_ref, vt_ref):
    seq = k_ref.shape[0]
    n_heads = k_ref.shape[1] // HEAD_DIM
    nb = seq // MOBA_BLOCK
    blk = MOBA_BLOCK
    neg_inf = jnp.float32(-jnp.inf)

    key_i = lax.broadcasted_iota(jnp.int32, (blk, blk), 0)
    qry_i = lax.broadcasted_iota(jnp.int32, (blk, blk), 1)
    causal_t = key_i <= qry_i
    blk_i = lax.broadcasted_iota(jnp.int32, (nb, blk), 0)

    def hcols(h):
        return slice(h * HEAD_DIM, (h + 1) * HEAD_DIM)

    def qblock(own, h):
        return q_ref[own * blk:(own + 1) * blk, hcols(h)]

    def scores(own, h):
        q = qblock(own, h)
        return [_dot_nt(k_ref[n * blk:(n + 1) * blk, hcols(h)], q) for n in range(own + 1)]

    def stats(own, h, s_blocks):
        bias = None
        if own > MOBA_TOP_K:
            gate = _dot_nt(k_means[h], qblock(own, h).astype(F32))
            gate = jnp.where(blk_i < own, gate, neg_inf)
            cnt = jnp.zeros((nb, blk), jnp.int32)
            for m in range(own):
                gm = gate[m:m + 1, :]
                beats = (gm > gate) | ((gm == gate) & (m < blk_i))
                cnt = cnt + jnp.where(beats, 1, 0)
            bias = jnp.where((cnt < MOBA_TOP_K) & (blk_i < own), 0.0, neg_inf)

        s_blocks = s_blocks[:own] + [jnp.where(causal_t, s_blocks[own], neg_inf)]
        m_q = jnp.max(s_blocks[own], axis=0, keepdims=True)
        for n in range(own):
            bm = jnp.max(s_blocks[n], axis=0, keepdims=True)
            if bias is not None:
                bm = bm + bias[n:n + 1, :]
            m_q = jnp.maximum(m_q, bm)
        return s_blocks, m_q, bias

    def exp_pv(own, h, s_blocks, m_q, bias):
        acc = jnp.zeros((HEAD_DIM + ONES_ROWS, blk), F32)
        for n in range(own + 1):
            shift = m_q
            if bias is not None and n < own:
                shift = m_q - bias[n:n + 1, :]
            pn = jnp.exp2(s_blocks[n] - shift)
            acc = acc + _dot(vt_ref[h, :, n * blk:(n + 1) * blk], pn.astype(BF16))
        o = (acc[0:HEAD_DIM, :] / acc[HEAD_DIM:HEAD_DIM + 1, :]).T
        rows = slice(own * blk, (own + 1) * blk)
        y_ref[rows, hcols(h)] = (o * zs_ref[rows, hcols(h)].astype(F32)).astype(BF16)

    units = [(own, h) for own in reversed(range(nb)) for h in range(n_heads)]
    pending = [scores(*units[0]), scores(*units[1])]

    k_means = []
    for h in range(n_heads):
        for n in range(nb):
            vt_ref[h, 0:HEAD_DIM, n * blk:(n + 1) * blk] = (
                v_ref[n * blk:(n + 1) * blk, hcols(h)].astype(F32).T.astype(BF16))
        vt_ref[h, HEAD_DIM:HEAD_DIM + ONES_ROWS, :] = jnp.ones((ONES_ROWS, seq), BF16)
        kf = k_ref[:, hcols(h)].astype(F32)
        k_means.append(jnp.mean(kf.reshape(nb, blk, HEAD_DIM), axis=1))

    st = stats(*units[0], pending.pop(0))
    for i, unit in enumerate(units):
        if i + 2 < len(units):
            pending.append(scores(*units[i + 2]))
        st_next = stats(*units[i + 1], pending.pop(0)) if i + 1 < len(units) else None
        exp_pv(*unit, *st)
        st = st_next


def _moba_attn(q, k, v, zs, batch, seq, heads_per_step):
    spec = pl.BlockSpec((seq, heads_per_step * HEAD_DIM), lambda b, h: (b, h))
    return pl.pallas_call(
        _moba_attn_kernel,
        out_shape=jax.ShapeDtypeStruct((batch * seq, WIDTH), BF16),
        grid=(batch, HEADS // heads_per_step),
        in_specs=[spec, spec, spec, spec],
        out_specs=spec,
        scratch_shapes=[pltpu.VMEM((heads_per_step, HEAD_DIM + ONES_ROWS, seq), BF16)],
        compiler_params=pltpu.CompilerParams(
            dimension_semantics=("arbitrary", "arbitrary"),
            vmem_limit_bytes=VMEM_LIMIT),
        name="moba_attn",
    )(q, k, v, zs)


def kernel(x, p, w_in_a, sgu_norm_g, sgu_norm_b, w_s, b_s, w_in_b, w_out, ln_g, ln_b,
           w_ple_gate, w_ple_proj):
    batch, seq, d = x.shape
    n = batch * seq
    x2 = x.reshape(n, d)
    p2 = p.reshape(DEPTH, n, PLE_DIM)

    ln_g3, ln_b3 = ln_g[:, None, :], ln_b[:, None, :]

    vg, x0b, mu, rstd = _sgu_v(x2, w_in_a[0], tm=512)
    groups_per_tile = MXU_COLS // (WIDTH // SGU_GROUPS)
    bs_tiles = b_s[0].reshape(SGU_GROUPS // groups_per_tile, groups_per_tile, SGU_CHUNK)
    y, wo0, wg0, wp_flat = _sgu_uz(
        x0b, w_in_a[0], vg, mu, rstd, sgu_norm_g[0][None], sgu_norm_b[0][None], w_s[0],
        bs_tiles.transpose(0, 2, 1), tm=2048,
        side_sources=[(w_out, 0), (w_ple_gate, 0),
                      (w_ple_proj.reshape(DEPTH * PLE_DIM, d), None)])
    wp = wp_flat.reshape(DEPTH, PLE_DIM, d)
    x2, x2b = _tail(0, y, x2, p2, wo0, ln_g3, ln_b3, wg0, wp, tm=512, with_bf16_copy=True)

    q, k, v, zs, wo1, wg1 = _moba_in(
        x2b, w_in_b[0], tm=2048, side_sources=[(w_out, 1), (w_ple_gate, 1)])
    y = _moba_attn(q, k, v, zs, batch, seq, heads_per_step=4)
    (x2,) = _tail(1, y, x2, p2, wo1, ln_g3, ln_b3, wg1, wp, tm=512, with_bf16_copy=False)

    return x2.reshape(batch, seq, d)
```

```python
from functools import partial

import jax
import jax.numpy as jnp
from jax import lax
from jax.experimental import pallas as pl
from jax.experimental.pallas import tpu as pltpu

D_MODEL = 2048
WIDTH = 2048
SGU_CHUNK = 128
SGU_GROUPS = 16
HEADS = 16
HEAD_DIM = 128
MOBA_BLOCK = 256
MOBA_TOP_K = 3
PLE_DIM = 256
LN_EPS = 1e-5
DEPTH = 2
ALPHA = (2 * DEPTH) ** 0.25
LOG2_E = 1.4426950408889634
MXU_COLS = 256
LANES = 128
EPILOGUE_ROWS = 512
TAIL_ROWS = 256
SGU_ROWS = 512
BF16_ROWS = 16
ONES_ROWS = BF16_ROWS

BF16 = jnp.bfloat16
F32 = jnp.float32

VMEM_LIMIT = 56 * 1024 * 1024


def _dot(a, b):
    return jnp.dot(a, b, preferred_element_type=F32)


def _dot_nt(a, b):
    return lax.dot_general(a, b, (((1,), (1,)), ((), ())), preferred_element_type=F32)


def _silu(z):
    return z * jax.nn.sigmoid(z)


def _layer_norm(x, g, b):
    mu = jnp.mean(x, axis=-1, keepdims=True)
    xc = x - mu
    var = jnp.mean(xc * xc, axis=-1, keepdims=True)
    return xc * lax.rsqrt(var + LN_EPS) * g + b


def _side_cast_specs(sources, n_steps, step_index):
    in_specs, out_specs, out_shapes = [], [], []
    for arr, layer in sources:
        rows, cols = arr.shape[-2:]
        slab = rows // n_steps
        assert slab * n_steps == rows and slab % BF16_ROWS == 0, (rows, n_steps)
        if layer is None:
            in_specs.append(pl.BlockSpec((slab, cols), lambda *g: (step_index(*g), 0)))
        else:
            in_specs.append(pl.BlockSpec(
                (None, slab, cols), lambda *g, layer=layer: (layer, step_index(*g), 0)))
        out_specs.append(pl.BlockSpec((slab, cols), lambda *g: (step_index(*g), 0)))
        out_shapes.append(jax.ShapeDtypeStruct((rows, cols), BF16))
    return in_specs, out_specs, out_shapes


def _side_cast(side_in_refs, side_out_refs):
    for src_ref, dst_ref in zip(side_in_refs, side_out_refs):
        dst_ref[...] = src_ref[...].astype(BF16)


def _sgu_kernel(n_side, x_ref, w_ref, ng_ref, nb_ref, ws_ref, bs_ref, *refs):
    side_in, (y_ref, *side_out), (xb_ref, vg_ref, wc_ref) = (
        refs[:n_side], refs[n_side:2 * n_side + 1], refs[2 * n_side + 1:])

    @pl.when(pl.program_id(0) == 0)
    def _():
        row = lax.broadcasted_iota(jnp.int32, (SGU_CHUNK, SGU_CHUNK), 0)
        col = lax.broadcasted_iota(jnp.int32, (SGU_CHUNK, SGU_CHUNK), 1)
        keep = (col <= row)[None]
        wc_ref[...] = jnp.where(keep, ws_ref[...], 0.0).astype(BF16)

    tm = x_ref.shape[0]
    n_pieces = WIDTH // MXU_COLS
    gw = WIDTH // SGU_GROUPS
    xb_ref[...] = x_ref[...].astype(BF16)

    def in_proj(path, c, epilogue):
        w_cols = slice(path * WIDTH + c * MXU_COLS, path * WIDTH + (c + 1) * MXU_COLS)
        parts = [epilogue(_dot(xb_ref[r * SGU_ROWS:(r + 1) * SGU_ROWS, :], w_ref[:, w_cols]))
                 for r in range(tm // SGU_ROWS)]
        return jnp.concatenate(parts, axis=0)

    def lane_fold(a):
        return a[:, 0:LANES] + a[:, LANES:2 * LANES]

    row_acc = jnp.zeros((tm, LANES), F32)
    for c in range(n_pieces):
        cols = slice(c * MXU_COLS, (c + 1) * MXU_COLS)
        vc = in_proj(1, c, jax.nn.gelu)
        vg_ref[:, cols] = vc
        row_acc = row_acc + lane_fold(vc)
    _side_cast(side_in, side_out)
    mu = jnp.sum(row_acc, axis=-1, keepdims=True) * (1.0 / WIDTH)
    sq_acc = jnp.zeros((tm, LANES), F32)
    for c in range(n_pieces):
        cols = slice(c * MXU_COLS, (c + 1) * MXU_COLS)
        dc = vg_ref[:, cols] - mu
        sq_acc = sq_acc + lane_fold(dc * dc)
    rstd = lax.rsqrt(jnp.sum(sq_acc, axis=-1, keepdims=True) * (1.0 / WIDTH) + LN_EPS)

    def uz_piece(c):
        return in_proj(0, c, jax.nn.gelu) * in_proj(2, c, _silu)

    uz_next = uz_piece(0)
    for c in range(n_pieces):
        cols = slice(c * MXU_COLS, (c + 1) * MXU_COLS)
        uz = uz_next
        if c + 1 < n_pieces:
            uz_next = uz_piece(c + 1)
        vn = ((vg_ref[:, cols] - mu) * rstd * ng_ref[:, cols] + nb_ref[:, cols]).astype(BF16)
        for k in range(tm // SGU_CHUNK):
            rows = slice(k * SGU_CHUNK, (k + 1) * SGU_CHUNK)
            for gi in range(MXU_COLS // gw):
                g = c * (MXU_COLS // gw) + gi
                gcols = slice(gi * gw, (gi + 1) * gw)
                s = _dot(wc_ref[g], vn[rows, gcols]) + bs_ref[:, g:g + 1]
                y_ref[rows, c * MXU_COLS + gi * gw:c * MXU_COLS + (gi + 1) * gw] = (
                    uz[rows, gcols] * s).astype(BF16)


def _sgu(x, w, ng, nb, ws, bs_t, tm, side_sources):
    n = x.shape[0]
    row = lambda i: (i, 0)
    fix2 = lambda i: (0, 0)
    side_in_specs, side_out_specs, side_out_shapes = _side_cast_specs(
        side_sources, n // tm, lambda i: i)
    return pl.pallas_call(
        partial(_sgu_kernel, len(side_sources)),
        out_shape=[jax.ShapeDtypeStruct((n, WIDTH), BF16)] + side_out_shapes,
        grid=(n // tm,),
        in_specs=[
            pl.BlockSpec((tm, D_MODEL), row),
            pl.BlockSpec((D_MODEL, 3 * WIDTH), fix2, pipeline_mode=pl.Buffered(1)),
            pl.BlockSpec((1, WIDTH), fix2),
            pl.BlockSpec((1, WIDTH), fix2),
            pl.BlockSpec((SGU_GROUPS, SGU_CHUNK, SGU_CHUNK), lambda i: (0, 0, 0)),
            pl.BlockSpec((SGU_CHUNK, SGU_GROUPS), fix2),
        ] + side_in_specs,
        out_specs=[pl.BlockSpec((tm, WIDTH), row)] + side_out_specs,
        scratch_shapes=[pltpu.VMEM((tm, D_MODEL), BF16),
                        pltpu.VMEM((tm, WIDTH), F32),
                        pltpu.VMEM((SGU_GROUPS, SGU_CHUNK, SGU_CHUNK), BF16)],
        compiler_params=pltpu.CompilerParams(
            dimension_semantics=("arbitrary",), vmem_limit_bytes=VMEM_LIMIT),
        name="sgu",
    )(x, w, ng, nb, ws, bs_t, *[arr for arr, _ in side_sources])


def _tail_kernel(y_ref, x_ref, p_ref, wo_ref, g_ref, b_ref, wg_ref, wp_ref, o_ref,
                 maybe_ob_ref=None):
    tm, d = o_ref.shape
    n_pieces = tm // TAIL_ROWS

    def pre_norm(r):
        rows = slice(r * TAIL_ROWS, (r + 1) * TAIL_ROWS)
        for c in range(d // MXU_COLS):
            cols = slice(c * MXU_COLS, (c + 1) * MXU_COLS)
            o_ref[rows, cols] = ALPHA * x_ref[rows, cols] + _dot(y_ref[rows, :], wo_ref[:, cols])

    pre_norm(0)
    for r in range(n_pieces):
        rows = slice(r * TAIL_ROWS, (r + 1) * TAIL_ROWS)
        if r + 1 < n_pieces:
            pre_norm(r + 1)
        xn = _layer_norm(o_ref[rows, :], g_ref[...], b_ref[...])
        xnb = xn.astype(BF16)
        pb = p_ref[rows, :].astype(BF16)
        for c in range(d // MXU_COLS):
            cols = slice(c * MXU_COLS, (c + 1) * MXU_COLS)
            gate = jax.nn.sigmoid(_dot(xnb, wg_ref[:, cols]))
            proj = _dot(pb, wp_ref[:, cols])
            out = xn[:, cols] + gate * proj
            o_ref[rows, cols] = out
            if maybe_ob_ref is not None:
                maybe_ob_ref[rows, cols] = out.astype(BF16)


def _tail(layer, y, x, p, wo, g, b, wg, wp, tm, with_bf16_copy):
    n = x.shape[0]
    row = lambda i: (i, 0)
    fix = lambda i: (layer, 0, 0)
    fix2 = lambda i: (0, 0)
    once = pl.Buffered(1)
    out_shape = [jax.ShapeDtypeStruct((n, D_MODEL), F32)]
    out_specs = [pl.BlockSpec((tm, D_MODEL), row)]
    if with_bf16_copy:
        out_shape.append(jax.ShapeDtypeStruct((n, D_MODEL), BF16))
        out_specs.append(pl.BlockSpec((tm, D_MODEL), row))
    return pl.pallas_call(
        _tail_kernel,
        out_shape=out_shape,
        grid=(n // tm,),
        in_specs=[
            pl.BlockSpec((tm, WIDTH), row),
            pl.BlockSpec((tm, D_MODEL), row),
            pl.BlockSpec((None, tm, PLE_DIM), lambda i: (layer, i, 0)),
            pl.BlockSpec((WIDTH, D_MODEL), fix2, pipeline_mode=once),
            pl.BlockSpec((None, 1, D_MODEL), fix),
            pl.BlockSpec((None, 1, D_MODEL), fix),
            pl.BlockSpec((D_MODEL, D_MODEL), fix2, pipeline_mode=once),
            pl.BlockSpec((None, PLE_DIM, D_MODEL), fix, pipeline_mode=once),
        ],
        out_specs=out_specs,
        compiler_params=pltpu.CompilerParams(
            dimension_semantics=("arbitrary",), vmem_limit_bytes=VMEM_LIMIT),
        name="layer_tail",
    )(y, x, p, wo, g, b, wg, wp)


def _moba_in_kernel(n_side, x_ref, wq_ref, wk_ref, wv_ref, wz_ref, *refs):
    side_in, (q_ref, k_ref, v_ref, zs_ref, *side_out), (wb_ref,) = (
        refs[:n_side], refs[n_side:2 * n_side + 4], refs[2 * n_side + 4:])

    @pl.when(pl.program_id(1) == 0)
    def _():
        for k, w_ref in enumerate((wq_ref, wk_ref, wv_ref, wz_ref)):
            wb_ref[k] = w_ref[...].astype(BF16)

    tm = x_ref.shape[0]
    for r in range(tm // EPILOGUE_ROWS):
        rows = slice(r * EPILOGUE_ROWS, (r + 1) * EPILOGUE_ROWS)
        xb = x_ref[rows, :]
        zs_ref[rows, :] = _silu(_dot(xb, wb_ref[3])).astype(BF16)
        q_ref[rows, :] = (_dot(xb, wb_ref[0]) * (HEAD_DIM ** -0.5 * LOG2_E)).astype(BF16)
        k_ref[rows, :] = _dot(xb, wb_ref[1]).astype(BF16)
        v_ref[rows, :] = _dot(xb, wb_ref[2]).astype(BF16)
        if r == 0:
            _side_cast(side_in, side_out)


def _moba_in(xb, w, tm, side_sources):
    n = xb.shape[0]
    tn = MXU_COLS
    nj, ni = WIDTH // tn, n // tm
    out = jax.ShapeDtypeStruct((n, WIDTH), BF16)
    side_in_specs, side_out_specs, side_out_shapes = _side_cast_specs(
        side_sources, nj * ni, lambda j, i: j * ni + i)

    def wspec(k):
        return pl.BlockSpec((D_MODEL, tn), lambda j, i: (0, j + k * nj))

    ospec = pl.BlockSpec((tm, tn), lambda j, i: (i, j))
    return pl.pallas_call(
        partial(_moba_in_kernel, len(side_sources)),
        out_shape=[out, out, out, out] + side_out_shapes,
        grid=(nj, ni),
        in_specs=[pl.BlockSpec((tm, D_MODEL), lambda j, i: (i, 0)),
                  wspec(0), wspec(1), wspec(2), wspec(3)] + side_in_specs,
        out_specs=[ospec, ospec, ospec, ospec] + side_out_specs,
        scratch_shapes=[pltpu.VMEM((4, D_MODEL, tn), BF16)],
        compiler_params=pltpu.CompilerParams(
            dimension_semantics=("arbitrary", "arbitrary"),
            vmem_limit_bytes=VMEM_LIMIT),
        name="moba_in",
    )(xb, w, w, w, w, *[arr for arr, _ in side_sources])


def _moba_attn_kernel(q_ref, k_ref, v_ref, zs_ref, y_ref, vt_ref):
    seq = k_ref.shape[0]
    n_heads = k_ref.shape[1] // HEAD_DIM
    nb = seq // MOBA_BLOCK
    blk = MOBA_BLOCK
    neg_inf = jnp.float32(-jnp.inf)

    key_i = lax.broadcasted_iota(jnp.int32, (blk, blk), 0)
    qry_i = lax.broadcasted_iota(jnp.int32, (blk, blk), 1)
    causal_t = key_i <= qry_i
    blk_i = lax.broadcasted_iota(jnp.int32, (nb, blk), 0)

    def hcols(h):
        return slice(h * HEAD_DIM, (h + 1) * HEAD_DIM)

    def qblock(own, h):
        return q_ref[own * blk:(own + 1) * blk, hcols(h)]

    def scores(own, h):
        q = qblock(own, h)
        return [_dot_nt(k_ref[n * blk:(n + 1) * blk, hcols(h)], q) for n in range(own + 1)]

    def stats(own, h, s_blocks):
        bias = None
        if own > MOBA_TOP_K:
            gate = _dot_nt(k_means[h], qblock(own, h).astype(F32))
            gate = jnp.where(blk_i < own, gate, neg_inf)
            cnt = jnp.zeros((nb, blk), jnp.int32)
            for m in range(own):
                gm = gate[m:m + 1, :]
                beats = (gm > gate) | ((gm == gate) & (m < blk_i))
                cnt = cnt + jnp.where(beats, 1, 0)
            bias = jnp.where((cnt < MOBA_TOP_K) & (blk_i < own), 0.0, neg_inf)

        s_blocks = s_blocks[:own] + [jnp.where(causal_t, s_blocks[own], neg_inf)]
        m_q = jnp.max(s_blocks[own], axis=0, keepdims=True)
        for n in range(own):
            bm = jnp.max(s_blocks[n], axis=0, keepdims=True)
            if bias is not None:
                bm = bm + bias[n:n + 1, :]
            m_q = jnp.maximum(m_q, bm)
        return s_blocks, m_q, bias

    def exp_pv(own, h, s_blocks, m_q, bias):
        acc = jnp.zeros((HEAD_DIM + ONES_ROWS, blk), F32)
        for n in range(own + 1):
            shift = m_q
            if bias is not None and n < own:
                shift = m_q - bias[n:n + 1, :]
            pn = jnp.exp2(s_blocks[n] - shift)
            acc = acc + _dot(vt_ref[h, :, n * blk:(n + 1) * blk], pn.astype(BF16))
        o = (acc[0:HEAD_DIM, :] / acc[HEAD_DIM:HEAD_DIM + 1, :]).T
        rows = slice(own * blk, (own + 1) * blk)
        y_ref[rows, hcols(h)] = (o * zs_ref[rows, hcols(h)].astype(F32)).astype(BF16)

    units = [(own, h) for own in reversed(range(nb)) for h in range(n_heads)]
    pending = [scores(*units[0]), scores(*units[1])]

    k_means = []
    for h in range(n_heads):
        for n in range(nb):
            vt_ref[h, 0:HEAD_DIM, n * blk:(n + 1) * blk] = (
                v_ref[n * blk:(n + 1) * blk, hcols(h)].astype(F32).T.astype(BF16))
        vt_ref[h, HEAD_DIM:HEAD_DIM + ONES_ROWS, :] = jnp.ones((ONES_ROWS, seq), BF16)
        kf = k_ref[:, hcols(h)].astype(F32)
        k_means.append(jnp.mean(kf.reshape(nb, blk, HEAD_DIM), axis=1))

    st = stats(*units[0], pending.pop(0))
    for i, unit in enumerate(units):
        if i + 2 < len(units):
            pending.append(scores(*units[i + 2]))
        st_next = stats(*units[i + 1], pending.pop(0)) if i + 1 < len(units) else None
        exp_pv(*unit, *st)
        st = st_next


def _moba_attn(q, k, v, zs, batch, seq, heads_per_step):
    spec = pl.BlockSpec((seq, heads_per_step * HEAD_DIM), lambda b, h: (b, h))
    return pl.pallas_call(
        _moba_attn_kernel,
        out_shape=jax.ShapeDtypeStruct((batch * seq, WIDTH), BF16),
        grid=(batch, HEADS // heads_per_step),
        in_specs=[spec, spec, spec, spec],
        out_specs=spec,
        scratch_shapes=[pltpu.VMEM((heads_per_step, HEAD_DIM + ONES_ROWS, seq), BF16)],
        compiler_params=pltpu.CompilerParams(
            dimension_semantics=("arbitrary", "arbitrary"),
            vmem_limit_bytes=VMEM_LIMIT),
        name="moba_attn",
    )(q, k, v, zs)


def kernel(x, p, w_in_a, sgu_norm_g, sgu_norm_b, w_s, b_s, w_in_b, w_out, ln_g, ln_b,
           w_ple_gate, w_ple_proj):
    batch, seq, d = x.shape
    n = batch * seq
    x2 = x.reshape(n, d)
    p2 = p.reshape(DEPTH, n, PLE_DIM)

    ln_g3, ln_b3 = ln_g[:, None, :], ln_b[:, None, :]

    y, wo0, wg0, wp_flat = _sgu(
        x2, w_in_a[0].astype(BF16), sgu_norm_g[0][None], sgu_norm_b[0][None], w_s[0], b_s[0].T,
        tm=512,
        side_sources=[(w_out, 0), (w_ple_gate, 0),
                      (w_ple_proj.reshape(DEPTH * PLE_DIM, d), None)])
    wp = wp_flat.reshape(DEPTH, PLE_DIM, d)
    x2, x2b = _tail(0, y, x2, p2, wo0, ln_g3, ln_b3, wg0, wp, tm=512, with_bf16_copy=True)

    q, k, v, zs, wo1, wg1 = _moba_in(
        x2b, w_in_b[0], tm=2048, side_sources=[(w_out, 1), (w_ple_gate, 1)])
    y = _moba_attn(q, k, v, zs, batch, seq, heads_per_step=4)
    (x2,) = _tail(1, y, x2, p2, wo1, ln_g3, ln_b3, wg1, wp, tm=512, with_bf16_copy=False)

    return x2.reshape(batch, seq, d)
```

```python
from functools import partial

import jax
import jax.numpy as jnp
from jax import lax
from jax.experimental import pallas as pl
from jax.experimental.pallas import tpu as pltpu

D_MODEL = 2048
WIDTH = 2048
SGU_CHUNK = 128
SGU_GROUPS = 16
HEADS = 16
HEAD_DIM = 128
MOBA_BLOCK = 256
MOBA_TOP_K = 3
PLE_DIM = 256
LN_EPS = 1e-5
DEPTH = 2
ALPHA = (2 * DEPTH) ** 0.25
LOG2_E = 1.4426950408889634
MXU_COLS = 256
LANES = 128
EPILOGUE_ROWS = 512
TAIL_ROWS = 256
SGU_ROWS = 512
BF16_ROWS = 16
ONES_ROWS = BF16_ROWS
VT_ROWS = HEAD_DIM + ONES_ROWS

BF16 = jnp.bfloat16
F32 = jnp.float32

VMEM_LIMIT = 56 * 1024 * 1024
SGU_TM = 512
TAIL_TM = 512
MOBA_IN_TM = 2048
ATTN_HEADS_PER_STEP = 4


def _dot(a, b):
    return jnp.dot(a, b, preferred_element_type=F32)


def _dot_nt(a, b):
    return lax.dot_general(a, b, (((1,), (1,)), ((), ())), preferred_element_type=F32)


def _silu(z):
    return z * jax.nn.sigmoid(z)


def _layer_norm(x, g, b):
    mu = jnp.mean(x, axis=-1, keepdims=True)
    xc = x - mu
    var = jnp.mean(xc * xc, axis=-1, keepdims=True)
    return xc * lax.rsqrt(var + LN_EPS) * g + b


def _side_cast_specs(sources, n_steps, step_index):
    in_specs, out_specs, out_shapes = [], [], []
    for arr, layer in sources:
        rows, cols = arr.shape[-2:]
        slab = rows // n_steps
        assert slab * n_steps == rows and slab % BF16_ROWS == 0, (rows, n_steps)
        if layer is None:
            in_specs.append(pl.BlockSpec((slab, cols), lambda *g: (step_index(*g), 0)))
        else:
            in_specs.append(pl.BlockSpec(
                (None, slab, cols), lambda *g, layer=layer: (layer, step_index(*g), 0)))
        out_specs.append(pl.BlockSpec((slab, cols), lambda *g: (step_index(*g), 0)))
        out_shapes.append(jax.ShapeDtypeStruct((rows, cols), BF16))
    return in_specs, out_specs, out_shapes


def _side_cast(side_in_refs, side_out_refs):
    for src_ref, dst_ref in zip(side_in_refs, side_out_refs):
        dst_ref[...] = src_ref[...].astype(BF16)


def _sgu_kernel(n_side, x_ref, w_ref, ng_ref, nb_ref, ws_ref, bs_ref, *refs):
    side_in, (y_ref, *side_out), (xb_ref, vg_ref, wc_ref) = (
        refs[:n_side], refs[n_side:2 * n_side + 1], refs[2 * n_side + 1:])

    @pl.when(pl.program_id(0) == 0)
    def _():
        row = lax.broadcasted_iota(jnp.int32, (SGU_CHUNK, SGU_CHUNK), 0)
        col = lax.broadcasted_iota(jnp.int32, (SGU_CHUNK, SGU_CHUNK), 1)
        keep = (col <= row)[None]
        wc_ref[...] = jnp.where(keep, ws_ref[...], 0.0).astype(BF16)

    tm = x_ref.shape[0]
    n_pieces = WIDTH // MXU_COLS
    gw = WIDTH // SGU_GROUPS
    xb_ref[...] = x_ref[...].astype(BF16)

    def in_proj(path, c, epilogue):
        w_cols = slice(path * WIDTH + c * MXU_COLS, path * WIDTH + (c + 1) * MXU_COLS)
        parts = [epilogue(_dot(xb_ref[r * SGU_ROWS:(r + 1) * SGU_ROWS, :], w_ref[:, w_cols]))
                 for r in range(tm // SGU_ROWS)]
        return jnp.concatenate(parts, axis=0)

    def lane_fold(a):
        return a[:, 0:LANES] + a[:, LANES:2 * LANES]

    row_acc = jnp.zeros((tm, LANES), F32)
    for c in range(n_pieces):
        cols = slice(c * MXU_COLS, (c + 1) * MXU_COLS)
        vc = in_proj(1, c, jax.nn.gelu)
        vg_ref[:, cols] = vc
        row_acc = row_acc + lane_fold(vc)
    _side_cast(side_in, side_out)
    mu = jnp.sum(row_acc, axis=-1, keepdims=True) * (1.0 / WIDTH)
    sq_acc = jnp.zeros((tm, LANES), F32)
    for c in range(n_pieces):
        cols = slice(c * MXU_COLS, (c + 1) * MXU_COLS)
        dc = vg_ref[:, cols] - mu
        sq_acc = sq_acc + lane_fold(dc * dc)
    rstd = lax.rsqrt(jnp.sum(sq_acc, axis=-1, keepdims=True) * (1.0 / WIDTH) + LN_EPS)

    def uz_piece(c):
        return in_proj(0, c, jax.nn.gelu) * in_proj(2, c, _silu)

    uz_next = uz_piece(0)
    for c in range(n_pieces):
        cols = slice(c * MXU_COLS, (c + 1) * MXU_COLS)
        uz = uz_next
        if c + 1 < n_pieces:
            uz_next = uz_piece(c + 1)
        vn = ((vg_ref[:, cols] - mu) * rstd * ng_ref[:, cols] + nb_ref[:, cols]).astype(BF16)
        for k in range(tm // SGU_CHUNK):
            rows = slice(k * SGU_CHUNK, (k + 1) * SGU_CHUNK)
            for gi in range(MXU_COLS // gw):
                g = c * (MXU_COLS // gw) + gi
                gcols = slice(gi * gw, (gi + 1) * gw)
                s = _dot(wc_ref[g], vn[rows, gcols]) + bs_ref[:, g:g + 1]
                y_ref[rows, c * MXU_COLS + gi * gw:c * MXU_COLS + (gi + 1) * gw] = (
                    uz[rows, gcols] * s).astype(BF16)


def _sgu(x, w, ng, nb, ws, bs_t, tm, side_sources):
    n = x.shape[0]
    row = lambda i: (i, 0)
    fix2 = lambda i: (0, 0)
    side_in_specs, side_out_specs, side_out_shapes = _side_cast_specs(
        side_sources, n // tm, lambda i: i)
    return pl.pallas_call(
        partial(_sgu_kernel, len(side_sources)),
        out_shape=[jax.ShapeDtypeStruct((n, WIDTH), BF16)] + side_out_shapes,
        grid=(n // tm,),
        in_specs=[
            pl.BlockSpec((tm, D_MODEL), row),
            pl.BlockSpec((D_MODEL, 3 * WIDTH), fix2, pipeline_mode=pl.Buffered(1)),
            pl.BlockSpec((1, WIDTH), fix2),
            pl.BlockSpec((1, WIDTH), fix2),
            pl.BlockSpec((SGU_GROUPS, SGU_CHUNK, SGU_CHUNK), lambda i: (0, 0, 0)),
            pl.BlockSpec((SGU_CHUNK, SGU_GROUPS), fix2),
        ] + side_in_specs,
        out_specs=[pl.BlockSpec((tm, WIDTH), row)] + side_out_specs,
        scratch_shapes=[pltpu.VMEM((tm, D_MODEL), BF16),
                        pltpu.VMEM((tm, WIDTH), F32),
                        pltpu.VMEM((SGU_GROUPS, SGU_CHUNK, SGU_CHUNK), BF16)],
        compiler_params=pltpu.CompilerParams(
            dimension_semantics=("arbitrary",), vmem_limit_bytes=VMEM_LIMIT),
        name="sgu",
    )(x, w, ng, nb, ws, bs_t, *[arr for arr, _ in side_sources])


def _tail_kernel(y_ref, x_ref, p_ref, wo_ref, g_ref, b_ref, wg_ref, wp_ref, o_ref,
                 maybe_ob_ref=None):
    tm, d = o_ref.shape
    n_pieces = tm // TAIL_ROWS

    def pre_norm(r):
        rows = slice(r * TAIL_ROWS, (r + 1) * TAIL_ROWS)
        for c in range(d // MXU_COLS):
            cols = slice(c * MXU_COLS, (c + 1) * MXU_COLS)
            o_ref[rows, cols] = ALPHA * x_ref[rows, cols] + _dot(y_ref[rows, :], wo_ref[:, cols])

    pre_norm(0)
    for r in range(n_pieces):
        rows = slice(r * TAIL_ROWS, (r + 1) * TAIL_ROWS)
        if r + 1 < n_pieces:
            pre_norm(r + 1)
        xn = _layer_norm(o_ref[rows, :], g_ref[...], b_ref[...])
        xnb = xn.astype(BF16)
        pb = p_ref[rows, :].astype(BF16)
        for c in range(d // MXU_COLS):
            cols = slice(c * MXU_COLS, (c + 1) * MXU_COLS)
            gate = jax.nn.sigmoid(_dot(xnb, wg_ref[:, cols]))
            proj = _dot(pb, wp_ref[:, cols])
            out = xn[:, cols] + gate * proj
            o_ref[rows, cols] = out
            if maybe_ob_ref is not None:
                maybe_ob_ref[rows, cols] = out.astype(BF16)


def _tail(layer, y, x, p, wo, g, b, wg, wp, tm, with_bf16_copy):
    n = x.shape[0]
    row = lambda i: (i, 0)
    fix = lambda i: (layer, 0, 0)
    fix2 = lambda i: (0, 0)
    once = pl.Buffered(1)
    out_shape = [jax.ShapeDtypeStruct((n, D_MODEL), F32)]
    out_specs = [pl.BlockSpec((tm, D_MODEL), row)]
    if with_bf16_copy:
        out_shape.append(jax.ShapeDtypeStruct((n, D_MODEL), BF16))
        out_specs.append(pl.BlockSpec((tm, D_MODEL), row))
    return pl.pallas_call(
        _tail_kernel,
        out_shape=out_shape,
        grid=(n // tm,),
        in_specs=[
            pl.BlockSpec((tm, WIDTH), row),
            pl.BlockSpec((tm, D_MODEL), row),
            pl.BlockSpec((None, tm, PLE_DIM), lambda i: (layer, i, 0)),
            pl.BlockSpec((WIDTH, D_MODEL), fix2, pipeline_mode=once),
            pl.BlockSpec((None, 1, D_MODEL), fix),
            pl.BlockSpec((None, 1, D_MODEL), fix),
            pl.BlockSpec((D_MODEL, D_MODEL), fix2, pipeline_mode=once),
            pl.BlockSpec((None, PLE_DIM, D_MODEL), fix, pipeline_mode=once),
        ],
        out_specs=out_specs,
        compiler_params=pltpu.CompilerParams(
            dimension_semantics=("arbitrary",), vmem_limit_bytes=VMEM_LIMIT),
        name="layer_tail",
    )(y, x, p, wo, g, b, wg, wp)


def _moba_in_kernel(n_side, x_ref, wq_ref, wk_ref, wv_ref, wz_ref, *refs):
    side_in, (q_ref, k_ref, vt_ref, zs_ref, kmean_ref, *side_out), (wb_ref,) = (
        refs[:n_side], refs[n_side:2 * n_side + 5], refs[2 * n_side + 5:])

    @pl.when(pl.program_id(1) == 0)
    def _():
        for k, w_ref in enumerate((wq_ref, wk_ref, wv_ref, wz_ref)):
            wb_ref[k] = w_ref[...].astype(BF16)

    tm = x_ref.shape[0]
    for r in range(tm // EPILOGUE_ROWS):
        rows = slice(r * EPILOGUE_ROWS, (r + 1) * EPILOGUE_ROWS)
        xb = x_ref[rows, :]
        v_t = _dot(xb, wb_ref[2]).T
        for h in range(v_t.shape[0] // HEAD_DIM):
            vt_ref[h * VT_ROWS:h * VT_ROWS + HEAD_DIM, rows] = (
                v_t[h * HEAD_DIM:(h + 1) * HEAD_DIM, :].astype(BF16))
            vt_ref[h * VT_ROWS + HEAD_DIM:(h + 1) * VT_ROWS, rows] = jnp.ones(
                (ONES_ROWS, EPILOGUE_ROWS), BF16)
        k = _dot(xb, wb_ref[1])
        k_ref[rows, :] = k.astype(BF16)
        for b in range(EPILOGUE_ROWS // MOBA_BLOCK):
            blk_row = r * (EPILOGUE_ROWS // MOBA_BLOCK) + b
            kmean_ref[blk_row:blk_row + 1, :] = jnp.mean(
                k[b * MOBA_BLOCK:(b + 1) * MOBA_BLOCK, :], axis=0, keepdims=True)
        zs_ref[rows, :] = _silu(_dot(xb, wb_ref[3])).astype(BF16)
        q_ref[rows, :] = (_dot(xb, wb_ref[0]) * (HEAD_DIM ** -0.5 * LOG2_E)).astype(BF16)
        if r == 0:
            _side_cast(side_in, side_out)


def _moba_in(xb, w, tm, side_sources):
    n = xb.shape[0]
    tn = MXU_COLS
    nj, ni = WIDTH // tn, n // tm
    out = jax.ShapeDtypeStruct((n, WIDTH), BF16)
    vt_rows = tn // HEAD_DIM * VT_ROWS
    side_in_specs, side_out_specs, side_out_shapes = _side_cast_specs(
        side_sources, nj * ni, lambda j, i: j * ni + i)

    def wspec(k):
        return pl.BlockSpec((D_MODEL, tn), lambda j, i: (0, j + k * nj))

    ospec = pl.BlockSpec((tm, tn), lambda j, i: (i, j))
    return pl.pallas_call(
        partial(_moba_in_kernel, len(side_sources)),
        out_shape=[out, out, jax.ShapeDtypeStruct((nj * vt_rows, n), BF16), out,
                   jax.ShapeDtypeStruct((n // MOBA_BLOCK, WIDTH), F32)] + side_out_shapes,
        grid=(nj, ni),
        in_specs=[pl.BlockSpec((tm, D_MODEL), lambda j, i: (i, 0)),
                  wspec(0), wspec(1), wspec(2), wspec(3)] + side_in_specs,
        out_specs=[ospec, ospec, pl.BlockSpec((vt_rows, tm), lambda j, i: (j, i)), ospec,
                   pl.BlockSpec((tm // MOBA_BLOCK, tn), lambda j, i: (i, j))] + side_out_specs,
        scratch_shapes=[pltpu.VMEM((4, D_MODEL, tn), BF16)],
        compiler_params=pltpu.CompilerParams(
            dimension_semantics=("arbitrary", "arbitrary"),
            vmem_limit_bytes=VMEM_LIMIT),
        name="moba_in",
    )(xb, w, w, w, w, *[arr for arr, _ in side_sources])


def _moba_attn_kernel(q_ref, k_ref, vt_ref, zs_ref, kmean_ref, y_ref):
    seq = k_ref.shape[0]
    n_heads = k_ref.shape[1] // HEAD_DIM
    nb = seq // MOBA_BLOCK
    blk = MOBA_BLOCK
    neg_inf = jnp.float32(-jnp.inf)

    key_i = lax.broadcasted_iota(jnp.int32, (blk, blk), 0)
    qry_i = lax.broadcasted_iota(jnp.int32, (blk, blk), 1)
    causal_t = key_i <= qry_i
    blk_i = lax.broadcasted_iota(jnp.int32, (nb, blk), 0)

    def hcols(h):
        return slice(h * HEAD_DIM, (h + 1) * HEAD_DIM)

    def qblock(own, h):
        return q_ref[own * blk:(own + 1) * blk, hcols(h)]

    def scores(own, h):
        q = qblock(own, h)
        return [_dot_nt(k_ref[n * blk:(n + 1) * blk, hcols(h)], q) for n in range(own + 1)]

    def stats(own, h, s_blocks):
        bias = None
        if own > MOBA_TOP_K:
            gate = _dot_nt(kmean_ref[:, hcols(h)], qblock(own, h).astype(F32))
            gate = jnp.where(blk_i < own, gate, neg_inf)
            cnt = jnp.zeros((nb, blk), jnp.int32)
            for m in range(own):
                gm = gate[m:m + 1, :]
                beats = (gm > gate) | ((gm == gate) & (m < blk_i))
                cnt = cnt + jnp.where(beats, 1, 0)
            bias = jnp.where((cnt < MOBA_TOP_K) & (blk_i < own), 0.0, neg_inf)

        s_blocks = s_blocks[:own] + [jnp.where(causal_t, s_blocks[own], neg_inf)]
        m_q = jnp.max(s_blocks[own], axis=0, keepdims=True)
        for n in range(own):
            bm = jnp.max(s_blocks[n], axis=0, keepdims=True)
            if bias is not None:
                bm = bm + bias[n:n + 1, :]
            m_q = jnp.maximum(m_q, bm)
        return s_blocks, m_q, bias

    def exp_pv(own, h, s_blocks, m_q, bias):
        acc = jnp.zeros((VT_ROWS, blk), F32)
        for n in range(own + 1):
            shift = m_q
            if bias is not None and n < own:
                shift = m_q - bias[n:n + 1, :]
            pn = jnp.exp2(s_blocks[n] - shift)
            vt = vt_ref[h * VT_ROWS:(h + 1) * VT_ROWS, n * blk:(n + 1) * blk]
            acc = acc + _dot(vt, pn.astype(BF16))
        o = (acc[0:HEAD_DIM, :] / acc[HEAD_DIM:HEAD_DIM + 1, :]).T
        rows = slice(own * blk, (own + 1) * blk)
        y_ref[rows, hcols(h)] = (o * zs_ref[rows, hcols(h)].astype(F32)).astype(BF16)

    units = [(own, h) for own in reversed(range(nb)) for h in range(n_heads)]
    pending = [scores(*units[0]), scores(*units[1])]
    st = stats(*units[0], pending.pop(0))
    for i, unit in enumerate(units):
        if i + 2 < len(units):
            pending.append(scores(*units[i + 2]))
        st_next = stats(*units[i + 1], pending.pop(0)) if i + 1 < len(units) else None
        exp_pv(*unit, *st)
        st = st_next


def _moba_attn(q, k, vt, zs, kmean, batch, seq, heads_per_step):
    spec = pl.BlockSpec((seq, heads_per_step * HEAD_DIM), lambda b, g: (b, g))
    return pl.pallas_call(
        _moba_attn_kernel,
        out_shape=jax.ShapeDtypeStruct((batch * seq, WIDTH), BF16),
        grid=(batch, HEADS // heads_per_step),
        in_specs=[spec, spec,
                  pl.BlockSpec((heads_per_step * VT_ROWS, seq), lambda b, g: (g, b)),
                  spec,
                  pl.BlockSpec((seq // MOBA_BLOCK, heads_per_step * HEAD_DIM),
                               lambda b, g: (b, g))],
        out_specs=spec,
        compiler_params=pltpu.CompilerParams(
            dimension_semantics=("arbitrary", "arbitrary"),
            vmem_limit_bytes=VMEM_LIMIT),
        name="moba_attn",
    )(q, k, vt, zs, kmean)


def kernel(x, p, w_in_a, sgu_norm_g, sgu_norm_b, w_s, b_s, w_in_b, w_out, ln_g, ln_b,
           w_ple_gate, w_ple_proj):
    batch, seq, d = x.shape
    n = batch * seq
    x2 = x.reshape(n, d)
    p2 = p.reshape(DEPTH, n, PLE_DIM)

    ln_g3, ln_b3 = ln_g[:, None, :], ln_b[:, None, :]

    y, wo0, wg0, wp_flat = _sgu(
        x2, w_in_a[0].astype(BF16), sgu_norm_g[0][None], sgu_norm_b[0][None], w_s[0], b_s[0].T,
        tm=SGU_TM,
        side_sources=[(w_out, 0), (w_ple_gate, 0),
                      (w_ple_proj.reshape(DEPTH * PLE_DIM, d), None)])
    wp = wp_flat.reshape(DEPTH, PLE_DIM, d)
    x2, x2b = _tail(0, y, x2, p2, wo0, ln_g3, ln_b3, wg0, wp, tm=TAIL_TM, with_bf16_copy=True)

    q, k, vt, zs, kmean, wo1, wg1 = _moba_in(
        x2b, w_in_b[0], tm=MOBA_IN_TM, side_sources=[(w_out, 1), (w_ple_gate, 1)])
    y = _moba_attn(q, k, vt, zs, kmean, batch, seq, heads_per_step=ATTN_HEADS_PER_STEP)
    (x2,) = _tail(1, y, x2, p2, wo1, ln_g3, ln_b3, wg1, wp, tm=TAIL_TM, with_bf16_copy=False)

    return x2.reshape(batch, seq, d)
```

```python
from functools import partial

import jax
import jax.numpy as jnp
from jax import lax
from jax.experimental import pallas as pl
from jax.experimental.pallas import tpu as pltpu

D_MODEL = 2048
WIDTH = 2048
SGU_CHUNK = 128
SGU_GROUPS = 16
HEADS = 16
HEAD_DIM = 128
MOBA_BLOCK = 256
MOBA_TOP_K = 3
PLE_DIM = 256
LN_EPS = 1e-5
DEPTH = 2
ALPHA = (2 * DEPTH) ** 0.25
LOG2_E = 1.4426950408889634
MXU_COLS = 256
LANES = 128
EPILOGUE_ROWS = 512
TAIL_ROWS = 256
SGU_ROWS = 512
BF16_ROWS = 16
ONES_ROWS = BF16_ROWS
VT_ROWS = HEAD_DIM + ONES_ROWS

BF16 = jnp.bfloat16
F32 = jnp.float32

VMEM_LIMIT = 56 * 1024 * 1024
SGU_TM = 512
SGU_W_SLAB = 64
TAIL_TM = 512
MOBA_IN_TM = 2048
ATTN_HEADS_PER_STEP = 4


def _dot(a, b):
    return jnp.dot(a, b, preferred_element_type=F32)


def _dot_nt(a, b):
    return lax.dot_general(a, b, (((1,), (1,)), ((), ())), preferred_element_type=F32)


def _silu(z):
    return z * jax.nn.sigmoid(z)


def _layer_norm(x, g, b):
    mu = jnp.mean(x, axis=-1, keepdims=True)
    xc = x - mu
    var = jnp.mean(xc * xc, axis=-1, keepdims=True)
    return xc * lax.rsqrt(var + LN_EPS) * g + b


def _side_cast_specs(sources, n_steps, step_index):
    in_specs, out_specs, out_shapes = [], [], []
    for arr, layer in sources:
        rows, cols = arr.shape[-2:]
        slab = rows // n_steps
        assert slab * n_steps == rows and slab % BF16_ROWS == 0, (rows, n_steps)
        if layer is None:
            in_specs.append(pl.BlockSpec((slab, cols), lambda *g: (step_index(*g), 0)))
        else:
            in_specs.append(pl.BlockSpec(
                (None, slab, cols), lambda *g, layer=layer: (layer, step_index(*g), 0)))
        out_specs.append(pl.BlockSpec((slab, cols), lambda *g: (step_index(*g), 0)))
        out_shapes.append(jax.ShapeDtypeStruct((rows, cols), BF16))
    return in_specs, out_specs, out_shapes


def _side_cast(side_in_refs, side_out_refs):
    for src_ref, dst_ref in zip(side_in_refs, side_out_refs):
        dst_ref[...] = src_ref[...].astype(BF16)


def _sgu_kernel(n_side, n_load, x_ref, wf_ref, ng_ref, nb_ref, ws_ref, bs_ref, *refs):
    side_in, (y_ref, *side_out), (w_ref, xb_ref, vg_ref, wc_ref) = (
        refs[:n_side], refs[n_side:2 * n_side + 1], refs[2 * n_side + 1:])
    step = pl.program_id(0)

    @pl.when(step < n_load)
    def _():
        slab = wf_ref.shape[0]
        w_ref[pl.ds(pl.multiple_of(step * slab, slab), slab), :] = wf_ref[...].astype(BF16)

    @pl.when(step == 0)
    def _():
        row = lax.broadcasted_iota(jnp.int32, (SGU_CHUNK, SGU_CHUNK), 0)
        col = lax.broadcasted_iota(jnp.int32, (SGU_CHUNK, SGU_CHUNK), 1)
        keep = (col <= row)[None]
        wc_ref[...] = jnp.where(keep, ws_ref[...], 0.0).astype(BF16)

    pl.when(step >= n_load)(partial(
        _sgu_tile, x_ref, w_ref, ng_ref, nb_ref, bs_ref, side_in, y_ref, side_out,
        xb_ref, vg_ref, wc_ref))


def _sgu_tile(x_ref, w_ref, ng_ref, nb_ref, bs_ref, side_in, y_ref, side_out,
              xb_ref, vg_ref, wc_ref):
    tm = x_ref.shape[0]
    n_pieces = WIDTH // MXU_COLS
    gw = WIDTH // SGU_GROUPS
    xb_ref[...] = x_ref[...].astype(BF16)

    def in_proj(path, c, epilogue):
        w_cols = slice(path * WIDTH + c * MXU_COLS, path * WIDTH + (c + 1) * MXU_COLS)
        parts = [epilogue(_dot(xb_ref[r * SGU_ROWS:(r + 1) * SGU_ROWS, :], w_ref[:, w_cols]))
                 for r in range(tm // SGU_ROWS)]
        return jnp.concatenate(parts, axis=0)

    def lane_fold(a):
        return a[:, 0:LANES] + a[:, LANES:2 * LANES]

    row_acc = jnp.zeros((tm, LANES), F32)
    for c in range(n_pieces):
        cols = slice(c * MXU_COLS, (c + 1) * MXU_COLS)
        vc = in_proj(1, c, jax.nn.gelu)
        vg_ref[:, cols] = vc
        row_acc = row_acc + lane_fold(vc)
    _side_cast(side_in, side_out)
    mu = jnp.sum(row_acc, axis=-1, keepdims=True) * (1.0 / WIDTH)
    sq_acc = jnp.zeros((tm, LANES), F32)
    for c in range(n_pieces):
        cols = slice(c * MXU_COLS, (c + 1) * MXU_COLS)
        dc = vg_ref[:, cols] - mu
        sq_acc = sq_acc + lane_fold(dc * dc)
    rstd = lax.rsqrt(jnp.sum(sq_acc, axis=-1, keepdims=True) * (1.0 / WIDTH) + LN_EPS)

    def uz_piece(c):
        return in_proj(0, c, jax.nn.gelu) * in_proj(2, c, _silu)

    uz_next = uz_piece(0)
    for c in range(n_pieces):
        cols = slice(c * MXU_COLS, (c + 1) * MXU_COLS)
        uz = uz_next
        if c + 1 < n_pieces:
            uz_next = uz_piece(c + 1)
        vn = ((vg_ref[:, cols] - mu) * rstd * ng_ref[:, cols] + nb_ref[:, cols]).astype(BF16)
        for k in range(tm // SGU_CHUNK):
            rows = slice(k * SGU_CHUNK, (k + 1) * SGU_CHUNK)
            for gi in range(MXU_COLS // gw):
                g = c * (MXU_COLS // gw) + gi
                gcols = slice(gi * gw, (gi + 1) * gw)
                s = _dot(wc_ref[g], vn[rows, gcols]) + bs_ref[:, g:g + 1]
                y_ref[rows, c * MXU_COLS + gi * gw:c * MXU_COLS + (gi + 1) * gw] = (
                    uz[rows, gcols] * s).astype(BF16)


def _sgu(x, w, ng, nb, ws, bs_t, tm, side_sources):
    n = x.shape[0]
    n_load = D_MODEL // SGU_W_SLAB
    tile = lambda s: jnp.maximum(s - n_load, 0)
    row = lambda s: (tile(s), 0)
    fix2 = lambda s: (0, 0)
    side_in_specs, side_out_specs, side_out_shapes = _side_cast_specs(
        side_sources, n // tm, tile)
    return pl.pallas_call(
        partial(_sgu_kernel, len(side_sources), n_load),
        out_shape=[jax.ShapeDtypeStruct((n, WIDTH), BF16)] + side_out_shapes,
        grid=(n_load + n // tm,),
        in_specs=[
            pl.BlockSpec((tm, D_MODEL), row),
            pl.BlockSpec((SGU_W_SLAB, 3 * WIDTH), lambda s: (jnp.minimum(s, n_load - 1), 0)),
            pl.BlockSpec((1, WIDTH), fix2),
            pl.BlockSpec((1, WIDTH), fix2),
            pl.BlockSpec((SGU_GROUPS, SGU_CHUNK, SGU_CHUNK), lambda s: (0, 0, 0)),
            pl.BlockSpec((SGU_CHUNK, SGU_GROUPS), fix2),
        ] + side_in_specs,
        out_specs=[pl.BlockSpec((tm, WIDTH), row)] + side_out_specs,
        scratch_shapes=[pltpu.VMEM((D_MODEL, 3 * WIDTH), BF16),
                        pltpu.VMEM((tm, D_MODEL), BF16),
                        pltpu.VMEM((tm, WIDTH), F32),
                        pltpu.VMEM((SGU_GROUPS, SGU_CHUNK, SGU_CHUNK), BF16)],
        compiler_params=pltpu.CompilerParams(
            dimension_semantics=("arbitrary",), vmem_limit_bytes=VMEM_LIMIT),
        name="sgu",
    )(x, w, ng, nb, ws, bs_t, *[arr for arr, _ in side_sources])


def _tail_kernel(y_ref, x_ref, p_ref, wo_ref, g_ref, b_ref, wg_ref, wp_ref, o_ref,
                 maybe_ob_ref=None):
    tm, d = o_ref.shape
    n_pieces = tm // TAIL_ROWS

    def pre_norm(r):
        rows = slice(r * TAIL_ROWS, (r + 1) * TAIL_ROWS)
        for c in range(d // MXU_COLS):
            cols = slice(c * MXU_COLS, (c + 1) * MXU_COLS)
            o_ref[rows, cols] = ALPHA * x_ref[rows, cols] + _dot(y_ref[rows, :], wo_ref[:, cols])

    pre_norm(0)
    for r in range(n_pieces):
        rows = slice(r * TAIL_ROWS, (r + 1) * TAIL_ROWS)
        if r + 1 < n_pieces:
            pre_norm(r + 1)
        xn = _layer_norm(o_ref[rows, :], g_ref[...], b_ref[...])
        xnb = xn.astype(BF16)
        pb = p_ref[rows, :].astype(BF16)
        for c in range(d // MXU_COLS):
            cols = slice(c * MXU_COLS, (c + 1) * MXU_COLS)
            gate = jax.nn.sigmoid(_dot(xnb, wg_ref[:, cols]))
            proj = _dot(pb, wp_ref[:, cols])
            out = xn[:, cols] + gate * proj
            o_ref[rows, cols] = out
            if maybe_ob_ref is not None:
                maybe_ob_ref[rows, cols] = out.astype(BF16)


def _tail(layer, y, x, p, wo, g, b, wg, wp, tm, with_bf16_copy):
    n = x.shape[0]
    row = lambda i: (i, 0)
    fix = lambda i: (layer, 0, 0)
    fix2 = lambda i: (0, 0)
    once = pl.Buffered(1)
    out_shape = [jax.ShapeDtypeStruct((n, D_MODEL), F32)]
    out_specs = [pl.BlockSpec((tm, D_MODEL), row)]
    if with_bf16_copy:
        out_shape.append(jax.ShapeDtypeStruct((n, D_MODEL), BF16))
        out_specs.append(pl.BlockSpec((tm, D_MODEL), row))
    return pl.pallas_call(
        _tail_kernel,
        out_shape=out_shape,
        grid=(n // tm,),
        in_specs=[
            pl.BlockSpec((tm, WIDTH), row),
            pl.BlockSpec((tm, D_MODEL), row),
            pl.BlockSpec((None, tm, PLE_DIM), lambda i: (layer, i, 0)),
            pl.BlockSpec((WIDTH, D_MODEL), fix2, pipeline_mode=once),
            pl.BlockSpec((None, 1, D_MODEL), fix),
            pl.BlockSpec((None, 1, D_MODEL), fix),
            pl.BlockSpec((D_MODEL, D_MODEL), fix2, pipeline_mode=once),
            pl.BlockSpec((None, PLE_DIM, D_MODEL), fix, pipeline_mode=once),
        ],
        out_specs=out_specs,
        compiler_params=pltpu.CompilerParams(
            dimension_semantics=("arbitrary",), vmem_limit_bytes=VMEM_LIMIT),
        name="layer_tail",
    )(y, x, p, wo, g, b, wg, wp)


def _moba_in_kernel(n_side, x_ref, wq_ref, wk_ref, wv_ref, wz_ref, *refs):
    side_in, (q_ref, k_ref, v_ref, zs_ref, *side_out), (wb_ref,) = (
        refs[:n_side], refs[n_side:2 * n_side + 4], refs[2 * n_side + 4:])

    @pl.when(pl.program_id(1) == 0)
    def _():
        for k, w_ref in enumerate((wq_ref, wk_ref, wv_ref, wz_ref)):
            wb_ref[k] = w_ref[...].astype(BF16)

    tm = x_ref.shape[0]
    for r in range(tm // EPILOGUE_ROWS):
        rows = slice(r * EPILOGUE_ROWS, (r + 1) * EPILOGUE_ROWS)
        xb = x_ref[rows, :]
        zs_ref[rows, :] = _silu(_dot(xb, wb_ref[3])).astype(BF16)
        q_ref[rows, :] = (_dot(xb, wb_ref[0]) * (HEAD_DIM ** -0.5 * LOG2_E)).astype(BF16)
        k_ref[rows, :] = _dot(xb, wb_ref[1]).astype(BF16)
        v_ref[rows, :] = _dot(xb, wb_ref[2]).astype(BF16)
        if r == 0:
            _side_cast(side_in, side_out)


def _moba_in(xb, w, tm, side_sources):
    n = xb.shape[0]
    tn = MXU_COLS
    nj, ni = WIDTH // tn, n // tm
    out = jax.ShapeDtypeStruct((n, WIDTH), BF16)
    side_in_specs, side_out_specs, side_out_shapes = _side_cast_specs(
        side_sources, nj * ni, lambda j, i: j * ni + i)

    def wspec(k):
        return pl.BlockSpec((D_MODEL, tn), lambda j, i: (0, j + k * nj))

    ospec = pl.BlockSpec((tm, tn), lambda j, i: (i, j))
    return pl.pallas_call(
        partial(_moba_in_kernel, len(side_sources)),
        out_shape=[out, out, out, out] + side_out_shapes,
        grid=(nj, ni),
        in_specs=[pl.BlockSpec((tm, D_MODEL), lambda j, i: (i, 0)),
                  wspec(0), wspec(1), wspec(2), wspec(3)] + side_in_specs,
        out_specs=[ospec, ospec, ospec, ospec] + side_out_specs,
        scratch_shapes=[pltpu.VMEM((4, D_MODEL, tn), BF16)],
        compiler_params=pltpu.CompilerParams(
            dimension_semantics=("arbitrary", "arbitrary"),
            vmem_limit_bytes=VMEM_LIMIT),
        name="moba_in",
    )(xb, w, w, w, w, *[arr for arr, _ in side_sources])


def _moba_attn_kernel(q_ref, k_ref, v_ref, zs_ref, y_ref, vt_ref):
    seq = k_ref.shape[0]
    n_heads = k_ref.shape[1] // HEAD_DIM
    nb = seq // MOBA_BLOCK
    blk = MOBA_BLOCK
    neg_inf = jnp.float32(-jnp.inf)

    key_i = lax.broadcasted_iota(jnp.int32, (blk, blk), 0)
    qry_i = lax.broadcasted_iota(jnp.int32, (blk, blk), 1)
    causal_t = key_i <= qry_i
    blk_i = lax.broadcasted_iota(jnp.int32, (nb, blk), 0)

    def hcols(h):
        return slice(h * HEAD_DIM, (h + 1) * HEAD_DIM)

    def qblock(own, h):
        return q_ref[own * blk:(own + 1) * blk, hcols(h)]

    def scores(own, h):
        q = qblock(own, h)
        return [_dot_nt(k_ref[n * blk:(n + 1) * blk, hcols(h)], q) for n in range(own + 1)]

    def stats(own, h, s_blocks):
        bias = None
        if own > MOBA_TOP_K:
            gate = _dot_nt(k_means[h], qblock(own, h).astype(F32))
            gate = jnp.where(blk_i < own, gate, neg_inf)
            cnt = jnp.zeros((nb, blk), jnp.int32)
            for m in range(own):
                gm = gate[m:m + 1, :]
                beats = (gm > gate) | ((gm == gate) & (m < blk_i))
                cnt = cnt + jnp.where(beats, 1, 0)
            bias = jnp.where((cnt < MOBA_TOP_K) & (blk_i < own), 0.0, neg_inf)

        s_blocks = s_blocks[:own] + [jnp.where(causal_t, s_blocks[own], neg_inf)]
        m_q = jnp.max(s_blocks[own], axis=0, keepdims=True)
        for n in range(own):
            bm = jnp.max(s_blocks[n], axis=0, keepdims=True)
            if bias is not None:
                bm = bm + bias[n:n + 1, :]
            m_q = jnp.maximum(m_q, bm)
        return s_blocks, m_q, bias

    def exp_pv(own, h, s_blocks, m_q, bias):
        acc = jnp.zeros((VT_ROWS, blk), F32)
        for n in range(own + 1):
            shift = m_q
            if bias is not None and n < own:
                shift = m_q - bias[n:n + 1, :]
            pn = jnp.exp2(s_blocks[n] - shift)
            acc = acc + _dot(vt_ref[h, :, n * blk:(n + 1) * blk], pn.astype(BF16))
        o = (acc[0:HEAD_DIM, :] / acc[HEAD_DIM:HEAD_DIM + 1, :]).T
        rows = slice(own * blk, (own + 1) * blk)
        y_ref[rows, hcols(h)] = (o * zs_ref[rows, hcols(h)].astype(F32)).astype(BF16)

    units = [(own, h) for own in reversed(range(nb)) for h in range(n_heads)]
    pending = [scores(*units[0]), scores(*units[1])]

    k_means = []
    for h in range(n_heads):
        for n in range(nb):
            vt_ref[h, 0:HEAD_DIM, n * blk:(n + 1) * blk] = (
                v_ref[n * blk:(n + 1) * blk, hcols(h)].astype(F32).T.astype(BF16))
        vt_ref[h, HEAD_DIM:VT_ROWS, :] = jnp.ones((ONES_ROWS, seq), BF16)
        kf = k_ref[:, hcols(h)].astype(F32)
        k_means.append(jnp.mean(kf.reshape(nb, blk, HEAD_DIM), axis=1))

    st = stats(*units[0], pending.pop(0))
    for i, unit in enumerate(units):
        if i + 2 < len(units):
            pending.append(scores(*units[i + 2]))
        st_next = stats(*units[i + 1], pending.pop(0)) if i + 1 < len(units) else None
        exp_pv(*unit, *st)
        st = st_next


def _moba_attn(q, k, v, zs, batch, seq, heads_per_step):
    spec = pl.BlockSpec((seq, heads_per_step * HEAD_DIM), lambda b, g: (b, g))
    return pl.pallas_call(
        _moba_attn_kernel,
        out_shape=jax.ShapeDtypeStruct((batch * seq, WIDTH), BF16),
        grid=(batch, HEADS // heads_per_step),
        in_specs=[spec, spec, spec, spec],
        out_specs=spec,
        scratch_shapes=[pltpu.VMEM((heads_per_step, VT_ROWS, seq), BF16)],
        compiler_params=pltpu.CompilerParams(
            dimension_semantics=("arbitrary", "arbitrary"),
            vmem_limit_bytes=VMEM_LIMIT),
        name="moba_attn",
    )(q, k, v, zs)


def kernel(x, p, w_in_a, sgu_norm_g, sgu_norm_b, w_s, b_s, w_in_b, w_out, ln_g, ln_b,
           w_ple_gate, w_ple_proj):
    batch, seq, d = x.shape
    n = batch * seq
    x2 = x.reshape(n, d)
    p2 = p.reshape(DEPTH, n, PLE_DIM)

    ln_g3, ln_b3 = ln_g[:, None, :], ln_b[:, None, :]

    y, wo0, wg0, wp_flat = _sgu(
        x2, w_in_a[0], sgu_norm_g[0][None], sgu_norm_b[0][None], w_s[0], b_s[0].T,
        tm=SGU_TM,
        side_sources=[(w_out, 0), (w_ple_gate, 0),
                      (w_ple_proj.reshape(DEPTH * PLE_DIM, d), None)])
    wp = wp_flat.reshape(DEPTH, PLE_DIM, d)
    x2, x2b = _tail(0, y, x2, p2, wo0, ln_g3, ln_b3, wg0, wp, tm=TAIL_TM, with_bf16_copy=True)

    q, k, v, zs, wo1, wg1 = _moba_in(
        x2b, w_in_b[0], tm=MOBA_IN_TM, side_sources=[(w_out, 1), (w_ple_gate, 1)])
    y = _moba_attn(q, k, v, zs, batch, seq, heads_per_step=ATTN_HEADS_PER_STEP)
    (x2,) = _tail(1, y, x2, p2, wo1, ln_g3, ln_b3, wg1, wp, tm=TAIL_TM, with_bf16_copy=False)

    return x2.reshape(batch, seq, d)
```

```python
from functools import partial

import jax
import jax.numpy as jnp
from jax import lax
from jax.experimental import pallas as pl
from jax.experimental.pallas import tpu as pltpu

D_MODEL = 2048
WIDTH = 2048
SGU_CHUNK = 128
SGU_GROUPS = 16
HEADS = 16
HEAD_DIM = 128
MOBA_BLOCK = 256
MOBA_TOP_K = 3
PLE_DIM = 256
LN_EPS = 1e-5
DEPTH = 2
ALPHA = (2 * DEPTH) ** 0.25
LOG2_E = 1.4426950408889634
MXU_COLS = 256
LANES = 128
EPILOGUE_ROWS = 512
TAIL_ROWS = 256
SGU_ROWS = 512
BF16_ROWS = 16
ONES_ROWS = BF16_ROWS
VT_ROWS = HEAD_DIM + ONES_ROWS

BF16 = jnp.bfloat16
F32 = jnp.float32

VMEM_LIMIT = 56 * 1024 * 1024
SGU_VMEM_LIMIT = 59 * 1024 * 1024
SGU_TM = 512
SGU_W_SLAB = 128
TAIL_TM = 512
MOBA_IN_TM = 2048
ATTN_HEADS_PER_STEP = 4


def _dot(a, b):
    return jnp.dot(a, b, preferred_element_type=F32)


def _dot_nt(a, b):
    return lax.dot_general(a, b, (((1,), (1,)), ((), ())), preferred_element_type=F32)


def _silu(z):
    return z * jax.nn.sigmoid(z)


def _layer_norm(x, g, b):
    mu = jnp.mean(x, axis=-1, keepdims=True)
    xc = x - mu
    var = jnp.mean(xc * xc, axis=-1, keepdims=True)
    return xc * lax.rsqrt(var + LN_EPS) * g + b


def _side_cast_specs(sources, n_steps, step_index):
    in_specs, out_specs, out_shapes = [], [], []
    for arr, layer in sources:
        rows, cols = arr.shape[-2:]
        slab = rows // n_steps
        assert slab * n_steps == rows and slab % BF16_ROWS == 0, (rows, n_steps)
        if layer is None:
            in_specs.append(pl.BlockSpec((slab, cols), lambda *g: (step_index(*g), 0)))
        else:
            in_specs.append(pl.BlockSpec(
                (None, slab, cols), lambda *g, layer=layer: (layer, step_index(*g), 0)))
        out_specs.append(pl.BlockSpec((slab, cols), lambda *g: (step_index(*g), 0)))
        out_shapes.append(jax.ShapeDtypeStruct((rows, cols), BF16))
    return in_specs, out_specs, out_shapes


def _side_cast(side_in_refs, side_out_refs):
    for src_ref, dst_ref in zip(side_in_refs, side_out_refs):
        dst_ref[...] = src_ref[...].astype(BF16)


def _sgu_kernel(n_side, n_load, x_ref, wf_ref, ng_ref, nb_ref, ws_ref, bs_ref, *refs):
    side_in, (y_ref, *side_out), (w_ref, xb_ref, vg_ref, wc_ref) = (
        refs[:n_side], refs[n_side:2 * n_side + 1], refs[2 * n_side + 1:])
    step = pl.program_id(0)

    @pl.when(step < n_load)
    def _():
        slab = wf_ref.shape[0]
        w_ref[pl.ds(pl.multiple_of(step * slab, slab), slab), :] = wf_ref[...].astype(BF16)

    @pl.when(step == 0)
    def _():
        row = lax.broadcasted_iota(jnp.int32, (SGU_CHUNK, SGU_CHUNK), 0)
        col = lax.broadcasted_iota(jnp.int32, (SGU_CHUNK, SGU_CHUNK), 1)
        keep = (col <= row)[None]
        wc_ref[...] = jnp.where(keep, ws_ref[...], 0.0).astype(BF16)

    pl.when(step >= n_load)(partial(
        _sgu_tile, x_ref, w_ref, ng_ref, nb_ref, bs_ref, side_in, y_ref, side_out,
        xb_ref, vg_ref, wc_ref))


def _sgu_tile(x_ref, w_ref, ng_ref, nb_ref, bs_ref, side_in, y_ref, side_out,
              xb_ref, vg_ref, wc_ref):
    tm = x_ref.shape[0]
    n_pieces = WIDTH // MXU_COLS
    gw = WIDTH // SGU_GROUPS
    xb_ref[...] = x_ref[...].astype(BF16)

    def in_proj(path, c, epilogue):
        w_cols = slice(path * WIDTH + c * MXU_COLS, path * WIDTH + (c + 1) * MXU_COLS)
        parts = [epilogue(_dot(xb_ref[r * SGU_ROWS:(r + 1) * SGU_ROWS, :], w_ref[:, w_cols]))
                 for r in range(tm // SGU_ROWS)]
        return jnp.concatenate(parts, axis=0)

    def lane_fold(a):
        return a[:, 0:LANES] + a[:, LANES:2 * LANES]

    row_acc = jnp.zeros((tm, LANES), F32)
    for c in range(n_pieces):
        cols = slice(c * MXU_COLS, (c + 1) * MXU_COLS)
        vc = in_proj(1, c, jax.nn.gelu)
        vg_ref[:, cols] = vc
        row_acc = row_acc + lane_fold(vc)
    _side_cast(side_in, side_out)
    mu = jnp.sum(row_acc, axis=-1, keepdims=True) * (1.0 / WIDTH)
    sq_acc = jnp.zeros((tm, LANES), F32)
    for c in range(n_pieces):
        cols = slice(c * MXU_COLS, (c + 1) * MXU_COLS)
        dc = vg_ref[:, cols] - mu
        sq_acc = sq_acc + lane_fold(dc * dc)
    rstd = lax.rsqrt(jnp.sum(sq_acc, axis=-1, keepdims=True) * (1.0 / WIDTH) + LN_EPS)

    def uz_piece(c):
        return in_proj(0, c, jax.nn.gelu) * in_proj(2, c, _silu)

    uz_next = uz_piece(0)
    for c in range(n_pieces):
        cols = slice(c * MXU_COLS, (c + 1) * MXU_COLS)
        uz = uz_next
        if c + 1 < n_pieces:
            uz_next = uz_piece(c + 1)
        vn = ((vg_ref[:, cols] - mu) * rstd * ng_ref[:, cols] + nb_ref[:, cols]).astype(BF16)
        for k in range(tm // SGU_CHUNK):
            rows = slice(k * SGU_CHUNK, (k + 1) * SGU_CHUNK)
            for gi in range(MXU_COLS // gw):
                g = c * (MXU_COLS // gw) + gi
                gcols = slice(gi * gw, (gi + 1) * gw)
                s = _dot(wc_ref[g], vn[rows, gcols]) + bs_ref[:, g:g + 1]
                y_ref[rows, c * MXU_COLS + gi * gw:c * MXU_COLS + (gi + 1) * gw] = (
                    uz[rows, gcols] * s).astype(BF16)


def _sgu(x, w, ng, nb, ws, bs_t, tm, side_sources):
    n = x.shape[0]
    n_load = D_MODEL // SGU_W_SLAB
    tile = lambda s: jnp.maximum(s - n_load, 0)
    row = lambda s: (tile(s), 0)
    fix2 = lambda s: (0, 0)
    side_in_specs, side_out_specs, side_out_shapes = _side_cast_specs(
        side_sources, n // tm, tile)
    return pl.pallas_call(
        partial(_sgu_kernel, len(side_sources), n_load),
        out_shape=[jax.ShapeDtypeStruct((n, WIDTH), BF16)] + side_out_shapes,
        grid=(n_load + n // tm,),
        in_specs=[
            pl.BlockSpec((tm, D_MODEL), row),
            pl.BlockSpec((SGU_W_SLAB, 3 * WIDTH), lambda s: (jnp.minimum(s, n_load - 1), 0)),
            pl.BlockSpec((1, WIDTH), fix2),
            pl.BlockSpec((1, WIDTH), fix2),
            pl.BlockSpec((SGU_GROUPS, SGU_CHUNK, SGU_CHUNK), lambda s: (0, 0, 0)),
            pl.BlockSpec((SGU_CHUNK, SGU_GROUPS), fix2),
        ] + side_in_specs,
        out_specs=[pl.BlockSpec((tm, WIDTH), row)] + side_out_specs,
        scratch_shapes=[pltpu.VMEM((D_MODEL, 3 * WIDTH), BF16),
                        pltpu.VMEM((tm, D_MODEL), BF16),
                        pltpu.VMEM((tm, WIDTH), F32),
                        pltpu.VMEM((SGU_GROUPS, SGU_CHUNK, SGU_CHUNK), BF16)],
        compiler_params=pltpu.CompilerParams(
            dimension_semantics=("arbitrary",), vmem_limit_bytes=SGU_VMEM_LIMIT),
        name="sgu",
    )(x, w, ng, nb, ws, bs_t, *[arr for arr, _ in side_sources])


def _tail_kernel(y_ref, x_ref, p_ref, wo_ref, g_ref, b_ref, wg_ref, wp_ref, o_ref,
                 maybe_ob_ref=None):
    tm, d = o_ref.shape
    n_pieces = tm // TAIL_ROWS

    def pre_norm(r):
        rows = slice(r * TAIL_ROWS, (r + 1) * TAIL_ROWS)
        for c in range(d // MXU_COLS):
            cols = slice(c * MXU_COLS, (c + 1) * MXU_COLS)
            o_ref[rows, cols] = ALPHA * x_ref[rows, cols] + _dot(y_ref[rows, :], wo_ref[:, cols])

    pre_norm(0)
    for r in range(n_pieces):
        rows = slice(r * TAIL_ROWS, (r + 1) * TAIL_ROWS)
        if r + 1 < n_pieces:
            pre_norm(r + 1)
        xn = _layer_norm(o_ref[rows, :], g_ref[...], b_ref[...])
        xnb = xn.astype(BF16)
        pb = p_ref[rows, :].astype(BF16)
        for c in range(d // MXU_COLS):
            cols = slice(c * MXU_COLS, (c + 1) * MXU_COLS)
            gate = jax.nn.sigmoid(_dot(xnb, wg_ref[:, cols]))
            proj = _dot(pb, wp_ref[:, cols])
            out = xn[:, cols] + gate * proj
            o_ref[rows, cols] = out
            if maybe_ob_ref is not None:
                maybe_ob_ref[rows, cols] = out.astype(BF16)


def _tail(layer, y, x, p, wo, g, b, wg, wp, tm, with_bf16_copy):
    n = x.shape[0]
    row = lambda i: (i, 0)
    fix = lambda i: (layer, 0, 0)
    fix2 = lambda i: (0, 0)
    once = pl.Buffered(1)
    out_shape = [jax.ShapeDtypeStruct((n, D_MODEL), F32)]
    out_specs = [pl.BlockSpec((tm, D_MODEL), row)]
    if with_bf16_copy:
        out_shape.append(jax.ShapeDtypeStruct((n, D_MODEL), BF16))
        out_specs.append(pl.BlockSpec((tm, D_MODEL), row))
    return pl.pallas_call(
        _tail_kernel,
        out_shape=out_shape,
        grid=(n // tm,),
        in_specs=[
            pl.BlockSpec((tm, WIDTH), row),
            pl.BlockSpec((tm, D_MODEL), row),
            pl.BlockSpec((None, tm, PLE_DIM), lambda i: (layer, i, 0)),
            pl.BlockSpec((WIDTH, D_MODEL), fix2, pipeline_mode=once),
            pl.BlockSpec((None, 1, D_MODEL), fix),
            pl.BlockSpec((None, 1, D_MODEL), fix),
            pl.BlockSpec((D_MODEL, D_MODEL), fix2, pipeline_mode=once),
            pl.BlockSpec((None, PLE_DIM, D_MODEL), fix, pipeline_mode=once),
        ],
        out_specs=out_specs,
        compiler_params=pltpu.CompilerParams(
            dimension_semantics=("arbitrary",), vmem_limit_bytes=VMEM_LIMIT),
        name="layer_tail",
    )(y, x, p, wo, g, b, wg, wp)


def _moba_in_kernel(n_side, x_ref, wq_ref, wk_ref, wv_ref, wz_ref, *refs):
    side_in, (q_ref, k_ref, v_ref, zs_ref, *side_out), (wb_ref,) = (
        refs[:n_side], refs[n_side:2 * n_side + 4], refs[2 * n_side + 4:])

    @pl.when(pl.program_id(1) == 0)
    def _():
        for k, w_ref in enumerate((wq_ref, wk_ref, wv_ref, wz_ref)):
            wb_ref[k] = w_ref[...].astype(BF16)

    tm = x_ref.shape[0]
    for r in range(tm // EPILOGUE_ROWS):
        rows = slice(r * EPILOGUE_ROWS, (r + 1) * EPILOGUE_ROWS)
        xb = x_ref[rows, :]
        zs_ref[rows, :] = _silu(_dot(xb, wb_ref[3])).astype(BF16)
        q_ref[rows, :] = (_dot(xb, wb_ref[0]) * (HEAD_DIM ** -0.5 * LOG2_E)).astype(BF16)
        k_ref[rows, :] = _dot(xb, wb_ref[1]).astype(BF16)
        v_ref[rows, :] = _dot(xb, wb_ref[2]).astype(BF16)
        if r == 0:
            _side_cast(side_in, side_out)


def _moba_in(xb, w, tm, side_sources):
    n = xb.shape[0]
    tn = MXU_COLS
    nj, ni = WIDTH // tn, n // tm
    out = jax.ShapeDtypeStruct((n, WIDTH), BF16)
    side_in_specs, side_out_specs, side_out_shapes = _side_cast_specs(
        side_sources, nj * ni, lambda j, i: j * ni + i)

    def wspec(k):
        return pl.BlockSpec((D_MODEL, tn), lambda j, i: (0, j + k * nj))

    ospec = pl.BlockSpec((tm, tn), lambda j, i: (i, j))
    return pl.pallas_call(
        partial(_moba_in_kernel, len(side_sources)),
        out_shape=[out, out, out, out] + side_out_shapes,
        grid=(nj, ni),
        in_specs=[pl.BlockSpec((tm, D_MODEL), lambda j, i: (i, 0)),
                  wspec(0), wspec(1), wspec(2), wspec(3)] + side_in_specs,
        out_specs=[ospec, ospec, ospec, ospec] + side_out_specs,
        scratch_shapes=[pltpu.VMEM((4, D_MODEL, tn), BF16)],
        compiler_params=pltpu.CompilerParams(
            dimension_semantics=("arbitrary", "arbitrary"),
            vmem_limit_bytes=VMEM_LIMIT),
        name="moba_in",
    )(xb, w, w, w, w, *[arr for arr, _ in side_sources])


def _moba_attn_kernel(q_ref, k_ref, v_ref, zs_ref, y_ref, vt_ref):
    seq = k_ref.shape[0]
    n_heads = k_ref.shape[1] // HEAD_DIM
    nb = seq // MOBA_BLOCK
    blk = MOBA_BLOCK
    neg_inf = jnp.float32(-jnp.inf)

    key_i = lax.broadcasted_iota(jnp.int32, (blk, blk), 0)
    qry_i = lax.broadcasted_iota(jnp.int32, (blk, blk), 1)
    causal_t = key_i <= qry_i
    blk_i = lax.broadcasted_iota(jnp.int32, (nb, blk), 0)

    def hcols(h):
        return slice(h * HEAD_DIM, (h + 1) * HEAD_DIM)

    def qblock(own, h):
        return q_ref[own * blk:(own + 1) * blk, hcols(h)]

    def scores(own, h):
        q = qblock(own, h)
        return [_dot_nt(k_ref[n * blk:(n + 1) * blk, hcols(h)], q) for n in range(own + 1)]

    def stats(own, h, s_blocks):
        bias = None
        if own > MOBA_TOP_K:
            gate = _dot_nt(k_means[h], qblock(own, h).astype(F32))
            gate = jnp.where(blk_i < own, gate, neg_inf)
            cnt = jnp.zeros((nb, blk), jnp.int32)
            for m in range(own):
                gm = gate[m:m + 1, :]
                beats = (gm > gate) | ((gm == gate) & (m < blk_i))
                cnt = cnt + jnp.where(beats, 1, 0)
            bias = jnp.where((cnt < MOBA_TOP_K) & (blk_i < own), 0.0, neg_inf)

        s_blocks = s_blocks[:own] + [jnp.where(causal_t, s_blocks[own], neg_inf)]
        m_q = jnp.max(s_blocks[own], axis=0, keepdims=True)
        for n in range(own):
            bm = jnp.max(s_blocks[n], axis=0, keepdims=True)
            if bias is not None:
                bm = bm + bias[n:n + 1, :]
            m_q = jnp.maximum(m_q, bm)
        return s_blocks, m_q, bias

    def exp_pv(own, h, s_blocks, m_q, bias):
        acc = jnp.zeros((VT_ROWS, blk), F32)
        for n in range(own + 1):
            shift = m_q
            if bias is not None and n < own:
                shift = m_q - bias[n:n + 1, :]
            pn = jnp.exp2(s_blocks[n] - shift)
            acc = acc + _dot(vt_ref[h, :, n * blk:(n + 1) * blk], pn.astype(BF16))
        o = (acc[0:HEAD_DIM, :] / acc[HEAD_DIM:HEAD_DIM + 1, :]).T
        rows = slice(own * blk, (own + 1) * blk)
        y_ref[rows, hcols(h)] = (o * zs_ref[rows, hcols(h)].astype(F32)).astype(BF16)

    units = [(own, h) for own in reversed(range(nb)) for h in range(n_heads)]
    pending = [scores(*units[0]), scores(*units[1])]

    k_means = []
    for h in range(n_heads):
        for n in range(nb):
            vt_ref[h, 0:HEAD_DIM, n * blk:(n + 1) * blk] = (
                v_ref[n * blk:(n + 1) * blk, hcols(h)].astype(F32).T.astype(BF16))
        vt_ref[h, HEAD_DIM:VT_ROWS, :] = jnp.ones((ONES_ROWS, seq), BF16)
        kf = k_ref[:, hcols(h)].astype(F32)
        k_means.append(jnp.mean(kf.reshape(nb, blk, HEAD_DIM), axis=1))

    st = stats(*units[0], pending.pop(0))
    for i, unit in enumerate(units):
        if i + 2 < len(units):
            pending.append(scores(*units[i + 2]))
        st_next = stats(*units[i + 1], pending.pop(0)) if i + 1 < len(units) else None
        exp_pv(*unit, *st)
        st = st_next


def _moba_attn(q, k, v, zs, batch, seq, heads_per_step):
    spec = pl.BlockSpec((seq, heads_per_step * HEAD_DIM), lambda b, g: (b, g))
    return pl.pallas_call(
        _moba_attn_kernel,
        out_shape=jax.ShapeDtypeStruct((batch * seq, WIDTH), BF16),
        grid=(batch, HEADS // heads_per_step),
        in_specs=[spec, spec, spec, spec],
        out_specs=spec,
        scratch_shapes=[pltpu.VMEM((heads_per_step, VT_ROWS, seq), BF16)],
        compiler_params=pltpu.CompilerParams(
            dimension_semantics=("arbitrary", "arbitrary"),
            vmem_limit_bytes=VMEM_LIMIT),
        name="moba_attn",
    )(q, k, v, zs)


def kernel(x, p, w_in_a, sgu_norm_g, sgu_norm_b, w_s, b_s, w_in_b, w_out, ln_g, ln_b,
           w_ple_gate, w_ple_proj):
    batch, seq, d = x.shape
    n = batch * seq
    x2 = x.reshape(n, d)
    p2 = p.reshape(DEPTH, n, PLE_DIM)

    ln_g3, ln_b3 = ln_g[:, None, :], ln_b[:, None, :]

    y, wo0, wg0, wp_flat = _sgu(
        x2, w_in_a[0], sgu_norm_g[0][None], sgu_norm_b[0][None], w_s[0], b_s[0].T,
        tm=SGU_TM,
        side_sources=[(w_out, 0), (w_ple_gate, 0),
                      (w_ple_proj.reshape(DEPTH * PLE_DIM, d), None)])
    wp = wp_flat.reshape(DEPTH, PLE_DIM, d)
    x2, x2b = _tail(0, y, x2, p2, wo0, ln_g3, ln_b3, wg0, wp, tm=TAIL_TM, with_bf16_copy=True)

    q, k, v, zs, wo1, wg1 = _moba_in(
        x2b, w_in_b[0], tm=MOBA_IN_TM, side_sources=[(w_out, 1), (w_ple_gate, 1)])
    y = _moba_attn(q, k, v, zs, batch, seq, heads_per_step=ATTN_HEADS_PER_STEP)
    (x2,) = _tail(1, y, x2, p2, wo1, ln_g3, ln_b3, wg1, wp, tm=TAIL_TM, with_bf16_copy=False)

    return x2.reshape(batch, seq, d)
```

```python
from functools import partial

import jax
import jax.numpy as jnp
from jax import lax
from jax.experimental import pallas as pl
from jax.experimental.pallas import tpu as pltpu

D_MODEL = 2048
WIDTH = 2048
SGU_CHUNK = 128
SGU_GROUPS = 16
HEADS = 16
HEAD_DIM = 128
MOBA_BLOCK = 256
MOBA_TOP_K = 3
PLE_DIM = 256
LN_EPS = 1e-5
DEPTH = 2
ALPHA = (2 * DEPTH) ** 0.25
LOG2_E = 1.4426950408889634
MXU_COLS = 256
LANES = 128
EPILOGUE_ROWS = 512
TAIL_ROWS = 256
SGU_ROWS = 512
BF16_ROWS = 16
ONES_ROWS = BF16_ROWS
VT_ROWS = HEAD_DIM + ONES_ROWS

BF16 = jnp.bfloat16
F32 = jnp.float32

VMEM_LIMIT = 56 * 1024 * 1024
SGU_VMEM_LIMIT = 59 * 1024 * 1024
SGU_TM = 512
SGU_W_SLAB = 128
TAIL_TM = 512
MOBA_IN_TM = 2048
ATTN_HEADS_PER_STEP = 4


def _dot(a, b):
    return jnp.dot(a, b, preferred_element_type=F32)


def _dot_nt(a, b):
    return lax.dot_general(a, b, (((1,), (1,)), ((), ())), preferred_element_type=F32)


def _silu(z):
    return z * jax.nn.sigmoid(z)


def _layer_norm(x, g, b):
    mu = jnp.mean(x, axis=-1, keepdims=True)
    xc = x - mu
    var = jnp.mean(xc * xc, axis=-1, keepdims=True)
    return xc * lax.rsqrt(var + LN_EPS) * g + b


def _side_cast_specs(sources, n_steps, step_index):
    in_specs, out_specs, out_shapes = [], [], []
    for arr, layer in sources:
        rows, cols = arr.shape[-2:]
        slab = rows // n_steps
        assert slab * n_steps == rows and slab % BF16_ROWS == 0, (rows, n_steps)
        if layer is None:
            in_specs.append(pl.BlockSpec((slab, cols), lambda *g: (step_index(*g), 0)))
        else:
            in_specs.append(pl.BlockSpec(
                (None, slab, cols), lambda *g, layer=layer: (layer, step_index(*g), 0)))
        out_specs.append(pl.BlockSpec((slab, cols), lambda *g: (step_index(*g), 0)))
        out_shapes.append(jax.ShapeDtypeStruct((rows, cols), BF16))
    return in_specs, out_specs, out_shapes


def _side_cast(side_in_refs, side_out_refs):
    for src_ref, dst_ref in zip(side_in_refs, side_out_refs):
        dst_ref[...] = src_ref[...].astype(BF16)


def _sgu_kernel(n_side, n_load, x_ref, wf_ref, ng_ref, nb_ref, ws_ref, bs_ref, *refs):
    side_in, (y_ref, *side_out), (w_ref, xb_ref, vg_ref, wc_ref, bst_ref) = (
        refs[:n_side], refs[n_side:2 * n_side + 1], refs[2 * n_side + 1:])
    step = pl.program_id(0)

    @pl.when(step < n_load)
    def _():
        slab = wf_ref.shape[0]
        w_ref[pl.ds(pl.multiple_of(step * slab, slab), slab), :] = wf_ref[...].astype(BF16)

    @pl.when(step == 0)
    def _():
        row = lax.broadcasted_iota(jnp.int32, (SGU_CHUNK, SGU_CHUNK), 0)
        col = lax.broadcasted_iota(jnp.int32, (SGU_CHUNK, SGU_CHUNK), 1)
        keep = (col <= row)[None]
        wc_ref[...] = jnp.where(keep, ws_ref[...], 0.0).astype(BF16)
        bst_ref[...] = bs_ref[...].T

    pl.when(step >= n_load)(partial(
        _sgu_tile, x_ref, w_ref, ng_ref, nb_ref, bst_ref, side_in, y_ref, side_out,
        xb_ref, vg_ref, wc_ref))


def _sgu_tile(x_ref, w_ref, ng_ref, nb_ref, bs_ref, side_in, y_ref, side_out,
              xb_ref, vg_ref, wc_ref):
    tm = x_ref.shape[0]
    n_pieces = WIDTH // MXU_COLS
    gw = WIDTH // SGU_GROUPS
    xb_ref[...] = x_ref[...].astype(BF16)

    def in_proj(path, c, epilogue):
        w_cols = slice(path * WIDTH + c * MXU_COLS, path * WIDTH + (c + 1) * MXU_COLS)
        parts = [epilogue(_dot(xb_ref[r * SGU_ROWS:(r + 1) * SGU_ROWS, :], w_ref[:, w_cols]))
                 for r in range(tm // SGU_ROWS)]
        return jnp.concatenate(parts, axis=0)

    def lane_fold(a):
        return a[:, 0:LANES] + a[:, LANES:2 * LANES]

    row_acc = jnp.zeros((tm, LANES), F32)
    for c in range(n_pieces):
        cols = slice(c * MXU_COLS, (c + 1) * MXU_COLS)
        vc = in_proj(1, c, jax.nn.gelu)
        vg_ref[:, cols] = vc
        row_acc = row_acc + lane_fold(vc)
    _side_cast(side_in, side_out)
    mu = jnp.sum(row_acc, axis=-1, keepdims=True) * (1.0 / WIDTH)
    sq_acc = jnp.zeros((tm, LANES), F32)
    for c in range(n_pieces):
        cols = slice(c * MXU_COLS, (c + 1) * MXU_COLS)
        dc = vg_ref[:, cols] - mu
        sq_acc = sq_acc + lane_fold(dc * dc)
    rstd = lax.rsqrt(jnp.sum(sq_acc, axis=-1, keepdims=True) * (1.0 / WIDTH) + LN_EPS)

    def uz_piece(c):
        return in_proj(0, c, jax.nn.gelu) * in_proj(2, c, _silu)

    uz_next = uz_piece(0)
    for c in range(n_pieces):
        cols = slice(c * MXU_COLS, (c + 1) * MXU_COLS)
        uz = uz_next
        if c + 1 < n_pieces:
            uz_next = uz_piece(c + 1)
        vn = ((vg_ref[:, cols] - mu) * rstd * ng_ref[:, cols] + nb_ref[:, cols]).astype(BF16)
        for k in range(tm // SGU_CHUNK):
            rows = slice(k * SGU_CHUNK, (k + 1) * SGU_CHUNK)
            for gi in range(MXU_COLS // gw):
                g = c * (MXU_COLS // gw) + gi
                gcols = slice(gi * gw, (gi + 1) * gw)
                s = _dot(wc_ref[g], vn[rows, gcols]) + bs_ref[:, g:g + 1]
                y_ref[rows, c * MXU_COLS + gi * gw:c * MXU_COLS + (gi + 1) * gw] = (
                    uz[rows, gcols] * s).astype(BF16)


def _sgu(layer, x, w, ng, nb, ws, bs, tm, side_sources):
    n = x.shape[0]
    n_load = D_MODEL // SGU_W_SLAB
    tile = lambda s: jnp.maximum(s - n_load, 0)
    row = lambda s: (tile(s), 0)
    side_in_specs, side_out_specs, side_out_shapes = _side_cast_specs(
        side_sources, n // tm, tile)
    return pl.pallas_call(
        partial(_sgu_kernel, len(side_sources), n_load),
        out_shape=[jax.ShapeDtypeStruct((n, WIDTH), BF16)] + side_out_shapes,
        grid=(n_load + n // tm,),
        in_specs=[
            pl.BlockSpec((tm, D_MODEL), row),
            pl.BlockSpec((None, SGU_W_SLAB, 3 * WIDTH),
                         lambda s: (layer, jnp.minimum(s, n_load - 1), 0)),
            pl.BlockSpec((1, WIDTH), lambda s: (layer, 0)),
            pl.BlockSpec((1, WIDTH), lambda s: (layer, 0)),
            pl.BlockSpec((None, SGU_GROUPS, SGU_CHUNK, SGU_CHUNK), lambda s: (layer, 0, 0, 0)),
            pl.BlockSpec((None, SGU_GROUPS, SGU_CHUNK), lambda s: (layer, 0, 0)),
        ] + side_in_specs,
        out_specs=[pl.BlockSpec((tm, WIDTH), row)] + side_out_specs,
        scratch_shapes=[pltpu.VMEM((D_MODEL, 3 * WIDTH), BF16),
                        pltpu.VMEM((tm, D_MODEL), BF16),
                        pltpu.VMEM((tm, WIDTH), F32),
                        pltpu.VMEM((SGU_GROUPS, SGU_CHUNK, SGU_CHUNK), BF16),
                        pltpu.VMEM((SGU_CHUNK, SGU_GROUPS), F32)],
        compiler_params=pltpu.CompilerParams(
            dimension_semantics=("arbitrary",), vmem_limit_bytes=SGU_VMEM_LIMIT),
        name="sgu",
    )(x, w, ng, nb, ws, bs, *[arr for arr, _ in side_sources])


def _tail_kernel(layer, y_ref, x_ref, p_ref, wo_ref, g_ref, b_ref, wg_ref, wp_ref, o_ref,
                 maybe_ob_ref=None):
    tm, d = o_ref.shape
    ln_g, ln_b = g_ref[layer:layer + 1, :], b_ref[layer:layer + 1, :]
    n_pieces = tm // TAIL_ROWS

    def pre_norm(r):
        rows = slice(r * TAIL_ROWS, (r + 1) * TAIL_ROWS)
        for c in range(d // MXU_COLS):
            cols = slice(c * MXU_COLS, (c + 1) * MXU_COLS)
            o_ref[rows, cols] = ALPHA * x_ref[rows, cols] + _dot(y_ref[rows, :], wo_ref[:, cols])

    pre_norm(0)
    for r in range(n_pieces):
        rows = slice(r * TAIL_ROWS, (r + 1) * TAIL_ROWS)
        if r + 1 < n_pieces:
            pre_norm(r + 1)
        xn = _layer_norm(o_ref[rows, :], ln_g, ln_b)
        xnb = xn.astype(BF16)
        pb = p_ref[rows, :].astype(BF16)
        for c in range(d // MXU_COLS):
            cols = slice(c * MXU_COLS, (c + 1) * MXU_COLS)
            gate = jax.nn.sigmoid(_dot(xnb, wg_ref[:, cols]))
            proj = _dot(pb, wp_ref[:, cols])
            out = xn[:, cols] + gate * proj
            o_ref[rows, cols] = out
            if maybe_ob_ref is not None:
                maybe_ob_ref[rows, cols] = out.astype(BF16)


def _tail(layer, y, x, p, wo, g, b, wg, wp, tm, with_bf16_copy):
    n = x.shape[0]
    row = lambda i: (i, 0)
    fix = lambda i: (layer, 0, 0)
    fix2 = lambda i: (0, 0)
    once = pl.Buffered(1)
    out_shape = [jax.ShapeDtypeStruct((n, D_MODEL), F32)]
    out_specs = [pl.BlockSpec((tm, D_MODEL), row)]
    if with_bf16_copy:
        out_shape.append(jax.ShapeDtypeStruct((n, D_MODEL), BF16))
        out_specs.append(pl.BlockSpec((tm, D_MODEL), row))
    return pl.pallas_call(
        partial(_tail_kernel, layer),
        out_shape=out_shape,
        grid=(n // tm,),
        in_specs=[
            pl.BlockSpec((tm, WIDTH), row),
            pl.BlockSpec((tm, D_MODEL), row),
            pl.BlockSpec((None, tm, PLE_DIM), lambda i: (layer, i, 0)),
            pl.BlockSpec((WIDTH, D_MODEL), fix2, pipeline_mode=once),
            pl.BlockSpec((DEPTH, D_MODEL), fix2),
            pl.BlockSpec((DEPTH, D_MODEL), fix2),
            pl.BlockSpec((D_MODEL, D_MODEL), fix2, pipeline_mode=once),
            pl.BlockSpec((None, PLE_DIM, D_MODEL), fix, pipeline_mode=once),
        ],
        out_specs=out_specs,
        compiler_params=pltpu.CompilerParams(
            dimension_semantics=("arbitrary",), vmem_limit_bytes=VMEM_LIMIT),
        name="layer_tail",
    )(y, x, p, wo, g, b, wg, wp)


def _moba_in_kernel(n_side, x_ref, wq_ref, wk_ref, wv_ref, wz_ref, *refs):
    side_in, (q_ref, k_ref, v_ref, zs_ref, *side_out), (wb_ref,) = (
        refs[:n_side], refs[n_side:2 * n_side + 4], refs[2 * n_side + 4:])

    @pl.when(pl.program_id(1) == 0)
    def _():
        for k, w_ref in enumerate((wq_ref, wk_ref, wv_ref, wz_ref)):
            wb_ref[k] = w_ref[...].astype(BF16)

    tm = x_ref.shape[0]
    for r in range(tm // EPILOGUE_ROWS):
        rows = slice(r * EPILOGUE_ROWS, (r + 1) * EPILOGUE_ROWS)
        xb = x_ref[rows, :]
        zs_ref[rows, :] = _silu(_dot(xb, wb_ref[3])).astype(BF16)
        q_ref[rows, :] = (_dot(xb, wb_ref[0]) * (HEAD_DIM ** -0.5 * LOG2_E)).astype(BF16)
        k_ref[rows, :] = _dot(xb, wb_ref[1]).astype(BF16)
        v_ref[rows, :] = _dot(xb, wb_ref[2]).astype(BF16)
        if r == 0:
            _side_cast(side_in, side_out)


def _moba_in(xb, w, tm, side_sources):
    n = xb.shape[0]
    tn = MXU_COLS
    nj, ni = WIDTH // tn, n // tm
    out = jax.ShapeDtypeStruct((n, WIDTH), BF16)
    side_in_specs, side_out_specs, side_out_shapes = _side_cast_specs(
        side_sources, nj * ni, lambda j, i: j * ni + i)

    def wspec(k):
        return pl.BlockSpec((D_MODEL, tn), lambda j, i: (0, j + k * nj))

    ospec = pl.BlockSpec((tm, tn), lambda j, i: (i, j))
    return pl.pallas_call(
        partial(_moba_in_kernel, len(side_sources)),
        out_shape=[out, out, out, out] + side_out_shapes,
        grid=(nj, ni),
        in_specs=[pl.BlockSpec((tm, D_MODEL), lambda j, i: (i, 0)),
                  wspec(0), wspec(1), wspec(2), wspec(3)] + side_in_specs,
        out_specs=[ospec, ospec, ospec, ospec] + side_out_specs,
        scratch_shapes=[pltpu.VMEM((4, D_MODEL, tn), BF16)],
        compiler_params=pltpu.CompilerParams(
            dimension_semantics=("arbitrary", "arbitrary"),
            vmem_limit_bytes=VMEM_LIMIT),
        name="moba_in",
    )(xb, w, w, w, w, *[arr for arr, _ in side_sources])


def _moba_attn_kernel(q_ref, k_ref, v_ref, zs_ref, y_ref, vt_ref):
    seq = k_ref.shape[0]
    n_heads = k_ref.shape[1] // HEAD_DIM
    nb = seq // MOBA_BLOCK
    blk = MOBA_BLOCK
    neg_inf = jnp.float32(-jnp.inf)

    key_i = lax.broadcasted_iota(jnp.int32, (blk, blk), 0)
    qry_i = lax.broadcasted_iota(jnp.int32, (blk, blk), 1)
    causal_t = key_i <= qry_i
    blk_i = lax.broadcasted_iota(jnp.int32, (nb, blk), 0)

    def hcols(h):
        return slice(h * HEAD_DIM, (h + 1) * HEAD_DIM)

    def qblock(own, h):
        return q_ref[own * blk:(own + 1) * blk, hcols(h)]

    def scores(own, h):
        q = qblock(own, h)
        return [_dot_nt(k_ref[n * blk:(n + 1) * blk, hcols(h)], q) for n in range(own + 1)]

    def stats(own, h, s_blocks):
        bias = None
        if own > MOBA_TOP_K:
            gate = _dot_nt(k_means[h], qblock(own, h).astype(F32))
            gate = jnp.where(blk_i < own, gate, neg_inf)
            cnt = jnp.zeros((nb, blk), jnp.int32)
            for m in range(own):
                gm = gate[m:m + 1, :]
                beats = (gm > gate) | ((gm == gate) & (m < blk_i))
                cnt = cnt + jnp.where(beats, 1, 0)
            bias = jnp.where((cnt < MOBA_TOP_K) & (blk_i < own), 0.0, neg_inf)

        s_blocks = s_blocks[:own] + [jnp.where(causal_t, s_blocks[own], neg_inf)]
        m_q = jnp.max(s_blocks[own], axis=0, keepdims=True)
        for n in range(own):
            bm = jnp.max(s_blocks[n], axis=0, keepdims=True)
            if bias is not None:
                bm = bm + bias[n:n + 1, :]
            m_q = jnp.maximum(m_q, bm)
        return s_blocks, m_q, bias

    def exp_pv(own, h, s_blocks, m_q, bias):
        acc = jnp.zeros((VT_ROWS, blk), F32)
        for n in range(own + 1):
            shift = m_q
            if bias is not None and n < own:
                shift = m_q - bias[n:n + 1, :]
            pn = jnp.exp2(s_blocks[n] - shift)
            acc = acc + _dot(vt_ref[h, :, n * blk:(n + 1) * blk], pn.astype(BF16))
        o = (acc[0:HEAD_DIM, :] / acc[HEAD_DIM:HEAD_DIM + 1, :]).T
        rows = slice(own * blk, (own + 1) * blk)
        y_ref[rows, hcols(h)] = (o * zs_ref[rows, hcols(h)].astype(F32)).astype(BF16)

    units = [(own, h) for own in reversed(range(nb)) for h in range(n_heads)]
    pending = [scores(*units[0]), scores(*units[1])]

    k_means = []
    for h in range(n_heads):
        for n in range(nb):
            vt_ref[h, 0:HEAD_DIM, n * blk:(n + 1) * blk] = (
                v_ref[n * blk:(n + 1) * blk, hcols(h)].astype(F32).T.astype(BF16))
        vt_ref[h, HEAD_DIM:VT_ROWS, :] = jnp.ones((ONES_ROWS, seq), BF16)
        kf = k_ref[:, hcols(h)].astype(F32)
        k_means.append(jnp.mean(kf.reshape(nb, blk, HEAD_DIM), axis=1))

    st = stats(*units[0], pending.pop(0))
    for i, unit in enumerate(units):
        if i + 2 < len(units):
            pending.append(scores(*units[i + 2]))
        st_next = stats(*units[i + 1], pending.pop(0)) if i + 1 < len(units) else None
        exp_pv(*unit, *st)
        st = st_next


def _moba_attn(q, k, v, zs, batch, seq, heads_per_step):
    spec = pl.BlockSpec((seq, heads_per_step * HEAD_DIM), lambda b, g: (b, g))
    return pl.pallas_call(
        _moba_attn_kernel,
        out_shape=jax.ShapeDtypeStruct((batch * seq, WIDTH), BF16),
        grid=(batch, HEADS // heads_per_step),
        in_specs=[spec, spec, spec, spec],
        out_specs=spec,
        scratch_shapes=[pltpu.VMEM((heads_per_step, VT_ROWS, seq), BF16)],
        compiler_params=pltpu.CompilerParams(
            dimension_semantics=("arbitrary", "arbitrary"),
            vmem_limit_bytes=VMEM_LIMIT),
        name="moba_attn",
    )(q, k, v, zs)


def kernel(x, p, w_in_a, sgu_norm_g, sgu_norm_b, w_s, b_s, w_in_b, w_out, ln_g, ln_b,
           w_ple_gate, w_ple_proj):
    batch, seq, d = x.shape
    n = batch * seq
    x2 = x.reshape(n, d)
    p2 = p.reshape(DEPTH, n, PLE_DIM)

    y, wo0, wg0, wp_flat = _sgu(
        0, x2, w_in_a, sgu_norm_g, sgu_norm_b, w_s, b_s, tm=SGU_TM,
        side_sources=[(w_out, 0), (w_ple_gate, 0),
                      (w_ple_proj.reshape(DEPTH * PLE_DIM, d), None)])
    wp = wp_flat.reshape(DEPTH, PLE_DIM, d)
    x2, x2b = _tail(0, y, x2, p2, wo0, ln_g, ln_b, wg0, wp, tm=TAIL_TM, with_bf16_copy=True)

    q, k, v, zs, wo1, wg1 = _moba_in(
        x2b, w_in_b[0], tm=MOBA_IN_TM, side_sources=[(w_out, 1), (w_ple_gate, 1)])
    y = _moba_attn(q, k, v, zs, batch, seq, heads_per_step=ATTN_HEADS_PER_STEP)
    (x2,) = _tail(1, y, x2, p2, wo1, ln_g, ln_b, wg1, wp, tm=TAIL_TM, with_bf16_copy=False)

    return x2.reshape(batch, seq, d)
```

```python
from functools import partial

import jax
import jax.numpy as jnp
from jax import lax
from jax.experimental import pallas as pl
from jax.experimental.pallas import tpu as pltpu

D_MODEL = 2048
WIDTH = 2048
SGU_CHUNK = 128
SGU_GROUPS = 16
HEADS = 16
HEAD_DIM = 128
MOBA_BLOCK = 256
MOBA_TOP_K = 3
PLE_DIM = 256
LN_EPS = 1e-5
DEPTH = 2
ALPHA = (2 * DEPTH) ** 0.25
LOG2_E = 1.4426950408889634
MXU_COLS = 256
LANES = 128
EPILOGUE_ROWS = 512
TAIL_ROWS = 256
SGU_ROWS = 512
BF16_ROWS = 16
ONES_ROWS = BF16_ROWS
VT_ROWS = HEAD_DIM + ONES_ROWS

BF16 = jnp.bfloat16
F32 = jnp.float32

VMEM_LIMIT = 56 * 1024 * 1024
SGU_VMEM_LIMIT = 59 * 1024 * 1024
SGU_TM = 512
SGU_W_SLAB = 128
TAIL_TM = 512
MOBA_IN_TM = 2048
ATTN_HEADS_PER_STEP = 4


def _dot(a, b):
    return jnp.dot(a, b, preferred_element_type=F32)


def _dot_nt(a, b):
    return lax.dot_general(a, b, (((1,), (1,)), ((), ())), preferred_element_type=F32)


def _silu(z):
    return z * jax.nn.sigmoid(z)


def _layer_norm(x, g, b):
    mu = jnp.mean(x, axis=-1, keepdims=True)
    xc = x - mu
    var = jnp.mean(xc * xc, axis=-1, keepdims=True)
    return xc * lax.rsqrt(var + LN_EPS) * g + b


def _side_cast_specs(sources, n_steps, step_index):
    in_specs, out_specs, out_shapes = [], [], []
    for arr, layer in sources:
        rows, cols = arr.shape[-2:]
        slab = rows // n_steps
        assert slab * n_steps == rows and slab % BF16_ROWS == 0, (rows, n_steps)
        if layer is None:
            in_specs.append(pl.BlockSpec((slab, cols), lambda *g: (step_index(*g), 0)))
        else:
            in_specs.append(pl.BlockSpec(
                (None, slab, cols), lambda *g, layer=layer: (layer, step_index(*g), 0)))
        out_specs.append(pl.BlockSpec((slab, cols), lambda *g: (step_index(*g), 0)))
        out_shapes.append(jax.ShapeDtypeStruct((rows, cols), BF16))
    return in_specs, out_specs, out_shapes


def _side_cast(side_in_refs, side_out_refs):
    for src_ref, dst_ref in zip(side_in_refs, side_out_refs):
        dst_ref[...] = src_ref[...].astype(BF16)


def _sgu_kernel(n_side, n_load, x_ref, wf_ref, ng_ref, nb_ref, ws_ref, bs_ref, *refs):
    side_in, (y_ref, *side_out), (w_ref, xb_ref, vg_ref, wc_ref, bst_ref) = (
        refs[:n_side], refs[n_side:2 * n_side + 1], refs[2 * n_side + 1:])
    step = pl.program_id(0)

    @pl.when(step < n_load)
    def _():
        slab = wf_ref.shape[0]
        w_ref[pl.ds(pl.multiple_of(step * slab, slab), slab), :] = wf_ref[...].astype(BF16)

    @pl.when(step == 0)
    def _():
        row = lax.broadcasted_iota(jnp.int32, (SGU_CHUNK, SGU_CHUNK), 0)
        col = lax.broadcasted_iota(jnp.int32, (SGU_CHUNK, SGU_CHUNK), 1)
        keep = (col <= row)[None]
        wc_ref[...] = jnp.where(keep, ws_ref[...], 0.0).astype(BF16)
        bst_ref[...] = bs_ref[...].T

    pl.when(step >= n_load)(partial(
        _sgu_tile, x_ref, w_ref, ng_ref, nb_ref, bst_ref, side_in, y_ref, side_out,
        xb_ref, vg_ref, wc_ref))


def _sgu_tile(x_ref, w_ref, ng_ref, nb_ref, bs_ref, side_in, y_ref, side_out,
              xb_ref, vg_ref, wc_ref):
    tm = x_ref.shape[0]
    n_pieces = WIDTH // MXU_COLS
    gw = WIDTH // SGU_GROUPS
    xb_ref[...] = x_ref[...].astype(BF16)

    def in_proj(path, c, epilogue):
        w_cols = slice(path * WIDTH + c * MXU_COLS, path * WIDTH + (c + 1) * MXU_COLS)
        parts = [epilogue(_dot(xb_ref[r * SGU_ROWS:(r + 1) * SGU_ROWS, :], w_ref[:, w_cols]))
                 for r in range(tm // SGU_ROWS)]
        return jnp.concatenate(parts, axis=0)

    def lane_fold(a):
        return a[:, 0:LANES] + a[:, LANES:2 * LANES]

    pilot = None
    d_acc = jnp.zeros((tm, LANES), F32)
    sq_acc = jnp.zeros((tm, LANES), F32)
    for c in range(n_pieces):
        cols = slice(c * MXU_COLS, (c + 1) * MXU_COLS)
        vc = in_proj(1, c, jax.nn.gelu)
        vg_ref[:, cols] = vc
        if c == 0:
            pilot = jnp.mean(vc, axis=-1, keepdims=True)
        dc = vc - pilot
        d_acc = d_acc + lane_fold(dc)
        sq_acc = sq_acc + lane_fold(dc * dc)
    _side_cast(side_in, side_out)
    d_mean = jnp.sum(d_acc, axis=-1, keepdims=True) * (1.0 / WIDTH)
    mu = pilot + d_mean
    var = jnp.sum(sq_acc, axis=-1, keepdims=True) * (1.0 / WIDTH) - d_mean * d_mean
    rstd = lax.rsqrt(var + LN_EPS)

    def uz_piece(c):
        return in_proj(0, c, jax.nn.gelu) * in_proj(2, c, _silu)

    uz_next = uz_piece(0)
    for c in range(n_pieces):
        cols = slice(c * MXU_COLS, (c + 1) * MXU_COLS)
        uz = uz_next
        if c + 1 < n_pieces:
            uz_next = uz_piece(c + 1)
        vn = ((vg_ref[:, cols] - mu) * rstd * ng_ref[:, cols] + nb_ref[:, cols]).astype(BF16)
        for k in range(tm // SGU_CHUNK):
            rows = slice(k * SGU_CHUNK, (k + 1) * SGU_CHUNK)
            for gi in range(MXU_COLS // gw):
                g = c * (MXU_COLS // gw) + gi
                gcols = slice(gi * gw, (gi + 1) * gw)
                s = _dot(wc_ref[g], vn[rows, gcols]) + bs_ref[:, g:g + 1]
                y_ref[rows, c * MXU_COLS + gi * gw:c * MXU_COLS + (gi + 1) * gw] = (
                    uz[rows, gcols] * s).astype(BF16)


def _sgu(layer, x, w, ng, nb, ws, bs, tm, side_sources):
    n = x.shape[0]
    n_load = D_MODEL // SGU_W_SLAB
    tile = lambda s: jnp.maximum(s - n_load, 0)
    row = lambda s: (tile(s), 0)
    side_in_specs, side_out_specs, side_out_shapes = _side_cast_specs(
        side_sources, n // tm, tile)
    return pl.pallas_call(
        partial(_sgu_kernel, len(side_sources), n_load),
        out_shape=[jax.ShapeDtypeStruct((n, WIDTH), BF16)] + side_out_shapes,
        grid=(n_load + n // tm,),
        in_specs=[
            pl.BlockSpec((tm, D_MODEL), row),
            pl.BlockSpec((None, SGU_W_SLAB, 3 * WIDTH),
                         lambda s: (layer, jnp.minimum(s, n_load - 1), 0)),
            pl.BlockSpec((1, WIDTH), lambda s: (layer, 0)),
            pl.BlockSpec((1, WIDTH), lambda s: (layer, 0)),
            pl.BlockSpec((None, SGU_GROUPS, SGU_CHUNK, SGU_CHUNK), lambda s: (layer, 0, 0, 0)),
            pl.BlockSpec((None, SGU_GROUPS, SGU_CHUNK), lambda s: (layer, 0, 0)),
        ] + side_in_specs,
        out_specs=[pl.BlockSpec((tm, WIDTH), row)] + side_out_specs,
        scratch_shapes=[pltpu.VMEM((D_MODEL, 3 * WIDTH), BF16),
                        pltpu.VMEM((tm, D_MODEL), BF16),
                        pltpu.VMEM((tm, WIDTH), F32),
                        pltpu.VMEM((SGU_GROUPS, SGU_CHUNK, SGU_CHUNK), BF16),
                        pltpu.VMEM((SGU_CHUNK, SGU_GROUPS), F32)],
        compiler_params=pltpu.CompilerParams(
            dimension_semantics=("arbitrary",), vmem_limit_bytes=SGU_VMEM_LIMIT),
        name="sgu",
    )(x, w, ng, nb, ws, bs, *[arr for arr, _ in side_sources])


def _tail_kernel(layer, y_ref, x_ref, p_ref, wo_ref, g_ref, b_ref, wg_ref, wp_ref, o_ref,
                 maybe_ob_ref=None):
    tm, d = o_ref.shape
    ln_g, ln_b = g_ref[layer:layer + 1, :], b_ref[layer:layer + 1, :]
    n_pieces = tm // TAIL_ROWS

    def pre_norm(r):
        rows = slice(r * TAIL_ROWS, (r + 1) * TAIL_ROWS)
        for c in range(d // MXU_COLS):
            cols = slice(c * MXU_COLS, (c + 1) * MXU_COLS)
            o_ref[rows, cols] = ALPHA * x_ref[rows, cols] + _dot(y_ref[rows, :], wo_ref[:, cols])

    pre_norm(0)
    for r in range(n_pieces):
        rows = slice(r * TAIL_ROWS, (r + 1) * TAIL_ROWS)
        if r + 1 < n_pieces:
            pre_norm(r + 1)
        xn = _layer_norm(o_ref[rows, :], ln_g, ln_b)
        xnb = xn.astype(BF16)
        pb = p_ref[rows, :].astype(BF16)
        for c in range(d // MXU_COLS):
            cols = slice(c * MXU_COLS, (c + 1) * MXU_COLS)
            gate = jax.nn.sigmoid(_dot(xnb, wg_ref[:, cols]))
            proj = _dot(pb, wp_ref[:, cols])
            out = xn[:, cols] + gate * proj
            o_ref[rows, cols] = out
            if maybe_ob_ref is not None:
                maybe_ob_ref[rows, cols] = out.astype(BF16)


def _tail(layer, y, x, p, wo, g, b, wg, wp, tm, with_bf16_copy):
    n = x.shape[0]
    row = lambda i: (i, 0)
    fix = lambda i: (layer, 0, 0)
    fix2 = lambda i: (0, 0)
    once = pl.Buffered(1)
    out_shape = [jax.ShapeDtypeStruct((n, D_MODEL), F32)]
    out_specs = [pl.BlockSpec((tm, D_MODEL), row)]
    if with_bf16_copy:
        out_shape.append(jax.ShapeDtypeStruct((n, D_MODEL), BF16))
        out_specs.append(pl.BlockSpec((tm, D_MODEL), row))
    return pl.pallas_call(
        partial(_tail_kernel, layer),
        out_shape=out_shape,
        grid=(n // tm,),
        in_specs=[
            pl.BlockSpec((tm, WIDTH), row),
            pl.BlockSpec((tm, D_MODEL), row),
            pl.BlockSpec((None, tm, PLE_DIM), lambda i: (layer, i, 0)),
            pl.BlockSpec((WIDTH, D_MODEL), fix2, pipeline_mode=once),
            pl.BlockSpec((DEPTH, D_MODEL), fix2),
            pl.BlockSpec((DEPTH, D_MODEL), fix2),
            pl.BlockSpec((D_MODEL, D_MODEL), fix2, pipeline_mode=once),
            pl.BlockSpec((None, PLE_DIM, D_MODEL), fix, pipeline_mode=once),
        ],
        out_specs=out_specs,
        compiler_params=pltpu.CompilerParams(
            dimension_semantics=("arbitrary",), vmem_limit_bytes=VMEM_LIMIT),
        name="layer_tail",
    )(y, x, p, wo, g, b, wg, wp)


def _moba_in_kernel(n_side, x_ref, wq_ref, wk_ref, wv_ref, wz_ref, *refs):
    side_in, (q_ref, k_ref, v_ref, zs_ref, *side_out), (wb_ref,) = (
        refs[:n_side], refs[n_side:2 * n_side + 4], refs[2 * n_side + 4:])

    @pl.when(pl.program_id(1) == 0)
    def _():
        for k, w_ref in enumerate((wq_ref, wk_ref, wv_ref, wz_ref)):
            wb_ref[k] = w_ref[...].astype(BF16)

    tm = x_ref.shape[0]
    for r in range(tm // EPILOGUE_ROWS):
        rows = slice(r * EPILOGUE_ROWS, (r + 1) * EPILOGUE_ROWS)
        xb = x_ref[rows, :]
        zs_ref[rows, :] = _silu(_dot(xb, wb_ref[3])).astype(BF16)
        q_ref[rows, :] = (_dot(xb, wb_ref[0]) * (HEAD_DIM ** -0.5 * LOG2_E)).astype(BF16)
        k_ref[rows, :] = _dot(xb, wb_ref[1]).astype(BF16)
        v_ref[rows, :] = _dot(xb, wb_ref[2]).astype(BF16)
        if r == 0:
            _side_cast(side_in, side_out)


def _moba_in(xb, w, tm, side_sources):
    n = xb.shape[0]
    tn = MXU_COLS
    nj, ni = WIDTH // tn, n // tm
    out = jax.ShapeDtypeStruct((n, WIDTH), BF16)
    side_in_specs, side_out_specs, side_out_shapes = _side_cast_specs(
        side_sources, nj * ni, lambda j, i: j * ni + i)

    def wspec(k):
        return pl.BlockSpec((D_MODEL, tn), lambda j, i: (0, j + k * nj))

    ospec = pl.BlockSpec((tm, tn), lambda j, i: (i, j))
    return pl.pallas_call(
        partial(_moba_in_kernel, len(side_sources)),
        out_shape=[out, out, out, out] + side_out_shapes,
        grid=(nj, ni),
        in_specs=[pl.BlockSpec((tm, D_MODEL), lambda j, i: (i, 0)),
                  wspec(0), wspec(1), wspec(2), wspec(3)] + side_in_specs,
        out_specs=[ospec, ospec, ospec, ospec] + side_out_specs,
        scratch_shapes=[pltpu.VMEM((4, D_MODEL, tn), BF16)],
        compiler_params=pltpu.CompilerParams(
            dimension_semantics=("arbitrary", "arbitrary"),
            vmem_limit_bytes=VMEM_LIMIT),
        name="moba_in",
    )(xb, w, w, w, w, *[arr for arr, _ in side_sources])


def _moba_attn_kernel(q_ref, k_ref, v_ref, zs_ref, y_ref, vt_ref):
    seq = k_ref.shape[0]
    n_heads = k_ref.shape[1] // HEAD_DIM
    nb = seq // MOBA_BLOCK
    blk = MOBA_BLOCK
    neg_inf = jnp.float32(-jnp.inf)

    key_i = lax.broadcasted_iota(jnp.int32, (blk, blk), 0)
    qry_i = lax.broadcasted_iota(jnp.int32, (blk, blk), 1)
    causal_t = key_i <= qry_i
    blk_i = lax.broadcasted_iota(jnp.int32, (nb, blk), 0)

    def hcols(h):
        return slice(h * HEAD_DIM, (h + 1) * HEAD_DIM)

    def qblock(own, h):
        return q_ref[own * blk:(own + 1) * blk, hcols(h)]

    def scores(own, h):
        q = qblock(own, h)
        return [_dot_nt(k_ref[n * blk:(n + 1) * blk, hcols(h)], q) for n in range(own + 1)]

    def stats(own, h, s_blocks):
        bias = None
        if own > MOBA_TOP_K:
            gate = _dot_nt(k_means[h], qblock(own, h).astype(F32))
            gate = jnp.where(blk_i < own, gate, neg_inf)
            cnt = jnp.zeros((nb, blk), jnp.int32)
            for m in range(own):
                gm = gate[m:m + 1, :]
                beats = (gm > gate) | ((gm == gate) & (m < blk_i))
                cnt = cnt + jnp.where(beats, 1, 0)
            bias = jnp.where((cnt < MOBA_TOP_K) & (blk_i < own), 0.0, neg_inf)

        s_blocks = s_blocks[:own] + [jnp.where(causal_t, s_blocks[own], neg_inf)]
        m_q = jnp.max(s_blocks[own], axis=0, keepdims=True)
        for n in range(own):
            bm = jnp.max(s_blocks[n], axis=0, keepdims=True)
            if bias is not None:
                bm = bm + bias[n:n + 1, :]
            m_q = jnp.maximum(m_q, bm)
        return s_blocks, m_q, bias

    def exp_pv(own, h, s_blocks, m_q, bias):
        acc = jnp.zeros((VT_ROWS, blk), F32)
        for n in range(own + 1):
            shift = m_q
            if bias is not None and n < own:
                shift = m_q - bias[n:n + 1, :]
            pn = jnp.exp2(s_blocks[n] - shift)
            acc = acc + _dot(vt_ref[h, :, n * blk:(n + 1) * blk], pn.astype(BF16))
        o = (acc[0:HEAD_DIM, :] / acc[HEAD_DIM:HEAD_DIM + 1, :]).T
        rows = slice(own * blk, (own + 1) * blk)
        y_ref[rows, hcols(h)] = (o * zs_ref[rows, hcols(h)].astype(F32)).astype(BF16)

    units = [(own, h) for own in reversed(range(nb)) for h in range(n_heads)]
    pending = [scores(*units[0]), scores(*units[1])]

    k_means = []
    for h in range(n_heads):
        for n in range(nb):
            vt_ref[h, 0:HEAD_DIM, n * blk:(n + 1) * blk] = (
                v_ref[n * blk:(n + 1) * blk, hcols(h)].astype(F32).T.astype(BF16))
        vt_ref[h, HEAD_DIM:VT_ROWS, :] = jnp.ones((ONES_ROWS, seq), BF16)
        kf = k_ref[:, hcols(h)].astype(F32)
        k_means.append(jnp.mean(kf.reshape(nb, blk, HEAD_DIM), axis=1))

    st = stats(*units[0], pending.pop(0))
    for i, unit in enumerate(units):
        if i + 2 < len(units):
            pending.append(scores(*units[i + 2]))
        st_next = stats(*units[i + 1], pending.pop(0)) if i + 1 < len(units) else None
        exp_pv(*unit, *st)
        st = st_next


def _moba_attn(q, k, v, zs, batch, seq, heads_per_step):
    spec = pl.BlockSpec((seq, heads_per_step * HEAD_DIM), lambda b, g: (b, g))
    return pl.pallas_call(
        _moba_attn_kernel,
        out_shape=jax.ShapeDtypeStruct((batch * seq, WIDTH), BF16),
        grid=(batch, HEADS // heads_per_step),
        in_specs=[spec, spec, spec, spec],
        out_specs=spec,
        scratch_shapes=[pltpu.VMEM((heads_per_step, VT_ROWS, seq), BF16)],
        compiler_params=pltpu.CompilerParams(
            dimension_semantics=("arbitrary", "arbitrary"),
            vmem_limit_bytes=VMEM_LIMIT),
        name="moba_attn",
    )(q, k, v, zs)


def kernel(x, p, w_in_a, sgu_norm_g, sgu_norm_b, w_s, b_s, w_in_b, w_out, ln_g, ln_b,
           w_ple_gate, w_ple_proj):
    batch, seq, d = x.shape
    n = batch * seq
    x2 = x.reshape(n, d)
    p2 = p.reshape(DEPTH, n, PLE_DIM)

    y, wo0, wg0, wp_flat = _sgu(
        0, x2, w_in_a, sgu_norm_g, sgu_norm_b, w_s, b_s, tm=SGU_TM,
        side_sources=[(w_out, 0), (w_ple_gate, 0),
                      (w_ple_proj.reshape(DEPTH * PLE_DIM, d), None)])
    wp = wp_flat.reshape(DEPTH, PLE_DIM, d)
    x2, x2b = _tail(0, y, x2, p2, wo0, ln_g, ln_b, wg0, wp, tm=TAIL_TM, with_bf16_copy=True)

    q, k, v, zs, wo1, wg1 = _moba_in(
        x2b, w_in_b[0], tm=MOBA_IN_TM, side_sources=[(w_out, 1), (w_ple_gate, 1)])
    y = _moba_attn(q, k, v, zs, batch, seq, heads_per_step=ATTN_HEADS_PER_STEP)
    (x2,) = _tail(1, y, x2, p2, wo1, ln_g, ln_b, wg1, wp, tm=TAIL_TM, with_bf16_copy=False)

    return x2.reshape(batch, seq, d)
```

```python
from functools import partial

import jax
import jax.numpy as jnp
from jax import lax
from jax.experimental import pallas as pl
from jax.experimental.pallas import tpu as pltpu

D_MODEL = 2048
WIDTH = 2048
SGU_CHUNK = 128
SGU_GROUPS = 16
HEADS = 16
HEAD_DIM = 128
MOBA_BLOCK = 256
MOBA_TOP_K = 3
PLE_DIM = 256
LN_EPS = 1e-5
DEPTH = 2
ALPHA = (2 * DEPTH) ** 0.25
LOG2_E = 1.4426950408889634
MXU_COLS = 256
LANES = 128
EPILOGUE_ROWS = 512
TAIL_ROWS = 256
SGU_ROWS = 512
BF16_ROWS = 16
ONES_ROWS = BF16_ROWS
VT_ROWS = HEAD_DIM + ONES_ROWS

BF16 = jnp.bfloat16
F32 = jnp.float32

VMEM_LIMIT = 56 * 1024 * 1024
SGU_VMEM_LIMIT = 59 * 1024 * 1024
SGU_TM = 512
SGU_W_SLAB = 128
TAIL_TM = 512
MOBA_IN_TM = 2048
ATTN_HEADS_PER_STEP = 4


def _dot(a, b):
    return jnp.dot(a, b, preferred_element_type=F32)


def _dot_nt(a, b):
    return lax.dot_general(a, b, (((1,), (1,)), ((), ())), preferred_element_type=F32)


def _silu(z):
    return z * jax.nn.sigmoid(z)


class _RowStats:
    def __init__(self):
        self.pilot = self.d_acc = self.sq_acc = None

    def add(self, piece):
        def lane_fold(a):
            return a[:, 0:LANES] + a[:, LANES:2 * LANES]

        if self.pilot is None:
            self.pilot = jnp.mean(piece, axis=-1, keepdims=True)
        dc = piece - self.pilot
        d_part, sq_part = lane_fold(dc), lane_fold(dc * dc)
        self.d_acc = d_part if self.d_acc is None else self.d_acc + d_part
        self.sq_acc = sq_part if self.sq_acc is None else self.sq_acc + sq_part

    def mean_rstd(self, width):
        d_mean = jnp.sum(self.d_acc, axis=-1, keepdims=True) * (1.0 / width)
        var = jnp.sum(self.sq_acc, axis=-1, keepdims=True) * (1.0 / width) - d_mean * d_mean
        return self.pilot + d_mean, lax.rsqrt(var + LN_EPS)


def _side_cast_specs(sources, n_steps, step_index):
    in_specs, out_specs, out_shapes = [], [], []
    for arr, layer in sources:
        rows, cols = arr.shape[-2:]
        slab = rows // n_steps
        assert slab * n_steps == rows and slab % BF16_ROWS == 0, (rows, n_steps)
        if layer is None:
            in_specs.append(pl.BlockSpec((slab, cols), lambda *g: (step_index(*g), 0)))
        else:
            in_specs.append(pl.BlockSpec(
                (None, slab, cols), lambda *g, layer=layer: (layer, step_index(*g), 0)))
        out_specs.append(pl.BlockSpec((slab, cols), lambda *g: (step_index(*g), 0)))
        out_shapes.append(jax.ShapeDtypeStruct((rows, cols), BF16))
    return in_specs, out_specs, out_shapes


def _side_cast(side_in_refs, side_out_refs):
    for src_ref, dst_ref in zip(side_in_refs, side_out_refs):
        dst_ref[...] = src_ref[...].astype(BF16)


def _sgu_kernel(n_side, n_load, x_ref, wf_ref, ng_ref, nb_ref, ws_ref, bs_ref, *refs):
    side_in, (y_ref, *side_out), (w_ref, xb_ref, vg_ref, wc_ref, bst_ref) = (
        refs[:n_side], refs[n_side:2 * n_side + 1], refs[2 * n_side + 1:])
    step = pl.program_id(0)

    @pl.when(step < n_load)
    def _():
        slab = wf_ref.shape[0]
        w_ref[pl.ds(pl.multiple_of(step * slab, slab), slab), :] = wf_ref[...].astype(BF16)

    @pl.when(step == 0)
    def _():
        row = lax.broadcasted_iota(jnp.int32, (SGU_CHUNK, SGU_CHUNK), 0)
        col = lax.broadcasted_iota(jnp.int32, (SGU_CHUNK, SGU_CHUNK), 1)
        keep = (col <= row)[None]
        wc_ref[...] = jnp.where(keep, ws_ref[...], 0.0).astype(BF16)
        bst_ref[...] = bs_ref[...].T

    pl.when(step >= n_load)(partial(
        _sgu_tile, x_ref, w_ref, ng_ref, nb_ref, bst_ref, side_in, y_ref, side_out,
        xb_ref, vg_ref, wc_ref))


def _sgu_tile(x_ref, w_ref, ng_ref, nb_ref, bs_ref, side_in, y_ref, side_out,
              xb_ref, vg_ref, wc_ref):
    tm = x_ref.shape[0]
    n_pieces = WIDTH // MXU_COLS
    gw = WIDTH // SGU_GROUPS
    xb_ref[...] = x_ref[...].astype(BF16)

    def in_proj(path, c, epilogue):
        w_cols = slice(path * WIDTH + c * MXU_COLS, path * WIDTH + (c + 1) * MXU_COLS)
        parts = [epilogue(_dot(xb_ref[r * SGU_ROWS:(r + 1) * SGU_ROWS, :], w_ref[:, w_cols]))
                 for r in range(tm // SGU_ROWS)]
        return jnp.concatenate(parts, axis=0)

    stats = _RowStats()
    for c in range(n_pieces):
        vc = in_proj(1, c, jax.nn.gelu)
        vg_ref[:, c * MXU_COLS:(c + 1) * MXU_COLS] = vc
        stats.add(vc)
    _side_cast(side_in, side_out)
    mu, rstd = stats.mean_rstd(WIDTH)

    def uz_piece(c):
        return in_proj(0, c, jax.nn.gelu) * in_proj(2, c, _silu)

    uz_next = uz_piece(0)
    for c in range(n_pieces):
        cols = slice(c * MXU_COLS, (c + 1) * MXU_COLS)
        uz = uz_next
        if c + 1 < n_pieces:
            uz_next = uz_piece(c + 1)
        vn = ((vg_ref[:, cols] - mu) * rstd * ng_ref[:, cols] + nb_ref[:, cols]).astype(BF16)
        for k in range(tm // SGU_CHUNK):
            rows = slice(k * SGU_CHUNK, (k + 1) * SGU_CHUNK)
            for gi in range(MXU_COLS // gw):
                g = c * (MXU_COLS // gw) + gi
                gcols = slice(gi * gw, (gi + 1) * gw)
                s = _dot(wc_ref[g], vn[rows, gcols]) + bs_ref[:, g:g + 1]
                y_ref[rows, c * MXU_COLS + gi * gw:c * MXU_COLS + (gi + 1) * gw] = (
                    uz[rows, gcols] * s).astype(BF16)


def _sgu(layer, x, w, ng, nb, ws, bs, tm, side_sources):
    n = x.shape[0]
    n_load = D_MODEL // SGU_W_SLAB
    tile = lambda s: jnp.maximum(s - n_load, 0)
    row = lambda s: (tile(s), 0)
    side_in_specs, side_out_specs, side_out_shapes = _side_cast_specs(
        side_sources, n // tm, tile)
    return pl.pallas_call(
        partial(_sgu_kernel, len(side_sources), n_load),
        out_shape=[jax.ShapeDtypeStruct((n, WIDTH), BF16)] + side_out_shapes,
        grid=(n_load + n // tm,),
        in_specs=[
            pl.BlockSpec((tm, D_MODEL), row),
            pl.BlockSpec((None, SGU_W_SLAB, 3 * WIDTH),
                         lambda s: (layer, jnp.minimum(s, n_load - 1), 0)),
            pl.BlockSpec((1, WIDTH), lambda s: (layer, 0)),
            pl.BlockSpec((1, WIDTH), lambda s: (layer, 0)),
            pl.BlockSpec((None, SGU_GROUPS, SGU_CHUNK, SGU_CHUNK), lambda s: (layer, 0, 0, 0)),
            pl.BlockSpec((None, SGU_GROUPS, SGU_CHUNK), lambda s: (layer, 0, 0)),
        ] + side_in_specs,
        out_specs=[pl.BlockSpec((tm, WIDTH), row)] + side_out_specs,
        scratch_shapes=[pltpu.VMEM((D_MODEL, 3 * WIDTH), BF16),
                        pltpu.VMEM((tm, D_MODEL), BF16),
                        pltpu.VMEM((tm, WIDTH), F32),
                        pltpu.VMEM((SGU_GROUPS, SGU_CHUNK, SGU_CHUNK), BF16),
                        pltpu.VMEM((SGU_CHUNK, SGU_GROUPS), F32)],
        compiler_params=pltpu.CompilerParams(
            dimension_semantics=("arbitrary",), vmem_limit_bytes=SGU_VMEM_LIMIT),
        name="sgu",
    )(x, w, ng, nb, ws, bs, *[arr for arr, _ in side_sources])


def _tail_kernel(layer, y_ref, x_ref, p_ref, wo_ref, g_ref, b_ref, wg_ref, wp_ref, o_ref,
                 maybe_ob_ref=None):
    tm, d = o_ref.shape
    ln_g, ln_b = g_ref[layer:layer + 1, :], b_ref[layer:layer + 1, :]
    n_pieces = tm // TAIL_ROWS

    def pre_norm(r):
        rows = slice(r * TAIL_ROWS, (r + 1) * TAIL_ROWS)
        stats = _RowStats()
        for c in range(d // MXU_COLS):
            cols = slice(c * MXU_COLS, (c + 1) * MXU_COLS)
            t = ALPHA * x_ref[rows, cols] + _dot(y_ref[rows, :], wo_ref[:, cols])
            o_ref[rows, cols] = t
            stats.add(t)
        return stats.mean_rstd(d)

    mu_rstd = pre_norm(0)
    for r in range(n_pieces):
        rows = slice(r * TAIL_ROWS, (r + 1) * TAIL_ROWS)
        mu, rstd = mu_rstd
        if r + 1 < n_pieces:
            mu_rstd = pre_norm(r + 1)
        xn = (o_ref[rows, :] - mu) * rstd * ln_g + ln_b
        xnb = xn.astype(BF16)
        pb = p_ref[rows, :].astype(BF16)
        for c in range(d // MXU_COLS):
            cols = slice(c * MXU_COLS, (c + 1) * MXU_COLS)
            gate = jax.nn.sigmoid(_dot(xnb, wg_ref[:, cols]))
            proj = _dot(pb, wp_ref[:, cols])
            out = xn[:, cols] + gate * proj
            o_ref[rows, cols] = out
            if maybe_ob_ref is not None:
                maybe_ob_ref[rows, cols] = out.astype(BF16)


def _tail(layer, y, x, p, wo, g, b, wg, wp, tm, with_bf16_copy):
    n = x.shape[0]
    row = lambda i: (i, 0)
    fix = lambda i: (layer, 0, 0)
    fix2 = lambda i: (0, 0)
    once = pl.Buffered(1)
    out_shape = [jax.ShapeDtypeStruct((n, D_MODEL), F32)]
    out_specs = [pl.BlockSpec((tm, D_MODEL), row)]
    if with_bf16_copy:
        out_shape.append(jax.ShapeDtypeStruct((n, D_MODEL), BF16))
        out_specs.append(pl.BlockSpec((tm, D_MODEL), row))
    return pl.pallas_call(
        partial(_tail_kernel, layer),
        out_shape=out_shape,
        grid=(n // tm,),
        in_specs=[
            pl.BlockSpec((tm, WIDTH), row),
            pl.BlockSpec((tm, D_MODEL), row),
            pl.BlockSpec((None, tm, PLE_DIM), lambda i: (layer, i, 0)),
            pl.BlockSpec((WIDTH, D_MODEL), fix2, pipeline_mode=once),
            pl.BlockSpec((DEPTH, D_MODEL), fix2),
            pl.BlockSpec((DEPTH, D_MODEL), fix2),
            pl.BlockSpec((D_MODEL, D_MODEL), fix2, pipeline_mode=once),
            pl.BlockSpec((None, PLE_DIM, D_MODEL), fix, pipeline_mode=once),
        ],
        out_specs=out_specs,
        compiler_params=pltpu.CompilerParams(
            dimension_semantics=("arbitrary",), vmem_limit_bytes=VMEM_LIMIT),
        name="layer_tail",
    )(y, x, p, wo, g, b, wg, wp)


def _moba_in_kernel(n_side, x_ref, wq_ref, wk_ref, wv_ref, wz_ref, *refs):
    side_in, (q_ref, k_ref, v_ref, zs_ref, *side_out), (wb_ref,) = (
        refs[:n_side], refs[n_side:2 * n_side + 4], refs[2 * n_side + 4:])

    @pl.when(pl.program_id(1) == 0)
    def _():
        for k, w_ref in enumerate((wq_ref, wk_ref, wv_ref, wz_ref)):
            wb_ref[k] = w_ref[...].astype(BF16)

    tm = x_ref.shape[0]
    for r in range(tm // EPILOGUE_ROWS):
        rows = slice(r * EPILOGUE_ROWS, (r + 1) * EPILOGUE_ROWS)
        xb = x_ref[rows, :]
        zs_ref[rows, :] = _silu(_dot(xb, wb_ref[3])).astype(BF16)
        q_ref[rows, :] = (_dot(xb, wb_ref[0]) * (HEAD_DIM ** -0.5 * LOG2_E)).astype(BF16)
        k_ref[rows, :] = _dot(xb, wb_ref[1]).astype(BF16)
        v_ref[rows, :] = _dot(xb, wb_ref[2]).astype(BF16)
        if r == 0:
            _side_cast(side_in, side_out)


def _moba_in(xb, w, tm, side_sources):
    n = xb.shape[0]
    tn = MXU_COLS
    nj, ni = WIDTH // tn, n // tm
    out = jax.ShapeDtypeStruct((n, WIDTH), BF16)
    side_in_specs, side_out_specs, side_out_shapes = _side_cast_specs(
        side_sources, nj * ni, lambda j, i: j * ni + i)

    def wspec(k):
        return pl.BlockSpec((D_MODEL, tn), lambda j, i: (0, j + k * nj))

    ospec = pl.BlockSpec((tm, tn), lambda j, i: (i, j))
    return pl.pallas_call(
        partial(_moba_in_kernel, len(side_sources)),
        out_shape=[out, out, out, out] + side_out_shapes,
        grid=(nj, ni),
        in_specs=[pl.BlockSpec((tm, D_MODEL), lambda j, i: (i, 0)),
                  wspec(0), wspec(1), wspec(2), wspec(3)] + side_in_specs,
        out_specs=[ospec, ospec, ospec, ospec] + side_out_specs,
        scratch_shapes=[pltpu.VMEM((4, D_MODEL, tn), BF16)],
        compiler_params=pltpu.CompilerParams(
            dimension_semantics=("arbitrary", "arbitrary"),
            vmem_limit_bytes=VMEM_LIMIT),
        name="moba_in",
    )(xb, w, w, w, w, *[arr for arr, _ in side_sources])


def _moba_attn_kernel(q_ref, k_ref, v_ref, zs_ref, y_ref, vt_ref):
    seq = k_ref.shape[0]
    n_heads = k_ref.shape[1] // HEAD_DIM
    nb = seq // MOBA_BLOCK
    blk = MOBA_BLOCK
    neg_inf = jnp.float32(-jnp.inf)

    key_i = lax.broadcasted_iota(jnp.int32, (blk, blk), 0)
    qry_i = lax.broadcasted_iota(jnp.int32, (blk, blk), 1)
    causal_t = key_i <= qry_i
    blk_i = lax.broadcasted_iota(jnp.int32, (nb, blk), 0)

    def hcols(h):
        return slice(h * HEAD_DIM, (h + 1) * HEAD_DIM)

    def qblock(own, h):
        return q_ref[own * blk:(own + 1) * blk, hcols(h)]

    def scores(own, h):
        q = qblock(own, h)
        return [_dot_nt(k_ref[n * blk:(n + 1) * blk, hcols(h)], q) for n in range(own + 1)]

    def stats(own, h, s_blocks):
        bias = None
        if own > MOBA_TOP_K:
            gate = _dot_nt(k_means[h], qblock(own, h).astype(F32))
            gate = jnp.where(blk_i < own, gate, neg_inf)
            cnt = jnp.zeros((nb, blk), jnp.int32)
            for m in range(own):
                gm = gate[m:m + 1, :]
                beats = (gm > gate) | ((gm == gate) & (m < blk_i))
                cnt = cnt + jnp.where(beats, 1, 0)
            bias = jnp.where((cnt < MOBA_TOP_K) & (blk_i < own), 0.0, neg_inf)

        s_blocks = s_blocks[:own] + [jnp.where(causal_t, s_blocks[own], neg_inf)]
        m_q = jnp.max(s_blocks[own], axis=0, keepdims=True)
        for n in range(own):
            bm = jnp.max(s_blocks[n], axis=0, keepdims=True)
            if bias is not None:
                bm = bm + bias[n:n + 1, :]
            m_q = jnp.maximum(m_q, bm)
        return s_blocks, m_q, bias

    def exp_pv(own, h, s_blocks, m_q, bias):
        acc = jnp.zeros((VT_ROWS, blk), F32)
        for n in range(own + 1):
            shift = m_q
            if bias is not None and n < own:
                shift = m_q - bias[n:n + 1, :]
            pn = jnp.exp2(s_blocks[n] - shift)
            acc = acc + _dot(vt_ref[h, :, n * blk:(n + 1) * blk], pn.astype(BF16))
        o = (acc[0:HEAD_DIM, :] / acc[HEAD_DIM:HEAD_DIM + 1, :]).T
        rows = slice(own * blk, (own + 1) * blk)
        y_ref[rows, hcols(h)] = (o * zs_ref[rows, hcols(h)].astype(F32)).astype(BF16)

    units = [(own, h) for own in reversed(range(nb)) for h in range(n_heads)]
    pending = [scores(*units[0]), scores(*units[1])]

    k_means = []
    for h in range(n_heads):
        for n in range(nb):
            vt_ref[h, 0:HEAD_DIM, n * blk:(n + 1) * blk] = (
                v_ref[n * blk:(n + 1) * blk, hcols(h)].astype(F32).T.astype(BF16))
        vt_ref[h, HEAD_DIM:VT_ROWS, :] = jnp.ones((ONES_ROWS, seq), BF16)
        kf = k_ref[:, hcols(h)].astype(F32)
        k_means.append(jnp.mean(kf.reshape(nb, blk, HEAD_DIM), axis=1))

    st = stats(*units[0], pending.pop(0))
    for i, unit in enumerate(units):
        if i + 2 < len(units):
            pending.append(scores(*units[i + 2]))
        st_next = stats(*units[i + 1], pending.pop(0)) if i + 1 < len(units) else None
        exp_pv(*unit, *st)
        st = st_next


def _moba_attn(q, k, v, zs, batch, seq, heads_per_step):
    spec = pl.BlockSpec((seq, heads_per_step * HEAD_DIM), lambda b, g: (b, g))
    return pl.pallas_call(
        _moba_attn_kernel,
        out_shape=jax.ShapeDtypeStruct((batch * seq, WIDTH), BF16),
        grid=(batch, HEADS // heads_per_step),
        in_specs=[spec, spec, spec, spec],
        out_specs=spec,
        scratch_shapes=[pltpu.VMEM((heads_per_step, VT_ROWS, seq), BF16)],
        compiler_params=pltpu.CompilerParams(
            dimension_semantics=("arbitrary", "arbitrary"),
            vmem_limit_bytes=VMEM_LIMIT),
        name="moba_attn",
    )(q, k, v, zs)


def kernel(x, p, w_in_a, sgu_norm_g, sgu_norm_b, w_s, b_s, w_in_b, w_out, ln_g, ln_b,
           w_ple_gate, w_ple_proj):
    batch, seq, d = x.shape
    n = batch * seq
    x2 = x.reshape(n, d)
    p2 = p.reshape(DEPTH, n, PLE_DIM)

    y, wo0, wg0, wp_flat = _sgu(
        0, x2, w_in_a, sgu_norm_g, sgu_norm_b, w_s, b_s, tm=SGU_TM,
        side_sources=[(w_out, 0), (w_ple_gate, 0),
                      (w_ple_proj.reshape(DEPTH * PLE_DIM, d), None)])
    wp = wp_flat.reshape(DEPTH, PLE_DIM, d)
    x2, x2b = _tail(0, y, x2, p2, wo0, ln_g, ln_b, wg0, wp, tm=TAIL_TM, with_bf16_copy=True)

    q, k, v, zs, wo1, wg1 = _moba_in(
        x2b, w_in_b[0], tm=MOBA_IN_TM, side_sources=[(w_out, 1), (w_ple_gate, 1)])
    y = _moba_attn(q, k, v, zs, batch, seq, heads_per_step=ATTN_HEADS_PER_STEP)
    (x2,) = _tail(1, y, x2, p2, wo1, ln_g, ln_b, wg1, wp, tm=TAIL_TM, with_bf16_copy=False)

    return x2.reshape(batch, seq, d)
```

```python
from functools import partial

import jax
import jax.numpy as jnp
from jax import lax
from jax.experimental import pallas as pl
from jax.experimental.pallas import tpu as pltpu

D_MODEL = 2048
WIDTH = 2048
SGU_CHUNK = 128
SGU_GROUPS = 16
HEADS = 16
HEAD_DIM = 128
MOBA_BLOCK = 256
MOBA_TOP_K = 3
PLE_DIM = 256
LN_EPS = 1e-5
DEPTH = 2
ALPHA = (2 * DEPTH) ** 0.25
LOG2_E = 1.4426950408889634
MXU_COLS = 256
LANES = 128
EPILOGUE_ROWS = 512
TAIL_ROWS = 256
SGU_ROWS = 512
BF16_ROWS = 16
ONES_ROWS = BF16_ROWS
VT_ROWS = HEAD_DIM + ONES_ROWS

BF16 = jnp.bfloat16
F32 = jnp.float32

VMEM_LIMIT = 56 * 1024 * 1024
SGU_VMEM_LIMIT = 59 * 1024 * 1024
SGU_TM = 512
SGU_W_SLAB = 128
TAIL_TM = 512
MOBA_IN_TM = 1024
MOBA_IN_TN = 512
ATTN_HEADS_PER_STEP = 4


def _dot(a, b):
    return jnp.dot(a, b, preferred_element_type=F32)


def _dot_nt(a, b):
    return lax.dot_general(a, b, (((1,), (1,)), ((), ())), preferred_element_type=F32)


def _silu(z):
    return z * jax.nn.sigmoid(z)


class _RowStats:
    def __init__(self):
        self.pilot = self.d_acc = self.sq_acc = None

    def add(self, piece):
        def lane_fold(a):
            return a[:, 0:LANES] + a[:, LANES:2 * LANES]

        if self.pilot is None:
            self.pilot = jnp.mean(piece, axis=-1, keepdims=True)
        dc = piece - self.pilot
        d_part, sq_part = lane_fold(dc), lane_fold(dc * dc)
        self.d_acc = d_part if self.d_acc is None else self.d_acc + d_part
        self.sq_acc = sq_part if self.sq_acc is None else self.sq_acc + sq_part

    def mean_rstd(self, width):
        d_mean = jnp.sum(self.d_acc, axis=-1, keepdims=True) * (1.0 / width)
        var = jnp.sum(self.sq_acc, axis=-1, keepdims=True) * (1.0 / width) - d_mean * d_mean
        return self.pilot + d_mean, lax.rsqrt(var + LN_EPS)


def _side_cast_specs(sources, n_steps, step_index):
    in_specs, out_specs, out_shapes = [], [], []
    for arr, layer in sources:
        rows, cols = arr.shape[-2:]
        slab = rows // n_steps
        assert slab * n_steps == rows and slab % BF16_ROWS == 0, (rows, n_steps)
        if layer is None:
            in_specs.append(pl.BlockSpec((slab, cols), lambda *g: (step_index(*g), 0)))
        else:
            in_specs.append(pl.BlockSpec(
                (None, slab, cols), lambda *g, layer=layer: (layer, step_index(*g), 0)))
        out_specs.append(pl.BlockSpec((slab, cols), lambda *g: (step_index(*g), 0)))
        out_shapes.append(jax.ShapeDtypeStruct((rows, cols), BF16))
    return in_specs, out_specs, out_shapes


def _side_cast(side_in_refs, side_out_refs):
    for src_ref, dst_ref in zip(side_in_refs, side_out_refs):
        dst_ref[...] = src_ref[...].astype(BF16)


def _sgu_kernel(n_side, n_load, x_ref, wf_ref, ng_ref, nb_ref, ws_ref, bs_ref, *refs):
    side_in, (y_ref, *side_out), (w_ref, xb_ref, vg_ref, wc_ref, bst_ref) = (
        refs[:n_side], refs[n_side:2 * n_side + 1], refs[2 * n_side + 1:])
    step = pl.program_id(0)

    @pl.when(step < n_load)
    def _():
        slab = wf_ref.shape[0]
        w_ref[pl.ds(pl.multiple_of(step * slab, slab), slab), :] = wf_ref[...].astype(BF16)

    @pl.when(step == 0)
    def _():
        row = lax.broadcasted_iota(jnp.int32, (SGU_CHUNK, SGU_CHUNK), 0)
        col = lax.broadcasted_iota(jnp.int32, (SGU_CHUNK, SGU_CHUNK), 1)
        keep = (col <= row)[None]
        wc_ref[...] = jnp.where(keep, ws_ref[...], 0.0).astype(BF16)
        bst_ref[...] = bs_ref[...].T

    pl.when(step >= n_load)(partial(
        _sgu_tile, x_ref, w_ref, ng_ref, nb_ref, bst_ref, side_in, y_ref, side_out,
        xb_ref, vg_ref, wc_ref))


def _sgu_tile(x_ref, w_ref, ng_ref, nb_ref, bs_ref, side_in, y_ref, side_out,
              xb_ref, vg_ref, wc_ref):
    tm = x_ref.shape[0]
    n_pieces = WIDTH // MXU_COLS
    gw = WIDTH // SGU_GROUPS
    xb_ref[...] = x_ref[...].astype(BF16)

    def in_proj(path, c, epilogue):
        w_cols = slice(path * WIDTH + c * MXU_COLS, path * WIDTH + (c + 1) * MXU_COLS)
        parts = [epilogue(_dot(xb_ref[r * SGU_ROWS:(r + 1) * SGU_ROWS, :], w_ref[:, w_cols]))
                 for r in range(tm // SGU_ROWS)]
        return jnp.concatenate(parts, axis=0)

    stats = _RowStats()
    for c in range(n_pieces):
        vc = in_proj(1, c, jax.nn.gelu)
        vg_ref[:, c * MXU_COLS:(c + 1) * MXU_COLS] = vc
        stats.add(vc)
    _side_cast(side_in, side_out)
    mu, rstd = stats.mean_rstd(WIDTH)

    def uz_piece(c):
        return in_proj(0, c, jax.nn.gelu) * in_proj(2, c, _silu)

    uz_next = uz_piece(0)
    for c in range(n_pieces):
        cols = slice(c * MXU_COLS, (c + 1) * MXU_COLS)
        uz = uz_next
        if c + 1 < n_pieces:
            uz_next = uz_piece(c + 1)
        vn = ((vg_ref[:, cols] - mu) * rstd * ng_ref[:, cols] + nb_ref[:, cols]).astype(BF16)
        for k in range(tm // SGU_CHUNK):
            rows = slice(k * SGU_CHUNK, (k + 1) * SGU_CHUNK)
            for gi in range(MXU_COLS // gw):
                g = c * (MXU_COLS // gw) + gi
                gcols = slice(gi * gw, (gi + 1) * gw)
                s = _dot(wc_ref[g], vn[rows, gcols]) + bs_ref[:, g:g + 1]
                y_ref[rows, c * MXU_COLS + gi * gw:c * MXU_COLS + (gi + 1) * gw] = (
                    uz[rows, gcols] * s).astype(BF16)


def _sgu(layer, x, w, ng, nb, ws, bs, tm, side_sources):
    n = x.shape[0]
    n_load = D_MODEL // SGU_W_SLAB
    tile = lambda s: jnp.maximum(s - n_load, 0)
    row = lambda s: (tile(s), 0)
    side_in_specs, side_out_specs, side_out_shapes = _side_cast_specs(
        side_sources, n // tm, tile)
    return pl.pallas_call(
        partial(_sgu_kernel, len(side_sources), n_load),
        out_shape=[jax.ShapeDtypeStruct((n, WIDTH), BF16)] + side_out_shapes,
        grid=(n_load + n // tm,),
        in_specs=[
            pl.BlockSpec((tm, D_MODEL), row),
            pl.BlockSpec((None, SGU_W_SLAB, 3 * WIDTH),
                         lambda s: (layer, jnp.minimum(s, n_load - 1), 0)),
            pl.BlockSpec((1, WIDTH), lambda s: (layer, 0)),
            pl.BlockSpec((1, WIDTH), lambda s: (layer, 0)),
            pl.BlockSpec((None, SGU_GROUPS, SGU_CHUNK, SGU_CHUNK), lambda s: (layer, 0, 0, 0)),
            pl.BlockSpec((None, SGU_GROUPS, SGU_CHUNK), lambda s: (layer, 0, 0)),
        ] + side_in_specs,
        out_specs=[pl.BlockSpec((tm, WIDTH), row)] + side_out_specs,
        scratch_shapes=[pltpu.VMEM((D_MODEL, 3 * WIDTH), BF16),
                        pltpu.VMEM((tm, D_MODEL), BF16),
                        pltpu.VMEM((tm, WIDTH), F32),
                        pltpu.VMEM((SGU_GROUPS, SGU_CHUNK, SGU_CHUNK), BF16),
                        pltpu.VMEM((SGU_CHUNK, SGU_GROUPS), F32)],
        compiler_params=pltpu.CompilerParams(
            dimension_semantics=("arbitrary",), vmem_limit_bytes=SGU_VMEM_LIMIT),
        name="sgu",
    )(x, w, ng, nb, ws, bs, *[arr for arr, _ in side_sources])


def _tail_kernel(layer, y_ref, x_ref, p_ref, wo_ref, g_ref, b_ref, wg_ref, wp_ref, o_ref,
                 maybe_ob_ref=None):
    tm, d = o_ref.shape
    ln_g, ln_b = g_ref[layer:layer + 1, :], b_ref[layer:layer + 1, :]
    n_pieces = tm // TAIL_ROWS

    def pre_norm(r):
        rows = slice(r * TAIL_ROWS, (r + 1) * TAIL_ROWS)
        stats = _RowStats()
        for c in range(d // MXU_COLS):
            cols = slice(c * MXU_COLS, (c + 1) * MXU_COLS)
            t = ALPHA * x_ref[rows, cols] + _dot(y_ref[rows, :], wo_ref[:, cols])
            o_ref[rows, cols] = t
            stats.add(t)
        return stats.mean_rstd(d)

    mu_rstd = pre_norm(0)
    for r in range(n_pieces):
        rows = slice(r * TAIL_ROWS, (r + 1) * TAIL_ROWS)
        mu, rstd = mu_rstd
        if r + 1 < n_pieces:
            mu_rstd = pre_norm(r + 1)
        xn = (o_ref[rows, :] - mu) * rstd * ln_g + ln_b
        xnb = xn.astype(BF16)
        pb = p_ref[rows, :].astype(BF16)
        for c in range(d // MXU_COLS):
            cols = slice(c * MXU_COLS, (c + 1) * MXU_COLS)
            gate = jax.nn.sigmoid(_dot(xnb, wg_ref[:, cols]))
            proj = _dot(pb, wp_ref[:, cols])
            out = xn[:, cols] + gate * proj
            o_ref[rows, cols] = out
            if maybe_ob_ref is not None:
                maybe_ob_ref[rows, cols] = out.astype(BF16)


def _tail(layer, y, x, p, wo, g, b, wg, wp, tm, with_bf16_copy):
    n = x.shape[0]
    row = lambda i: (i, 0)
    fix = lambda i: (layer, 0, 0)
    fix2 = lambda i: (0, 0)
    once = pl.Buffered(1)
    out_shape = [jax.ShapeDtypeStruct((n, D_MODEL), F32)]
    out_specs = [pl.BlockSpec((tm, D_MODEL), row)]
    if with_bf16_copy:
        out_shape.append(jax.ShapeDtypeStruct((n, D_MODEL), BF16))
        out_specs.append(pl.BlockSpec((tm, D_MODEL), row))
    return pl.pallas_call(
        partial(_tail_kernel, layer),
        out_shape=out_shape,
        grid=(n // tm,),
        in_specs=[
            pl.BlockSpec((tm, WIDTH), row),
            pl.BlockSpec((tm, D_MODEL), row),
            pl.BlockSpec((None, tm, PLE_DIM), lambda i: (layer, i, 0)),
            pl.BlockSpec((WIDTH, D_MODEL), fix2, pipeline_mode=once),
            pl.BlockSpec((DEPTH, D_MODEL), fix2),
            pl.BlockSpec((DEPTH, D_MODEL), fix2),
            pl.BlockSpec((D_MODEL, D_MODEL), fix2, pipeline_mode=once),
            pl.BlockSpec((None, PLE_DIM, D_MODEL), fix, pipeline_mode=once),
        ],
        out_specs=out_specs,
        compiler_params=pltpu.CompilerParams(
            dimension_semantics=("arbitrary",), vmem_limit_bytes=VMEM_LIMIT),
        name="layer_tail",
    )(y, x, p, wo, g, b, wg, wp)


def _moba_in_kernel(n_side, x_ref, wq_ref, wk_ref, wv_ref, wz_ref, *refs):
    side_in, (q_ref, k_ref, v_ref, zs_ref, *side_out), (wb_ref,) = (
        refs[:n_side], refs[n_side:2 * n_side + 4], refs[2 * n_side + 4:])

    @pl.when(pl.program_id(1) == 0)
    def _():
        for k, w_ref in enumerate((wq_ref, wk_ref, wv_ref, wz_ref)):
            wb_ref[k] = w_ref[...].astype(BF16)

    tm = x_ref.shape[0]
    for r in range(tm // EPILOGUE_ROWS):
        rows = slice(r * EPILOGUE_ROWS, (r + 1) * EPILOGUE_ROWS)
        xb = x_ref[rows, :]
        zs_ref[rows, :] = _silu(_dot(xb, wb_ref[3])).astype(BF16)
        q_ref[rows, :] = (_dot(xb, wb_ref[0]) * (HEAD_DIM ** -0.5 * LOG2_E)).astype(BF16)
        k_ref[rows, :] = _dot(xb, wb_ref[1]).astype(BF16)
        v_ref[rows, :] = _dot(xb, wb_ref[2]).astype(BF16)
        if r == 0:
            _side_cast(side_in, side_out)


def _moba_in(xb, w, tm, side_sources):
    n = xb.shape[0]
    tn = MOBA_IN_TN
    nj, ni = WIDTH // tn, n // tm
    out = jax.ShapeDtypeStruct((n, WIDTH), BF16)
    side_in_specs, side_out_specs, side_out_shapes = _side_cast_specs(
        side_sources, nj * ni, lambda j, i: j * ni + i)

    def wspec(k):
        return pl.BlockSpec((D_MODEL, tn), lambda j, i: (0, j + k * nj))

    ospec = pl.BlockSpec((tm, tn), lambda j, i: (i, j))
    return pl.pallas_call(
        partial(_moba_in_kernel, len(side_sources)),
        out_shape=[out, out, out, out] + side_out_shapes,
        grid=(nj, ni),
        in_specs=[pl.BlockSpec((tm, D_MODEL), lambda j, i: (i, 0)),
                  wspec(0), wspec(1), wspec(2), wspec(3)] + side_in_specs,
        out_specs=[ospec, ospec, ospec, ospec] + side_out_specs,
        scratch_shapes=[pltpu.VMEM((4, D_MODEL, tn), BF16)],
        compiler_params=pltpu.CompilerParams(
            dimension_semantics=("arbitrary", "arbitrary"),
            vmem_limit_bytes=SGU_VMEM_LIMIT),
        name="moba_in",
    )(xb, w, w, w, w, *[arr for arr, _ in side_sources])


def _moba_attn_kernel(n_side, q_ref, k_ref, v_ref, zs_ref, *refs):
    side_in, (y_ref, *side_out), (vt_ref,) = (
        refs[:n_side], refs[n_side:2 * n_side + 1], refs[2 * n_side + 1:])
    _side_cast(side_in, side_out)
    seq = k_ref.shape[0]
    n_heads = k_ref.shape[1] // HEAD_DIM
    nb = seq // MOBA_BLOCK
    blk = MOBA_BLOCK
    neg_inf = jnp.float32(-jnp.inf)

    key_i = lax.broadcasted_iota(jnp.int32, (blk, blk), 0)
    qry_i = lax.broadcasted_iota(jnp.int32, (blk, blk), 1)
    causal_t = key_i <= qry_i
    blk_i = lax.broadcasted_iota(jnp.int32, (nb, blk), 0)

    def hcols(h):
        return slice(h * HEAD_DIM, (h + 1) * HEAD_DIM)

    def qblock(own, h):
        return q_ref[own * blk:(own + 1) * blk, hcols(h)]

    def scores(own, h):
        q = qblock(own, h)
        return [_dot_nt(k_ref[n * blk:(n + 1) * blk, hcols(h)], q) for n in range(own + 1)]

    def stats(own, h, s_blocks):
        bias = None
        if own > MOBA_TOP_K:
            gate = _dot_nt(k_means[h], qblock(own, h).astype(F32))
            gate = jnp.where(blk_i < own, gate, neg_inf)
            cnt = jnp.zeros((nb, blk), jnp.int32)
            for m in range(own):
                gm = gate[m:m + 1, :]
                beats = (gm > gate) | ((gm == gate) & (m < blk_i))
                cnt = cnt + jnp.where(beats, 1, 0)
            bias = jnp.where((cnt < MOBA_TOP_K) & (blk_i < own), 0.0, neg_inf)

        s_blocks = s_blocks[:own] + [jnp.where(causal_t, s_blocks[own], neg_inf)]
        m_q = jnp.max(s_blocks[own], axis=0, keepdims=True)
        for n in range(own):
            bm = jnp.max(s_blocks[n], axis=0, keepdims=True)
            if bias is not None:
                bm = bm + bias[n:n + 1, :]
            m_q = jnp.maximum(m_q, bm)
        return s_blocks, m_q, bias

    def exp_pv(own, h, s_blocks, m_q, bias):
        acc = jnp.zeros((VT_ROWS, blk), F32)
        for n in range(own + 1):
            shift = m_q
            if bias is not None and n < own:
                shift = m_q - bias[n:n + 1, :]
            pn = jnp.exp2(s_blocks[n] - shift)
            acc = acc + _dot(vt_ref[h, :, n * blk:(n + 1) * blk], pn.astype(BF16))
        o = (acc[0:HEAD_DIM, :] / acc[HEAD_DIM:HEAD_DIM + 1, :]).T
        rows = slice(own * blk, (own + 1) * blk)
        y_ref[rows, hcols(h)] = (o * zs_ref[rows, hcols(h)].astype(F32)).astype(BF16)

    units = [(own, h) for own in reversed(range(nb)) for h in range(n_heads)]
    pending = [scores(*units[0]), scores(*units[1])]

    k_means = []
    for h in range(n_heads):
        for n in range(nb):
            vt_ref[h, 0:HEAD_DIM, n * blk:(n + 1) * blk] = (
                v_ref[n * blk:(n + 1) * blk, hcols(h)].astype(F32).T.astype(BF16))
        vt_ref[h, HEAD_DIM:VT_ROWS, :] = jnp.ones((ONES_ROWS, seq), BF16)
        kf = k_ref[:, hcols(h)].astype(F32)
        k_means.append(jnp.mean(kf.reshape(nb, blk, HEAD_DIM), axis=1))

    st = stats(*units[0], pending.pop(0))
    for i, unit in enumerate(units):
        if i + 2 < len(units):
            pending.append(scores(*units[i + 2]))
        st_next = stats(*units[i + 1], pending.pop(0)) if i + 1 < len(units) else None
        exp_pv(*unit, *st)
        st = st_next


def _moba_attn(q, k, v, zs, batch, seq, heads_per_step, side_sources):
    spec = pl.BlockSpec((seq, heads_per_step * HEAD_DIM), lambda b, g: (b, g))
    n_groups = HEADS // heads_per_step
    side_in_specs, side_out_specs, side_out_shapes = _side_cast_specs(
        side_sources, batch * n_groups, lambda b, g: b * n_groups + g)
    return pl.pallas_call(
        partial(_moba_attn_kernel, len(side_sources)),
        out_shape=[jax.ShapeDtypeStruct((batch * seq, WIDTH), BF16)] + side_out_shapes,
        grid=(batch, n_groups),
        in_specs=[spec, spec, spec, spec] + side_in_specs,
        out_specs=[spec] + side_out_specs,
        scratch_shapes=[pltpu.VMEM((heads_per_step, VT_ROWS, seq), BF16)],
        compiler_params=pltpu.CompilerParams(
            dimension_semantics=("arbitrary", "arbitrary"),
            vmem_limit_bytes=VMEM_LIMIT),
        name="moba_attn",
    )(q, k, v, zs, *[arr for arr, _ in side_sources])


def kernel(x, p, w_in_a, sgu_norm_g, sgu_norm_b, w_s, b_s, w_in_b, w_out, ln_g, ln_b,
           w_ple_gate, w_ple_proj):
    batch, seq, d = x.shape
    n = batch * seq
    x2 = x.reshape(n, d)
    p2 = p.reshape(DEPTH, n, PLE_DIM)

    y, wo0, wg0, wp_flat = _sgu(
        0, x2, w_in_a, sgu_norm_g, sgu_norm_b, w_s, b_s, tm=SGU_TM,
        side_sources=[(w_out, 0), (w_ple_gate, 0),
                      (w_ple_proj.reshape(DEPTH * PLE_DIM, d), None)])
    wp = wp_flat.reshape(DEPTH, PLE_DIM, d)
    x2, x2b = _tail(0, y, x2, p2, wo0, ln_g, ln_b, wg0, wp, tm=TAIL_TM, with_bf16_copy=True)

    q, k, v, zs = _moba_in(x2b, w_in_b[0], tm=MOBA_IN_TM, side_sources=[])
    y, wo1, wg1 = _moba_attn(q, k, v, zs, batch, seq, heads_per_step=ATTN_HEADS_PER_STEP,
                             side_sources=[(w_out, 1), (w_ple_gate, 1)])
    (x2,) = _tail(1, y, x2, p2, wo1, ln_g, ln_b, wg1, wp, tm=TAIL_TM, with_bf16_copy=False)

    return x2.reshape(batch, seq, d)
```

```python
from functools import partial

import jax
import jax.numpy as jnp
from jax import lax
from jax.experimental import pallas as pl
from jax.experimental.pallas import tpu as pltpu

D_MODEL = 2048
WIDTH = 2048
SGU_CHUNK = 128
SGU_GROUPS = 16
HEADS = 16
HEAD_DIM = 128
MOBA_BLOCK = 256
MOBA_TOP_K = 3
PLE_DIM = 256
LN_EPS = 1e-5
DEPTH = 2
ALPHA = (2 * DEPTH) ** 0.25
LOG2_E = 1.4426950408889634
MXU_COLS = 256
LANES = 128
EPILOGUE_ROWS = 512
TAIL_ROWS = 256
SGU_ROWS = 512
BF16_ROWS = 16
ONES_ROWS = BF16_ROWS
VT_ROWS = HEAD_DIM + ONES_ROWS

BF16 = jnp.bfloat16
F32 = jnp.float32

VMEM_LIMIT = 56 * 1024 * 1024
SGU_VMEM_LIMIT = 59 * 1024 * 1024
SGU_TM = 512
SGU_W_SLAB = 128
TAIL_TM = 512
MOBA_IN_TM = 2048
ATTN_HEADS_PER_STEP = 4


def _dot(a, b):
    return jnp.dot(a, b, preferred_element_type=F32)


def _dot_nt(a, b):
    return lax.dot_general(a, b, (((1,), (1,)), ((), ())), preferred_element_type=F32)


def _silu(z):
    return z * jax.nn.sigmoid(z)


class _RowStats:
    def __init__(self):
        self.pilot = self.d_acc = self.sq_acc = None

    def add(self, piece):
        def lane_fold(a):
            return a[:, 0:LANES] + a[:, LANES:2 * LANES]

        if self.pilot is None:
            self.pilot = jnp.mean(piece, axis=-1, keepdims=True)
        dc = piece - self.pilot
        d_part, sq_part = lane_fold(dc), lane_fold(dc * dc)
        self.d_acc = d_part if self.d_acc is None else self.d_acc + d_part
        self.sq_acc = sq_part if self.sq_acc is None else self.sq_acc + sq_part

    def mean_rstd(self, width):
        d_mean = jnp.sum(self.d_acc, axis=-1, keepdims=True) * (1.0 / width)
        var = jnp.sum(self.sq_acc, axis=-1, keepdims=True) * (1.0 / width) - d_mean * d_mean
        return self.pilot + d_mean, lax.rsqrt(var + LN_EPS)


def _side_cast_specs(sources, n_steps, step_index):
    in_specs, out_specs, out_shapes = [], [], []
    for arr, layer in sources:
        rows, cols = arr.shape[-2:]
        slab = rows // n_steps
        assert slab * n_steps == rows and slab % BF16_ROWS == 0, (rows, n_steps)
        if layer is None:
            in_specs.append(pl.BlockSpec((slab, cols), lambda *g: (step_index(*g), 0)))
        else:
            in_specs.append(pl.BlockSpec(
                (None, slab, cols), lambda *g, layer=layer: (layer, step_index(*g), 0)))
        out_specs.append(pl.BlockSpec((slab, cols), lambda *g: (step_index(*g), 0)))
        out_shapes.append(jax.ShapeDtypeStruct((rows, cols), BF16))
    return in_specs, out_specs, out_shapes


def _side_cast(side_in_refs, side_out_refs):
    for src_ref, dst_ref in zip(side_in_refs, side_out_refs):
        dst_ref[...] = src_ref[...].astype(BF16)


def _sgu_kernel(n_side, n_load, x_ref, wf_ref, ng_ref, nb_ref, ws_ref, bs_ref, *refs):
    side_in, (y_ref, *side_out), (w_ref, xb_ref, vg_ref, wc_ref, bst_ref) = (
        refs[:n_side], refs[n_side:2 * n_side + 1], refs[2 * n_side + 1:])
    step = pl.program_id(0)

    @pl.when(step < n_load)
    def _():
        slab = wf_ref.shape[0]
        w_ref[pl.ds(pl.multiple_of(step * slab, slab), slab), :] = wf_ref[...].astype(BF16)

    @pl.when(step == 0)
    def _():
        row = lax.broadcasted_iota(jnp.int32, (SGU_CHUNK, SGU_CHUNK), 0)
        col = lax.broadcasted_iota(jnp.int32, (SGU_CHUNK, SGU_CHUNK), 1)
        keep = (col <= row)[None]
        wc_ref[...] = jnp.where(keep, ws_ref[...], 0.0).astype(BF16)
        bst_ref[...] = bs_ref[...].T

    pl.when(step >= n_load)(partial(
        _sgu_tile, x_ref, w_ref, ng_ref, nb_ref, bst_ref, side_in, y_ref, side_out,
        xb_ref, vg_ref, wc_ref))


def _sgu_tile(x_ref, w_ref, ng_ref, nb_ref, bs_ref, side_in, y_ref, side_out,
              xb_ref, vg_ref, wc_ref):
    tm = x_ref.shape[0]
    n_pieces = WIDTH // MXU_COLS
    gw = WIDTH // SGU_GROUPS
    xb_ref[...] = x_ref[...].astype(BF16)

    def in_proj(path, c, epilogue):
        w_cols = slice(path * WIDTH + c * MXU_COLS, path * WIDTH + (c + 1) * MXU_COLS)
        parts = [epilogue(_dot(xb_ref[r * SGU_ROWS:(r + 1) * SGU_ROWS, :], w_ref[:, w_cols]))
                 for r in range(tm // SGU_ROWS)]
        return jnp.concatenate(parts, axis=0)

    stats = _RowStats()
    for c in range(n_pieces):
        vc = in_proj(1, c, jax.nn.gelu)
        vg_ref[:, c * MXU_COLS:(c + 1) * MXU_COLS] = vc
        stats.add(vc)
    _side_cast(side_in, side_out)
    mu, rstd = stats.mean_rstd(WIDTH)

    def uz_piece(c):
        return in_proj(0, c, jax.nn.gelu) * in_proj(2, c, _silu)

    uz_next = uz_piece(0)
    for c in range(n_pieces):
        cols = slice(c * MXU_COLS, (c + 1) * MXU_COLS)
        uz = uz_next
        if c + 1 < n_pieces:
            uz_next = uz_piece(c + 1)
        vn = ((vg_ref[:, cols] - mu) * rstd * ng_ref[:, cols] + nb_ref[:, cols]).astype(BF16)
        n_chunks = tm // SGU_CHUNK
        for gi in range(MXU_COLS // gw):
            g = c * (MXU_COLS // gw) + gi
            gcols = slice(gi * gw, (gi + 1) * gw)
            v_chunks = jnp.concatenate(
                [vn[k * SGU_CHUNK:(k + 1) * SGU_CHUNK, gcols] for k in range(n_chunks)], axis=1)
            s_chunks = _dot(wc_ref[g], v_chunks) + bs_ref[:, g:g + 1]
            for k in range(n_chunks):
                rows = slice(k * SGU_CHUNK, (k + 1) * SGU_CHUNK)
                y_ref[rows, c * MXU_COLS + gi * gw:c * MXU_COLS + (gi + 1) * gw] = (
                    uz[rows, gcols] * s_chunks[:, k * gw:(k + 1) * gw]).astype(BF16)


def _sgu(layer, x, w, ng, nb, ws, bs, tm, side_sources):
    n = x.shape[0]
    n_load = D_MODEL // SGU_W_SLAB
    tile = lambda s: jnp.maximum(s - n_load, 0)
    row = lambda s: (tile(s), 0)
    side_in_specs, side_out_specs, side_out_shapes = _side_cast_specs(
        side_sources, n // tm, tile)
    return pl.pallas_call(
        partial(_sgu_kernel, len(side_sources), n_load),
        out_shape=[jax.ShapeDtypeStruct((n, WIDTH), BF16)] + side_out_shapes,
        grid=(n_load + n // tm,),
        in_specs=[
            pl.BlockSpec((tm, D_MODEL), row),
            pl.BlockSpec((None, SGU_W_SLAB, 3 * WIDTH),
                         lambda s: (layer, jnp.minimum(s, n_load - 1), 0)),
            pl.BlockSpec((1, WIDTH), lambda s: (layer, 0)),
            pl.BlockSpec((1, WIDTH), lambda s: (layer, 0)),
            pl.BlockSpec((None, SGU_GROUPS, SGU_CHUNK, SGU_CHUNK), lambda s: (layer, 0, 0, 0)),
            pl.BlockSpec((None, SGU_GROUPS, SGU_CHUNK), lambda s: (layer, 0, 0)),
        ] + side_in_specs,
        out_specs=[pl.BlockSpec((tm, WIDTH), row)] + side_out_specs,
        scratch_shapes=[pltpu.VMEM((D_MODEL, 3 * WIDTH), BF16),
                        pltpu.VMEM((tm, D_MODEL), BF16),
                        pltpu.VMEM((tm, WIDTH), F32),
                        pltpu.VMEM((SGU_GROUPS, SGU_CHUNK, SGU_CHUNK), BF16),
                        pltpu.VMEM((SGU_CHUNK, SGU_GROUPS), F32)],
        compiler_params=pltpu.CompilerParams(
            dimension_semantics=("arbitrary",), vmem_limit_bytes=SGU_VMEM_LIMIT),
        name="sgu",
    )(x, w, ng, nb, ws, bs, *[arr for arr, _ in side_sources])


def _tail_kernel(layer, y_ref, x_ref, p_ref, wo_ref, g_ref, b_ref, wg_ref, wp_ref, o_ref,
                 maybe_ob_ref=None):
    tm, d = o_ref.shape
    ln_g, ln_b = g_ref[layer:layer + 1, :], b_ref[layer:layer + 1, :]
    n_pieces = tm // TAIL_ROWS

    def pre_norm(r):
        rows = slice(r * TAIL_ROWS, (r + 1) * TAIL_ROWS)
        stats = _RowStats()
        for c in range(d // MXU_COLS):
            cols = slice(c * MXU_COLS, (c + 1) * MXU_COLS)
            t = ALPHA * x_ref[rows, cols] + _dot(y_ref[rows, :], wo_ref[:, cols])
            o_ref[rows, cols] = t
            stats.add(t)
        return stats.mean_rstd(d)

    mu_rstd = pre_norm(0)
    for r in range(n_pieces):
        rows = slice(r * TAIL_ROWS, (r + 1) * TAIL_ROWS)
        mu, rstd = mu_rstd
        if r + 1 < n_pieces:
            mu_rstd = pre_norm(r + 1)
        xn = (o_ref[rows, :] - mu) * rstd * ln_g + ln_b
        xnb = xn.astype(BF16)
        pb = p_ref[rows, :].astype(BF16)
        for c in range(d // MXU_COLS):
            cols = slice(c * MXU_COLS, (c + 1) * MXU_COLS)
            gate = jax.nn.sigmoid(_dot(xnb, wg_ref[:, cols]))
            proj = _dot(pb, wp_ref[:, cols])
            out = xn[:, cols] + gate * proj
            o_ref[rows, cols] = out
            if maybe_ob_ref is not None:
                maybe_ob_ref[rows, cols] = out.astype(BF16)


def _tail(layer, y, x, p, wo, g, b, wg, wp, tm, with_bf16_copy):
    n = x.shape[0]
    row = lambda i: (i, 0)
    fix = lambda i: (layer, 0, 0)
    fix2 = lambda i: (0, 0)
    once = pl.Buffered(1)
    out_shape = [jax.ShapeDtypeStruct((n, D_MODEL), F32)]
    out_specs = [pl.BlockSpec((tm, D_MODEL), row)]
    if with_bf16_copy:
        out_shape.append(jax.ShapeDtypeStruct((n, D_MODEL), BF16))
        out_specs.append(pl.BlockSpec((tm, D_MODEL), row))
    return pl.pallas_call(
        partial(_tail_kernel, layer),
        out_shape=out_shape,
        grid=(n // tm,),
        in_specs=[
            pl.BlockSpec((tm, WIDTH), row),
            pl.BlockSpec((tm, D_MODEL), row),
            pl.BlockSpec((None, tm, PLE_DIM), lambda i: (layer, i, 0)),
            pl.BlockSpec((WIDTH, D_MODEL), fix2, pipeline_mode=once),
            pl.BlockSpec((DEPTH, D_MODEL), fix2),
            pl.BlockSpec((DEPTH, D_MODEL), fix2),
            pl.BlockSpec((D_MODEL, D_MODEL), fix2, pipeline_mode=once),
            pl.BlockSpec((None, PLE_DIM, D_MODEL), fix, pipeline_mode=once),
        ],
        out_specs=out_specs,
        compiler_params=pltpu.CompilerParams(
            dimension_semantics=("arbitrary",), vmem_limit_bytes=VMEM_LIMIT),
        name="layer_tail",
    )(y, x, p, wo, g, b, wg, wp)


def _moba_in_kernel(n_side, x_ref, wq_ref, wk_ref, wv_ref, wz_ref, *refs):
    side_in, (q_ref, k_ref, v_ref, zs_ref, *side_out), (wb_ref,) = (
        refs[:n_side], refs[n_side:2 * n_side + 4], refs[2 * n_side + 4:])

    @pl.when(pl.program_id(1) == 0)
    def _():
        for k, w_ref in enumerate((wq_ref, wk_ref, wv_ref, wz_ref)):
            wb_ref[k] = w_ref[...].astype(BF16)

    tm = x_ref.shape[0]
    for r in range(tm // EPILOGUE_ROWS):
        rows = slice(r * EPILOGUE_ROWS, (r + 1) * EPILOGUE_ROWS)
        xb = x_ref[rows, :]
        zs_ref[rows, :] = _silu(_dot(xb, wb_ref[3])).astype(BF16)
        q_ref[rows, :] = (_dot(xb, wb_ref[0]) * (HEAD_DIM ** -0.5 * LOG2_E)).astype(BF16)
        k_ref[rows, :] = _dot(xb, wb_ref[1]).astype(BF16)
        v_ref[rows, :] = _dot(xb, wb_ref[2]).astype(BF16)
        if r == 0:
            _side_cast(side_in, side_out)


def _moba_in(xb, w, tm, side_sources):
    n = xb.shape[0]
    tn = MXU_COLS
    nj, ni = WIDTH // tn, n // tm
    out = jax.ShapeDtypeStruct((n, WIDTH), BF16)
    side_in_specs, side_out_specs, side_out_shapes = _side_cast_specs(
        side_sources, nj * ni, lambda j, i: j * ni + i)

    def wspec(k):
        return pl.BlockSpec((D_MODEL, tn), lambda j, i: (0, j + k * nj))

    ospec = pl.BlockSpec((tm, tn), lambda j, i: (i, j))
    return pl.pallas_call(
        partial(_moba_in_kernel, len(side_sources)),
        out_shape=[out, out, out, out] + side_out_shapes,
        grid=(nj, ni),
        in_specs=[pl.BlockSpec((tm, D_MODEL), lambda j, i: (i, 0)),
                  wspec(0), wspec(1), wspec(2), wspec(3)] + side_in_specs,
        out_specs=[ospec, ospec, ospec, ospec] + side_out_specs,
        scratch_shapes=[pltpu.VMEM((4, D_MODEL, tn), BF16)],
        compiler_params=pltpu.CompilerParams(
            dimension_semantics=("arbitrary", "arbitrary"),
            vmem_limit_bytes=VMEM_LIMIT),
        name="moba_in",
    )(xb, w, w, w, w, *[arr for arr, _ in side_sources])


def _moba_attn_kernel(q_ref, k_ref, v_ref, zs_ref, y_ref, vt_ref):
    seq = k_ref.shape[0]
    n_heads = k_ref.shape[1] // HEAD_DIM
    nb = seq // MOBA_BLOCK
    blk = MOBA_BLOCK
    neg_inf = jnp.float32(-jnp.inf)

    key_i = lax.broadcasted_iota(jnp.int32, (blk, blk), 0)
    qry_i = lax.broadcasted_iota(jnp.int32, (blk, blk), 1)
    causal_t = key_i <= qry_i
    blk_i = lax.broadcasted_iota(jnp.int32, (nb, blk), 0)

    def hcols(h):
        return slice(h * HEAD_DIM, (h + 1) * HEAD_DIM)

    def qblock(own, h):
        return q_ref[own * blk:(own + 1) * blk, hcols(h)]

    def scores(own, h):
        q = qblock(own, h)
        return [_dot_nt(k_ref[n * blk:(n + 1) * blk, hcols(h)], q) for n in range(own + 1)]

    def stats(own, h, s_blocks):
        bias = None
        if own > MOBA_TOP_K:
            gate = _dot_nt(k_means[h], qblock(own, h).astype(F32))
            gate = jnp.where(blk_i < own, gate, neg_inf)
            cnt = jnp.zeros((nb, blk), jnp.int32)
            for m in range(own):
                gm = gate[m:m + 1, :]
                beats = (gm > gate) | ((gm == gate) & (m < blk_i))
                cnt = cnt + jnp.where(beats, 1, 0)
            bias = jnp.where((cnt < MOBA_TOP_K) & (blk_i < own), 0.0, neg_inf)

        s_blocks = s_blocks[:own] + [jnp.where(causal_t, s_blocks[own], neg_inf)]
        m_q = jnp.max(s_blocks[own], axis=0, keepdims=True)
        for n in range(own):
            bm = jnp.max(s_blocks[n], axis=0, keepdims=True)
            if bias is not None:
                bm = bm + bias[n:n + 1, :]
            m_q = jnp.maximum(m_q, bm)
        return s_blocks, m_q, bias

    def exp_pv(own, h, s_blocks, m_q, bias):
        acc = jnp.zeros((VT_ROWS, blk), F32)
        for n in range(own + 1):
            shift = m_q
            if bias is not None and n < own:
                shift = m_q - bias[n:n + 1, :]
            pn = jnp.exp2(s_blocks[n] - shift)
            acc = acc + _dot(vt_ref[h, :, n * blk:(n + 1) * blk], pn.astype(BF16))
        o = (acc[0:HEAD_DIM, :] / acc[HEAD_DIM:HEAD_DIM + 1, :]).T
        rows = slice(own * blk, (own + 1) * blk)
        y_ref[rows, hcols(h)] = (o * zs_ref[rows, hcols(h)].astype(F32)).astype(BF16)

    units = [(own, h) for own in reversed(range(nb)) for h in range(n_heads)]
    pending = [scores(*units[0]), scores(*units[1])]

    k_means = []
    for h in range(n_heads):
        for n in range(nb):
            vt_ref[h, 0:HEAD_DIM, n * blk:(n + 1) * blk] = (
                v_ref[n * blk:(n + 1) * blk, hcols(h)].astype(F32).T.astype(BF16))
        vt_ref[h, HEAD_DIM:VT_ROWS, :] = jnp.ones((ONES_ROWS, seq), BF16)
        kf = k_ref[:, hcols(h)].astype(F32)
        k_means.append(jnp.mean(kf.reshape(nb, blk, HEAD_DIM), axis=1))

    st = stats(*units[0], pending.pop(0))
    for i, unit in enumerate(units):
        if i + 2 < len(units):
            pending.append(scores(*units[i + 2]))
        st_next = stats(*units[i + 1], pending.pop(0)) if i + 1 < len(units) else None
        exp_pv(*unit, *st)
        st = st_next


def _moba_attn(q, k, v, zs, batch, seq, heads_per_step):
    spec = pl.BlockSpec((seq, heads_per_step * HEAD_DIM), lambda b, g: (b, g))
    return pl.pallas_call(
        _moba_attn_kernel,
        out_shape=jax.ShapeDtypeStruct((batch * seq, WIDTH), BF16),
        grid=(batch, HEADS // heads_per_step),
        in_specs=[spec, spec, spec, spec],
        out_specs=spec,
        scratch_shapes=[pltpu.VMEM((heads_per_step, VT_ROWS, seq), BF16)],
        compiler_params=pltpu.CompilerParams(
            dimension_semantics=("arbitrary", "arbitrary"),
            vmem_limit_bytes=VMEM_LIMIT),
        name="moba_attn",
    )(q, k, v, zs)


def kernel(x, p, w_in_a, sgu_norm_g, sgu_norm_b, w_s, b_s, w_in_b, w_out, ln_g, ln_b,
           w_ple_gate, w_ple_proj):
    batch, seq, d = x.shape
    n = batch * seq
    x2 = x.reshape(n, d)
    p2 = p.reshape(DEPTH, n, PLE_DIM)

    y, wo0, wg0, wp_flat = _sgu(
        0, x2, w_in_a, sgu_norm_g, sgu_norm_b, w_s, b_s, tm=SGU_TM,
        side_sources=[(w_out, 0), (w_ple_gate, 0),
                      (w_ple_proj.reshape(DEPTH * PLE_DIM, d), None)])
    wp = wp_flat.reshape(DEPTH, PLE_DIM, d)
    x2, x2b = _tail(0, y, x2, p2, wo0, ln_g, ln_b, wg0, wp, tm=TAIL_TM, with_bf16_copy=True)

    q, k, v, zs, wo1, wg1 = _moba_in(
        x2b, w_in_b[0], tm=MOBA_IN_TM, side_sources=[(w_out, 1), (w_ple_gate, 1)])
    y = _moba_attn(q, k, v, zs, batch, seq, heads_per_step=ATTN_HEADS_PER_STEP)
    (x2,) = _tail(1, y, x2, p2, wo1, ln_g, ln_b, wg1, wp, tm=TAIL_TM, with_bf16_copy=False)

    return x2.reshape(batch, seq, d)
```

```python
from functools import partial

import jax
import jax.numpy as jnp
from jax import lax
from jax.experimental import pallas as pl
from jax.experimental.pallas import tpu as pltpu

D_MODEL = 2048
WIDTH = 2048
SGU_CHUNK = 128
SGU_GROUPS = 16
HEADS = 16
HEAD_DIM = 128
MOBA_BLOCK = 256
MOBA_TOP_K = 3
PLE_DIM = 256
LN_EPS = 1e-5
DEPTH = 2
ALPHA = (2 * DEPTH) ** 0.25
LOG2_E = 1.4426950408889634
MXU_COLS = 256
LANES = 128
EPILOGUE_ROWS = 512
TAIL_ROWS = 256
SGU_ROWS = 512
BF16_ROWS = 16
ONES_ROWS = BF16_ROWS
VT_ROWS = HEAD_DIM + ONES_ROWS

BF16 = jnp.bfloat16
F32 = jnp.float32

VMEM_LIMIT = 56 * 1024 * 1024
SGU_VMEM_LIMIT = 59 * 1024 * 1024
SGU_TM = 512
SGU_W_SLAB = 128
TAIL_TM = 512
MOBA_IN_TM = 2048
ATTN_HEADS_PER_STEP = 4


def _dot(a, b):
    return jnp.dot(a, b, preferred_element_type=F32)


def _dot_nt(a, b):
    return lax.dot_general(a, b, (((1,), (1,)), ((), ())), preferred_element_type=F32)


def _silu(z):
    return z * jax.nn.sigmoid(z)


class _RowStats:
    def __init__(self):
        self.pilot = self.d_acc = self.sq_acc = None

    def add(self, piece):
        def lane_fold(a):
            return a[:, 0:LANES] + a[:, LANES:2 * LANES]

        if self.pilot is None:
            self.pilot = jnp.mean(piece, axis=-1, keepdims=True)
        dc = piece - self.pilot
        d_part, sq_part = lane_fold(dc), lane_fold(dc * dc)
        self.d_acc = d_part if self.d_acc is None else self.d_acc + d_part
        self.sq_acc = sq_part if self.sq_acc is None else self.sq_acc + sq_part

    def mean_rstd(self, width):
        d_mean = jnp.sum(self.d_acc, axis=-1, keepdims=True) * (1.0 / width)
        var = jnp.sum(self.sq_acc, axis=-1, keepdims=True) * (1.0 / width) - d_mean * d_mean
        return self.pilot + d_mean, lax.rsqrt(var + LN_EPS)


def _side_cast_specs(sources, n_steps, step_index):
    in_specs, out_specs, out_shapes = [], [], []
    for arr, layer in sources:
        rows, cols = arr.shape[-2:]
        slab = rows // n_steps
        assert slab * n_steps == rows and slab % BF16_ROWS == 0, (rows, n_steps)
        if layer is None:
            in_specs.append(pl.BlockSpec((slab, cols), lambda *g: (step_index(*g), 0)))
        else:
            in_specs.append(pl.BlockSpec(
                (None, slab, cols), lambda *g, layer=layer: (layer, step_index(*g), 0)))
        out_specs.append(pl.BlockSpec((slab, cols), lambda *g: (step_index(*g), 0)))
        out_shapes.append(jax.ShapeDtypeStruct((rows, cols), BF16))
    return in_specs, out_specs, out_shapes


def _side_cast(side_in_refs, side_out_refs):
    for src_ref, dst_ref in zip(side_in_refs, side_out_refs):
        dst_ref[...] = src_ref[...].astype(BF16)


def _sgu_kernel(n_side, n_load, x_ref, wf_ref, ng_ref, nb_ref, ws_ref, bs_ref, *refs):
    side_in, (y_ref, *side_out), (w_ref, xb_ref, vg_ref, wc_ref, bst_ref) = (
        refs[:n_side], refs[n_side:2 * n_side + 1], refs[2 * n_side + 1:])
    step = pl.program_id(0)

    @pl.when(step < n_load)
    def _():
        slab = wf_ref.shape[0]
        w_ref[pl.ds(pl.multiple_of(step * slab, slab), slab), :] = wf_ref[...].astype(BF16)

    @pl.when(step == 0)
    def _():
        row = lax.broadcasted_iota(jnp.int32, (SGU_CHUNK, SGU_CHUNK), 0)
        col = lax.broadcasted_iota(jnp.int32, (SGU_CHUNK, SGU_CHUNK), 1)
        keep = (col <= row)[None]
        wc_ref[...] = jnp.where(keep, ws_ref[...], 0.0).astype(BF16)
        bst_ref[...] = bs_ref[...].T

    pl.when(step >= n_load)(partial(
        _sgu_tile, x_ref, w_ref, ng_ref, nb_ref, bst_ref, side_in, y_ref, side_out,
        xb_ref, vg_ref, wc_ref))


def _sgu_tile(x_ref, w_ref, ng_ref, nb_ref, bs_ref, side_in, y_ref, side_out,
              xb_ref, vg_ref, wc_ref):
    tm = x_ref.shape[0]
    n_pieces = WIDTH // MXU_COLS
    gw = WIDTH // SGU_GROUPS
    xb_ref[...] = x_ref[...].astype(BF16)

    def in_proj(path, c, epilogue):
        w_cols = slice(path * WIDTH + c * MXU_COLS, path * WIDTH + (c + 1) * MXU_COLS)
        parts = [epilogue(_dot(xb_ref[r * SGU_ROWS:(r + 1) * SGU_ROWS, :], w_ref[:, w_cols]))
                 for r in range(tm // SGU_ROWS)]
        return jnp.concatenate(parts, axis=0)

    stats = _RowStats()
    for c in range(n_pieces):
        vc = in_proj(1, c, jax.nn.gelu)
        vg_ref[:, c * MXU_COLS:(c + 1) * MXU_COLS] = vc
        stats.add(vc)
    _side_cast(side_in, side_out)
    mu, rstd = stats.mean_rstd(WIDTH)

    def uz_piece(c):
        return in_proj(0, c, jax.nn.gelu) * in_proj(2, c, _silu)

    uz_next = uz_piece(0)
    for c in range(n_pieces):
        cols = slice(c * MXU_COLS, (c + 1) * MXU_COLS)
        uz = uz_next
        if c + 1 < n_pieces:
            uz_next = uz_piece(c + 1)
        vn = ((vg_ref[:, cols] - mu) * rstd * ng_ref[:, cols] + nb_ref[:, cols]).astype(BF16)
        for k in range(tm // SGU_CHUNK):
            rows = slice(k * SGU_CHUNK, (k + 1) * SGU_CHUNK)
            for gi in range(MXU_COLS // gw):
                g = c * (MXU_COLS // gw) + gi
                gcols = slice(gi * gw, (gi + 1) * gw)
                s = _dot(wc_ref[g], vn[rows, gcols]) + bs_ref[:, g:g + 1]
                y_ref[rows, c * MXU_COLS + gi * gw:c * MXU_COLS + (gi + 1) * gw] = (
                    uz[rows, gcols] * s).astype(BF16)


def _sgu(layer, x, w, ng, nb, ws, bs, tm, side_sources):
    n = x.shape[0]
    n_load = D_MODEL // SGU_W_SLAB
    tile = lambda s: jnp.maximum(s - n_load, 0)
    row = lambda s: (tile(s), 0)
    side_in_specs, side_out_specs, side_out_shapes = _side_cast_specs(
        side_sources, n // tm, tile)
    return pl.pallas_call(
        partial(_sgu_kernel, len(side_sources), n_load),
        out_shape=[jax.ShapeDtypeStruct((n, WIDTH), BF16)] + side_out_shapes,
        grid=(n_load + n // tm,),
        in_specs=[
            pl.BlockSpec((tm, D_MODEL), row),
            pl.BlockSpec((None, SGU_W_SLAB, 3 * WIDTH),
                         lambda s: (layer, jnp.minimum(s, n_load - 1), 0)),
            pl.BlockSpec((1, WIDTH), lambda s: (layer, 0)),
            pl.BlockSpec((1, WIDTH), lambda s: (layer, 0)),
            pl.BlockSpec((None, SGU_GROUPS, SGU_CHUNK, SGU_CHUNK), lambda s: (layer, 0, 0, 0)),
            pl.BlockSpec((None, SGU_GROUPS, SGU_CHUNK), lambda s: (layer, 0, 0)),
        ] + side_in_specs,
        out_specs=[pl.BlockSpec((tm, WIDTH), row)] + side_out_specs,
        scratch_shapes=[pltpu.VMEM((D_MODEL, 3 * WIDTH), BF16),
                        pltpu.VMEM((tm, D_MODEL), BF16),
                        pltpu.VMEM((tm, WIDTH), F32),
                        pltpu.VMEM((SGU_GROUPS, SGU_CHUNK, SGU_CHUNK), BF16),
                        pltpu.VMEM((SGU_CHUNK, SGU_GROUPS), F32)],
        compiler_params=pltpu.CompilerParams(
            dimension_semantics=("arbitrary",), vmem_limit_bytes=SGU_VMEM_LIMIT),
        name="sgu",
    )(x, w, ng, nb, ws, bs, *[arr for arr, _ in side_sources])


def _tail_kernel(layer, y_ref, x_ref, p_ref, wo_ref, g_ref, b_ref, wg_ref, wp_ref, o_ref,
                 maybe_ob_ref=None):
    tm, d = o_ref.shape
    ln_g, ln_b = g_ref[layer:layer + 1, :], b_ref[layer:layer + 1, :]
    n_pieces = tm // TAIL_ROWS

    def pre_norm(r):
        rows = slice(r * TAIL_ROWS, (r + 1) * TAIL_ROWS)
        stats = _RowStats()
        for c in range(d // MXU_COLS):
            cols = slice(c * MXU_COLS, (c + 1) * MXU_COLS)
            t = ALPHA * x_ref[rows, cols] + _dot(y_ref[rows, :], wo_ref[:, cols])
            o_ref[rows, cols] = t
            stats.add(t)
        return stats.mean_rstd(d)

    mu_rstd = pre_norm(0)
    for r in range(n_pieces):
        rows = slice(r * TAIL_ROWS, (r + 1) * TAIL_ROWS)
        mu, rstd = mu_rstd
        if r + 1 < n_pieces:
            mu_rstd = pre_norm(r + 1)
        xn = (o_ref[rows, :] - mu) * rstd * ln_g + ln_b
        xnb = xn.astype(BF16)
        pb = p_ref[rows, :].astype(BF16)
        for c in range(d // MXU_COLS):
            cols = slice(c * MXU_COLS, (c + 1) * MXU_COLS)
            gate = jax.nn.sigmoid(_dot(xnb, wg_ref[:, cols]))
            proj = _dot(pb, wp_ref[:, cols])
            out = xn[:, cols] + gate * proj
            o_ref[rows, cols] = out
            if maybe_ob_ref is not None:
                maybe_ob_ref[rows, cols] = out.astype(BF16)


def _tail(layer, y, x, p, wo, g, b, wg, wp, tm, with_bf16_copy):
    n = x.shape[0]
    row = lambda i: (i, 0)
    fix = lambda i: (layer, 0, 0)
    fix2 = lambda i: (0, 0)
    once = pl.Buffered(1)
    out_shape = [jax.ShapeDtypeStruct((n, D_MODEL), F32)]
    out_specs = [pl.BlockSpec((tm, D_MODEL), row)]
    if with_bf16_copy:
        out_shape.append(jax.ShapeDtypeStruct((n, D_MODEL), BF16))
        out_specs.append(pl.BlockSpec((tm, D_MODEL), row))
    return pl.pallas_call(
        partial(_tail_kernel, layer),
        out_shape=out_shape,
        grid=(n // tm,),
        in_specs=[
            pl.BlockSpec((tm, WIDTH), row),
            pl.BlockSpec((tm, D_MODEL), row),
            pl.BlockSpec((None, tm, PLE_DIM), lambda i: (layer, i, 0)),
            pl.BlockSpec((WIDTH, D_MODEL), fix2, pipeline_mode=once),
            pl.BlockSpec((DEPTH, D_MODEL), fix2),
            pl.BlockSpec((DEPTH, D_MODEL), fix2),
            pl.BlockSpec((D_MODEL, D_MODEL), fix2, pipeline_mode=once),
            pl.BlockSpec((None, PLE_DIM, D_MODEL), fix, pipeline_mode=once),
        ],
        out_specs=out_specs,
        compiler_params=pltpu.CompilerParams(
            dimension_semantics=("arbitrary",), vmem_limit_bytes=VMEM_LIMIT),
        name="layer_tail",
    )(y, x, p, wo, g, b, wg, wp)


def _moba_in_kernel(n_side, x_ref, wq_ref, wk_ref, wv_ref, wz_ref, *refs):
    side_in, (q_ref, k_ref, v_ref, zs_ref, *side_out), (wb_ref,) = (
        refs[:n_side], refs[n_side:2 * n_side + 4], refs[2 * n_side + 4:])

    @pl.when(pl.program_id(1) == 0)
    def _():
        for k, w_ref in enumerate((wq_ref, wk_ref, wv_ref, wz_ref)):
            wb_ref[k] = w_ref[...].astype(BF16)

    tm = x_ref.shape[0]
    for r in range(tm // EPILOGUE_ROWS):
        rows = slice(r * EPILOGUE_ROWS, (r + 1) * EPILOGUE_ROWS)
        xb = x_ref[rows, :]
        zs_ref[rows, :] = _silu(_dot(xb, wb_ref[3])).astype(BF16)
        q_ref[rows, :] = (_dot(xb, wb_ref[0]) * (HEAD_DIM ** -0.5 * LOG2_E)).astype(BF16)
        k_ref[rows, :] = _dot(xb, wb_ref[1]).astype(BF16)
        v_ref[rows, :] = _dot(xb, wb_ref[2]).astype(BF16)
        if r == 0:
            _side_cast(side_in, side_out)


def _moba_in(xb, w, tm, side_sources):
    n = xb.shape[0]
    tn = MXU_COLS
    nj, ni = WIDTH // tn, n // tm
    out = jax.ShapeDtypeStruct((n, WIDTH), BF16)
    side_in_specs, side_out_specs, side_out_shapes = _side_cast_specs(
        side_sources, nj * ni, lambda j, i: j * ni + i)

    def wspec(k):
        return pl.BlockSpec((D_MODEL, tn), lambda j, i: (0, j + k * nj))

    ospec = pl.BlockSpec((tm, tn), lambda j, i: (i, j))
    return pl.pallas_call(
        partial(_moba_in_kernel, len(side_sources)),
        out_shape=[out, out, out, out] + side_out_shapes,
        grid=(nj, ni),
        in_specs=[pl.BlockSpec((tm, D_MODEL), lambda j, i: (i, 0)),
                  wspec(0), wspec(1), wspec(2), wspec(3)] + side_in_specs,
        out_specs=[ospec, ospec, ospec, ospec] + side_out_specs,
        scratch_shapes=[pltpu.VMEM((4, D_MODEL, tn), BF16)],
        compiler_params=pltpu.CompilerParams(
            dimension_semantics=("arbitrary", "arbitrary"),
            vmem_limit_bytes=VMEM_LIMIT),
        name="moba_in",
    )(xb, w, w, w, w, *[arr for arr, _ in side_sources])


def _moba_attn_kernel(q_ref, k_ref, v_ref, zs_ref, y_ref, vt_ref):
    seq = k_ref.shape[0]
    n_heads = k_ref.shape[1] // HEAD_DIM
    nb = seq // MOBA_BLOCK
    blk = MOBA_BLOCK
    neg_inf = jnp.float32(-jnp.inf)

    key_i = lax.broadcasted_iota(jnp.int32, (blk, blk), 0)
    qry_i = lax.broadcasted_iota(jnp.int32, (blk, blk), 1)
    causal_t = key_i <= qry_i
    blk_i = lax.broadcasted_iota(jnp.int32, (nb, blk), 0)

    def hcols(h):
        return slice(h * HEAD_DIM, (h + 1) * HEAD_DIM)

    def qblock(own, h):
        return q_ref[own * blk:(own + 1) * blk, hcols(h)]

    def scores(own, h):
        q = qblock(own, h)
        return [_dot_nt(k_ref[n * blk:(n + 1) * blk, hcols(h)], q) for n in range(own + 1)]

    def stats(own, h, s_blocks):
        bias = None
        if own > MOBA_TOP_K:
            gate = _dot_nt(k_means[h], qblock(own, h).astype(F32))
            gate = jnp.where(blk_i < own, gate, neg_inf)
            cnt = jnp.zeros((nb, blk), jnp.int32)
            for m in range(own):
                gm = gate[m:m + 1, :]
                beats = (gm > gate) | ((gm == gate) & (m < blk_i))
                cnt = cnt + jnp.where(beats, 1, 0)
            bias = jnp.where((cnt < MOBA_TOP_K) & (blk_i < own), 0.0, neg_inf)

        s_blocks = s_blocks[:own] + [jnp.where(causal_t, s_blocks[own], neg_inf)]
        m_q = jnp.max(s_blocks[own], axis=0, keepdims=True)
        for n in range(own):
            bm = jnp.max(s_blocks[n], axis=0, keepdims=True)
            if bias is not None:
                bm = bm + bias[n:n + 1, :]
            m_q = jnp.maximum(m_q, bm)
        return s_blocks, m_q, bias

    def exp_pv(own, h, s_blocks, m_q, bias):
        acc = jnp.zeros((VT_ROWS, blk), F32)
        for n in range(own + 1):
            shift = m_q
            if bias is not None and n < own:
                shift = m_q - bias[n:n + 1, :]
            pn = jnp.exp2(s_blocks[n] - shift)
            acc = acc + _dot(vt_ref[h, n], pn.astype(BF16))
        o = (acc[0:HEAD_DIM, :] / acc[HEAD_DIM:HEAD_DIM + 1, :]).T
        rows = slice(own * blk, (own + 1) * blk)
        y_ref[rows, hcols(h)] = (o * zs_ref[rows, hcols(h)].astype(F32)).astype(BF16)

    units = [(own, h) for own in reversed(range(nb)) for h in range(n_heads)]
    pending = [scores(*units[0]), scores(*units[1])]

    k_means = []
    for h in range(n_heads):
        for n in range(nb):
            vt_ref[h, n, 0:HEAD_DIM, :] = (
                v_ref[n * blk:(n + 1) * blk, hcols(h)].astype(F32).T.astype(BF16))
            vt_ref[h, n, HEAD_DIM:VT_ROWS, :] = jnp.ones((ONES_ROWS, blk), BF16)
        kf = k_ref[:, hcols(h)].astype(F32)
        k_means.append(jnp.mean(kf.reshape(nb, blk, HEAD_DIM), axis=1))

    st = stats(*units[0], pending.pop(0))
    for i, unit in enumerate(units):
        if i + 2 < len(units):
            pending.append(scores(*units[i + 2]))
        st_next = stats(*units[i + 1], pending.pop(0)) if i + 1 < len(units) else None
        exp_pv(*unit, *st)
        st = st_next


def _moba_attn(q, k, v, zs, batch, seq, heads_per_step):
    spec = pl.BlockSpec((seq, heads_per_step * HEAD_DIM), lambda b, g: (b, g))
    return pl.pallas_call(
        _moba_attn_kernel,
        out_shape=jax.ShapeDtypeStruct((batch * seq, WIDTH), BF16),
        grid=(batch, HEADS // heads_per_step),
        in_specs=[spec, spec, spec, spec],
        out_specs=spec,
        scratch_shapes=[pltpu.VMEM((heads_per_step, seq // MOBA_BLOCK, VT_ROWS, MOBA_BLOCK), BF16)],
        compiler_params=pltpu.CompilerParams(
            dimension_semantics=("arbitrary", "arbitrary"),
            vmem_limit_bytes=VMEM_LIMIT),
        name="moba_attn",
    )(q, k, v, zs)


def kernel(x, p, w_in_a, sgu_norm_g, sgu_norm_b, w_s, b_s, w_in_b, w_out, ln_g, ln_b,
           w_ple_gate, w_ple_proj):
    batch, seq, d = x.shape
    n = batch * seq
    x2 = x.reshape(n, d)
    p2 = p.reshape(DEPTH, n, PLE_DIM)

    y, wo0, wg0, wp_flat = _sgu(
        0, x2, w_in_a, sgu_norm_g, sgu_norm_b, w_s, b_s, tm=SGU_TM,
        side_sources=[(w_out, 0), (w_ple_gate, 0),
                      (w_ple_proj.reshape(DEPTH * PLE_DIM, d), None)])
    wp = wp_flat.reshape(DEPTH, PLE_DIM, d)
    x2, x2b = _tail(0, y, x2, p2, wo0, ln_g, ln_b, wg0, wp, tm=TAIL_TM, with_bf16_copy=True)

    q, k, v, zs, wo1, wg1 = _moba_in(
        x2b, w_in_b[0], tm=MOBA_IN_TM, side_sources=[(w_out, 1), (w_ple_gate, 1)])
    y = _moba_attn(q, k, v, zs, batch, seq, heads_per_step=ATTN_HEADS_PER_STEP)
    (x2,) = _tail(1, y, x2, p2, wo1, ln_g, ln_b, wg1, wp, tm=TAIL_TM, with_bf16_copy=False)

    return x2.reshape(batch, seq, d)
```

```python
from functools import partial

import jax
import jax.numpy as jnp
from jax import lax
from jax.experimental import pallas as pl
from jax.experimental.pallas import tpu as pltpu

D_MODEL = 2048
WIDTH = 2048
SGU_CHUNK = 128
SGU_GROUPS = 16
HEADS = 16
HEAD_DIM = 128
MOBA_BLOCK = 256
MOBA_TOP_K = 3
PLE_DIM = 256
LN_EPS = 1e-5
DEPTH = 2
ALPHA = (2 * DEPTH) ** 0.25
LOG2_E = 1.4426950408889634
MXU_COLS = 256
LANES = 128
EPILOGUE_ROWS = 512
TAIL_ROWS = 256
SGU_ROWS = 512
BF16_ROWS = 16
ONES_ROWS = BF16_ROWS
VT_ROWS = HEAD_DIM + ONES_ROWS

BF16 = jnp.bfloat16
F32 = jnp.float32

VMEM_LIMIT = 56 * 1024 * 1024
SGU_VMEM_LIMIT = 59 * 1024 * 1024
SGU_TM = 512
SGU_W_SLAB = 128
TAIL_TM = 512
MOBA_IN_TM = 2048
ATTN_HEADS_PER_STEP = 2


def _dot(a, b):
    return jnp.dot(a, b, preferred_element_type=F32)


def _dot_nt(a, b):
    return lax.dot_general(a, b, (((1,), (1,)), ((), ())), preferred_element_type=F32)


def _silu(z):
    return z * jax.nn.sigmoid(z)


class _RowStats:
    def __init__(self):
        self.pilot = self.d_acc = self.sq_acc = None

    def add(self, piece):
        def lane_fold(a):
            return a[:, 0:LANES] + a[:, LANES:2 * LANES]

        if self.pilot is None:
            self.pilot = jnp.mean(piece, axis=-1, keepdims=True)
        dc = piece - self.pilot
        d_part, sq_part = lane_fold(dc), lane_fold(dc * dc)
        self.d_acc = d_part if self.d_acc is None else self.d_acc + d_part
        self.sq_acc = sq_part if self.sq_acc is None else self.sq_acc + sq_part

    def mean_rstd(self, width):
        d_mean = jnp.sum(self.d_acc, axis=-1, keepdims=True) * (1.0 / width)
        var = jnp.sum(self.sq_acc, axis=-1, keepdims=True) * (1.0 / width) - d_mean * d_mean
        return self.pilot + d_mean, lax.rsqrt(var + LN_EPS)


def _side_cast_specs(sources, n_steps, step_index):
    in_specs, out_specs, out_shapes = [], [], []
    for arr, layer in sources:
        rows, cols = arr.shape[-2:]
        slab = rows // n_steps
        assert slab * n_steps == rows and slab % BF16_ROWS == 0, (rows, n_steps)
        if layer is None:
            in_specs.append(pl.BlockSpec((slab, cols), lambda *g: (step_index(*g), 0)))
        else:
            in_specs.append(pl.BlockSpec(
                (None, slab, cols), lambda *g, layer=layer: (layer, step_index(*g), 0)))
        out_specs.append(pl.BlockSpec((slab, cols), lambda *g: (step_index(*g), 0)))
        out_shapes.append(jax.ShapeDtypeStruct((rows, cols), BF16))
    return in_specs, out_specs, out_shapes


def _side_cast(side_in_refs, side_out_refs):
    for src_ref, dst_ref in zip(side_in_refs, side_out_refs):
        dst_ref[...] = src_ref[...].astype(BF16)


def _sgu_kernel(n_side, n_load, x_ref, wf_ref, ng_ref, nb_ref, ws_ref, bs_ref, *refs):
    side_in, (y_ref, *side_out), (w_ref, xb_ref, vg_ref, wc_ref, bst_ref) = (
        refs[:n_side], refs[n_side:2 * n_side + 1], refs[2 * n_side + 1:])
    step = pl.program_id(0)

    @pl.when(step < n_load)
    def _():
        slab = wf_ref.shape[0]
        w_ref[pl.ds(pl.multiple_of(step * slab, slab), slab), :] = wf_ref[...].astype(BF16)

    @pl.when(step == 0)
    def _():
        row = lax.broadcasted_iota(jnp.int32, (SGU_CHUNK, SGU_CHUNK), 0)
        col = lax.broadcasted_iota(jnp.int32, (SGU_CHUNK, SGU_CHUNK), 1)
        keep = (col <= row)[None]
        wc_ref[...] = jnp.where(keep, ws_ref[...], 0.0).astype(BF16)
        bst_ref[...] = bs_ref[...].T

    pl.when(step >= n_load)(partial(
        _sgu_tile, x_ref, w_ref, ng_ref, nb_ref, bst_ref, side_in, y_ref, side_out,
        xb_ref, vg_ref, wc_ref))


def _sgu_tile(x_ref, w_ref, ng_ref, nb_ref, bs_ref, side_in, y_ref, side_out,
              xb_ref, vg_ref, wc_ref):
    tm = x_ref.shape[0]
    n_pieces = WIDTH // MXU_COLS
    gw = WIDTH // SGU_GROUPS
    xb_ref[...] = x_ref[...].astype(BF16)

    def in_proj(path, c, epilogue):
        w_cols = slice(path * WIDTH + c * MXU_COLS, path * WIDTH + (c + 1) * MXU_COLS)
        parts = [epilogue(_dot(xb_ref[r * SGU_ROWS:(r + 1) * SGU_ROWS, :], w_ref[:, w_cols]))
                 for r in range(tm // SGU_ROWS)]
        return jnp.concatenate(parts, axis=0)

    stats = _RowStats()
    for c in range(n_pieces):
        vc = in_proj(1, c, jax.nn.gelu)
        vg_ref[:, c * MXU_COLS:(c + 1) * MXU_COLS] = vc
        stats.add(vc)
    _side_cast(side_in, side_out)
    mu, rstd = stats.mean_rstd(WIDTH)

    def uz_piece(c):
        return in_proj(0, c, jax.nn.gelu) * in_proj(2, c, _silu)

    uz_next = uz_piece(0)
    for c in range(n_pieces):
        cols = slice(c * MXU_COLS, (c + 1) * MXU_COLS)
        uz = uz_next
        if c + 1 < n_pieces:
            uz_next = uz_piece(c + 1)
        vn = ((vg_ref[:, cols] - mu) * rstd * ng_ref[:, cols] + nb_ref[:, cols]).astype(BF16)
        for k in range(tm // SGU_CHUNK):
            rows = slice(k * SGU_CHUNK, (k + 1) * SGU_CHUNK)
            for gi in range(MXU_COLS // gw):
                g = c * (MXU_COLS // gw) + gi
                gcols = slice(gi * gw, (gi + 1) * gw)
                s = _dot(wc_ref[g], vn[rows, gcols]) + bs_ref[:, g:g + 1]
                y_ref[rows, c * MXU_COLS + gi * gw:c * MXU_COLS + (gi + 1) * gw] = (
                    uz[rows, gcols] * s).astype(BF16)


def _sgu(layer, x, w, ng, nb, ws, bs, tm, side_sources):
    n = x.shape[0]
    n_load = D_MODEL // SGU_W_SLAB
    tile = lambda s: jnp.maximum(s - n_load, 0)
    row = lambda s: (tile(s), 0)
    side_in_specs, side_out_specs, side_out_shapes = _side_cast_specs(
        side_sources, n // tm, tile)
    return pl.pallas_call(
        partial(_sgu_kernel, len(side_sources), n_load),
        out_shape=[jax.ShapeDtypeStruct((n, WIDTH), BF16)] + side_out_shapes,
        grid=(n_load + n // tm,),
        in_specs=[
            pl.BlockSpec((tm, D_MODEL), row),
            pl.BlockSpec((None, SGU_W_SLAB, 3 * WIDTH),
                         lambda s: (layer, jnp.minimum(s, n_load - 1), 0)),
            pl.BlockSpec((1, WIDTH), lambda s: (layer, 0)),
            pl.BlockSpec((1, WIDTH), lambda s: (layer, 0)),
            pl.BlockSpec((None, SGU_GROUPS, SGU_CHUNK, SGU_CHUNK), lambda s: (layer, 0, 0, 0)),
            pl.BlockSpec((None, SGU_GROUPS, SGU_CHUNK), lambda s: (layer, 0, 0)),
        ] + side_in_specs,
        out_specs=[pl.BlockSpec((tm, WIDTH), row)] + side_out_specs,
        scratch_shapes=[pltpu.VMEM((D_MODEL, 3 * WIDTH), BF16),
                        pltpu.VMEM((tm, D_MODEL), BF16),
                        pltpu.VMEM((tm, WIDTH), F32),
                        pltpu.VMEM((SGU_GROUPS, SGU_CHUNK, SGU_CHUNK), BF16),
                        pltpu.VMEM((SGU_CHUNK, SGU_GROUPS), F32)],
        compiler_params=pltpu.CompilerParams(
            dimension_semantics=("arbitrary",), vmem_limit_bytes=SGU_VMEM_LIMIT),
        name="sgu",
    )(x, w, ng, nb, ws, bs, *[arr for arr, _ in side_sources])


def _tail_kernel(layer, y_ref, x_ref, p_ref, wo_ref, g_ref, b_ref, wg_ref, wp_ref, o_ref,
                 maybe_ob_ref=None):
    tm, d = o_ref.shape
    ln_g, ln_b = g_ref[layer:layer + 1, :], b_ref[layer:layer + 1, :]
    n_pieces = tm // TAIL_ROWS

    def pre_norm(r):
        rows = slice(r * TAIL_ROWS, (r + 1) * TAIL_ROWS)
        stats = _RowStats()
        for c in range(d // MXU_COLS):
            cols = slice(c * MXU_COLS, (c + 1) * MXU_COLS)
            t = ALPHA * x_ref[rows, cols] + _dot(y_ref[rows, :], wo_ref[:, cols])
            o_ref[rows, cols] = t
            stats.add(t)
        return stats.mean_rstd(d)

    mu_rstd = pre_norm(0)
    for r in range(n_pieces):
        rows = slice(r * TAIL_ROWS, (r + 1) * TAIL_ROWS)
        mu, rstd = mu_rstd
        if r + 1 < n_pieces:
            mu_rstd = pre_norm(r + 1)
        xn = (o_ref[rows, :] - mu) * rstd * ln_g + ln_b
        xnb = xn.astype(BF16)
        pb = p_ref[rows, :].astype(BF16)
        for c in range(d // MXU_COLS):
            cols = slice(c * MXU_COLS, (c + 1) * MXU_COLS)
            gate = jax.nn.sigmoid(_dot(xnb, wg_ref[:, cols]))
            proj = _dot(pb, wp_ref[:, cols])
            out = xn[:, cols] + gate * proj
            o_ref[rows, cols] = out
            if maybe_ob_ref is not None:
                maybe_ob_ref[rows, cols] = out.astype(BF16)


def _tail(layer, y, x, p, wo, g, b, wg, wp, tm, with_bf16_copy):
    n = x.shape[0]
    row = lambda i: (i, 0)
    fix = lambda i: (layer, 0, 0)
    fix2 = lambda i: (0, 0)
    once = pl.Buffered(1)
    out_shape = [jax.ShapeDtypeStruct((n, D_MODEL), F32)]
    out_specs = [pl.BlockSpec((tm, D_MODEL), row)]
    if with_bf16_copy:
        out_shape.append(jax.ShapeDtypeStruct((n, D_MODEL), BF16))
        out_specs.append(pl.BlockSpec((tm, D_MODEL), row))
    return pl.pallas_call(
        partial(_tail_kernel, layer),
        out_shape=out_shape,
        grid=(n // tm,),
        in_specs=[
            pl.BlockSpec((tm, WIDTH), row),
            pl.BlockSpec((tm, D_MODEL), row),
            pl.BlockSpec((None, tm, PLE_DIM), lambda i: (layer, i, 0)),
            pl.BlockSpec((WIDTH, D_MODEL), fix2, pipeline_mode=once),
            pl.BlockSpec((DEPTH, D_MODEL), fix2),
            pl.BlockSpec((DEPTH, D_MODEL), fix2),
            pl.BlockSpec((D_MODEL, D_MODEL), fix2, pipeline_mode=once),
            pl.BlockSpec((None, PLE_DIM, D_MODEL), fix, pipeline_mode=once),
        ],
        out_specs=out_specs,
        compiler_params=pltpu.CompilerParams(
            dimension_semantics=("arbitrary",), vmem_limit_bytes=VMEM_LIMIT),
        name="layer_tail",
    )(y, x, p, wo, g, b, wg, wp)


def _moba_in_kernel(n_side, x_ref, wq_ref, wk_ref, wv_ref, wz_ref, *refs):
    side_in, (q_ref, k_ref, v_ref, zs_ref, *side_out), (wb_ref,) = (
        refs[:n_side], refs[n_side:2 * n_side + 4], refs[2 * n_side + 4:])

    @pl.when(pl.program_id(1) == 0)
    def _():
        for k, w_ref in enumerate((wq_ref, wk_ref, wv_ref, wz_ref)):
            wb_ref[k] = w_ref[...].astype(BF16)

    tm = x_ref.shape[0]
    for r in range(tm // EPILOGUE_ROWS):
        rows = slice(r * EPILOGUE_ROWS, (r + 1) * EPILOGUE_ROWS)
        xb = x_ref[rows, :]
        zs_ref[rows, :] = _silu(_dot(xb, wb_ref[3])).astype(BF16)
        q_ref[rows, :] = (_dot(xb, wb_ref[0]) * (HEAD_DIM ** -0.5 * LOG2_E)).astype(BF16)
        k_ref[rows, :] = _dot(xb, wb_ref[1]).astype(BF16)
        v_ref[rows, :] = _dot(xb, wb_ref[2]).astype(BF16)
        if r == 0:
            _side_cast(side_in, side_out)


def _moba_in(xb, w, tm, side_sources):
    n = xb.shape[0]
    tn = MXU_COLS
    nj, ni = WIDTH // tn, n // tm
    out = jax.ShapeDtypeStruct((n, WIDTH), BF16)
    side_in_specs, side_out_specs, side_out_shapes = _side_cast_specs(
        side_sources, nj * ni, lambda j, i: j * ni + i)

    def wspec(k):
        return pl.BlockSpec((D_MODEL, tn), lambda j, i: (0, j + k * nj))

    ospec = pl.BlockSpec((tm, tn), lambda j, i: (i, j))
    return pl.pallas_call(
        partial(_moba_in_kernel, len(side_sources)),
        out_shape=[out, out, out, out] + side_out_shapes,
        grid=(nj, ni),
        in_specs=[pl.BlockSpec((tm, D_MODEL), lambda j, i: (i, 0)),
                  wspec(0), wspec(1), wspec(2), wspec(3)] + side_in_specs,
        out_specs=[ospec, ospec, ospec, ospec] + side_out_specs,
        scratch_shapes=[pltpu.VMEM((4, D_MODEL, tn), BF16)],
        compiler_params=pltpu.CompilerParams(
            dimension_semantics=("arbitrary", "arbitrary"),
            vmem_limit_bytes=VMEM_LIMIT),
        name="moba_in",
    )(xb, w, w, w, w, *[arr for arr, _ in side_sources])


def _moba_attn_kernel(q_ref, k_ref, v_ref, zs_ref, y_ref, vt_ref):
    seq = k_ref.shape[0]
    n_heads = k_ref.shape[1] // HEAD_DIM
    nb = seq // MOBA_BLOCK
    blk = MOBA_BLOCK
    neg_inf = jnp.float32(-jnp.inf)

    key_i = lax.broadcasted_iota(jnp.int32, (blk, blk), 0)
    qry_i = lax.broadcasted_iota(jnp.int32, (blk, blk), 1)
    causal_t = key_i <= qry_i
    blk_i = lax.broadcasted_iota(jnp.int32, (nb, blk), 0)

    def hcols(h):
        return slice(h * HEAD_DIM, (h + 1) * HEAD_DIM)

    def qblock(own, h):
        return q_ref[own * blk:(own + 1) * blk, hcols(h)]

    def scores(own, h):
        q = qblock(own, h)
        return [_dot_nt(k_ref[n * blk:(n + 1) * blk, hcols(h)], q) for n in range(own + 1)]

    def stats(own, h, s_blocks):
        bias = None
        if own > MOBA_TOP_K:
            gate = _dot_nt(k_means[h], qblock(own, h).astype(F32))
            gate = jnp.where(blk_i < own, gate, neg_inf)
            cnt = jnp.zeros((nb, blk), jnp.int32)
            for m in range(own):
                gm = gate[m:m + 1, :]
                beats = (gm > gate) | ((gm == gate) & (m < blk_i))
                cnt = cnt + jnp.where(beats, 1, 0)
            bias = jnp.where((cnt < MOBA_TOP_K) & (blk_i < own), 0.0, neg_inf)

        s_blocks = s_blocks[:own] + [jnp.where(causal_t, s_blocks[own], neg_inf)]
        m_q = jnp.max(s_blocks[own], axis=0, keepdims=True)
        for n in range(own):
            bm = jnp.max(s_blocks[n], axis=0, keepdims=True)
            if bias is not None:
                bm = bm + bias[n:n + 1, :]
            m_q = jnp.maximum(m_q, bm)
        return s_blocks, m_q, bias

    def exp_pv(own, h, s_blocks, m_q, bias):
        acc = jnp.zeros((VT_ROWS, blk), F32)
        for n in range(own + 1):
            shift = m_q
            if bias is not None and n < own:
                shift = m_q - bias[n:n + 1, :]
            pn = jnp.exp2(s_blocks[n] - shift)
            acc = acc + _dot(vt_ref[h, :, n * blk:(n + 1) * blk], pn.astype(BF16))
        o = (acc[0:HEAD_DIM, :] / acc[HEAD_DIM:HEAD_DIM + 1, :]).T
        rows = slice(own * blk, (own + 1) * blk)
        y_ref[rows, hcols(h)] = (o * zs_ref[rows, hcols(h)].astype(F32)).astype(BF16)

    units = [(own, h) for own in reversed(range(nb)) for h in range(n_heads)]
    pending = [scores(*units[0]), scores(*units[1])]

    k_means = []
    for h in range(n_heads):
        for n in range(nb):
            vt_ref[h, 0:HEAD_DIM, n * blk:(n + 1) * blk] = (
                v_ref[n * blk:(n + 1) * blk, hcols(h)].astype(F32).T.astype(BF16))
        vt_ref[h, HEAD_DIM:VT_ROWS, :] = jnp.ones((ONES_ROWS, seq), BF16)
        kf = k_ref[:, hcols(h)].astype(F32)
        k_means.append(jnp.mean(kf.reshape(nb, blk, HEAD_DIM), axis=1))

    st = stats(*units[0], pending.pop(0))
    for i, unit in enumerate(units):
        if i + 2 < len(units):
            pending.append(scores(*units[i + 2]))
        st_next = stats(*units[i + 1], pending.pop(0)) if i + 1 < len(units) else None
        exp_pv(*unit, *st)
        st = st_next


def _moba_attn(q, k, v, zs, batch, seq, heads_per_step):
    spec = pl.BlockSpec((seq, heads_per_step * HEAD_DIM), lambda b, g: (b, g))
    return pl.pallas_call(
        _moba_attn_kernel,
        out_shape=jax.ShapeDtypeStruct((batch * seq, WIDTH), BF16),
        grid=(batch, HEADS // heads_per_step),
        in_specs=[spec, spec, spec, spec],
        out_specs=spec,
        scratch_shapes=[pltpu.VMEM((heads_per_step, VT_ROWS, seq), BF16)],
        compiler_params=pltpu.CompilerParams(
            dimension_semantics=("arbitrary", "arbitrary"),
            vmem_limit_bytes=VMEM_LIMIT),
        name="moba_attn",
    )(q, k, v, zs)


def kernel(x, p, w_in_a, sgu_norm_g, sgu_norm_b, w_s, b_s, w_in_b, w_out, ln_g, ln_b,
           w_ple_gate, w_ple_proj):
    batch, seq, d = x.shape
    n = batch * seq
    x2 = x.reshape(n, d)
    p2 = p.reshape(DEPTH, n, PLE_DIM)

    y, wo0, wg0, wp_flat = _sgu(
        0, x2, w_in_a, sgu_norm_g, sgu_norm_b, w_s, b_s, tm=SGU_TM,
        side_sources=[(w_out, 0), (w_ple_gate, 0),
                      (w_ple_proj.reshape(DEPTH * PLE_DIM, d), None)])
    wp = wp_flat.reshape(DEPTH, PLE_DIM, d)
    x2, x2b = _tail(0, y, x2, p2, wo0, ln_g, ln_b, wg0, wp, tm=TAIL_TM, with_bf16_copy=True)

    q, k, v, zs, wo1, wg1 = _moba_in(
        x2b, w_in_b[0], tm=MOBA_IN_TM, side_sources=[(w_out, 1), (w_ple_gate, 1)])
    y = _moba_attn(q, k, v, zs, batch, seq, heads_per_step=ATTN_HEADS_PER_STEP)
    (x2,) = _tail(1, y, x2, p2, wo1, ln_g, ln_b, wg1, wp, tm=TAIL_TM, with_bf16_copy=False)

    return x2.reshape(batch, seq, d)
```

```python
from functools import partial

import jax
import jax.numpy as jnp
from jax import lax
from jax.experimental import pallas as pl
from jax.experimental.pallas import tpu as pltpu

D_MODEL = 2048
WIDTH = 2048
SGU_CHUNK = 128
SGU_GROUPS = 16
HEADS = 16
HEAD_DIM = 128
MOBA_BLOCK = 256
MOBA_TOP_K = 3
PLE_DIM = 256
LN_EPS = 1e-5
DEPTH = 2
ALPHA = (2 * DEPTH) ** 0.25
LOG2_E = 1.4426950408889634
MXU_COLS = 256
LANES = 128
EPILOGUE_ROWS = 512
TAIL_ROWS = 256
SGU_ROWS = 512
BF16_ROWS = 16
ONES_ROWS = BF16_ROWS
VT_ROWS = HEAD_DIM + ONES_ROWS

BF16 = jnp.bfloat16
F32 = jnp.float32

VMEM_LIMIT = 56 * 1024 * 1024
SGU_VMEM_LIMIT = 59 * 1024 * 1024
SGU_TM = 512
SGU_W_SLAB = 128
TAIL_TM = 512
MOBA_IN_TM = 2048
ATTN_HEADS_PER_STEP = 4


def _dot(a, b):
    return jnp.dot(a, b, preferred_element_type=F32)


def _dot_nt(a, b):
    return lax.dot_general(a, b, (((1,), (1,)), ((), ())), preferred_element_type=F32)


def _silu(z):
    return z * jax.nn.sigmoid(z)


class _RowStats:
    def __init__(self):
        self.pilot = self.d_acc = self.sq_acc = None

    def add(self, piece):
        def lane_fold(a):
            return a[:, 0:LANES] + a[:, LANES:2 * LANES]

        if self.pilot is None:
            self.pilot = jnp.mean(piece, axis=-1, keepdims=True)
        dc = piece - self.pilot
        d_part, sq_part = lane_fold(dc), lane_fold(dc * dc)
        self.d_acc = d_part if self.d_acc is None else self.d_acc + d_part
        self.sq_acc = sq_part if self.sq_acc is None else self.sq_acc + sq_part

    def mean_rstd(self, width):
        d_mean = jnp.sum(self.d_acc, axis=-1, keepdims=True) * (1.0 / width)
        var = jnp.sum(self.sq_acc, axis=-1, keepdims=True) * (1.0 / width) - d_mean * d_mean
        return self.pilot + d_mean, lax.rsqrt(var + LN_EPS)


def _side_cast_specs(sources, n_steps, step_index):
    in_specs, out_specs, out_shapes = [], [], []
    for arr, layer in sources:
        rows, cols = arr.shape[-2:]
        slab = rows // n_steps
        assert slab * n_steps == rows and slab % BF16_ROWS == 0, (rows, n_steps)
        if layer is None:
            in_specs.append(pl.BlockSpec((slab, cols), lambda *g: (step_index(*g), 0)))
        else:
            in_specs.append(pl.BlockSpec(
                (None, slab, cols), lambda *g, layer=layer: (layer, step_index(*g), 0)))
        out_specs.append(pl.BlockSpec((slab, cols), lambda *g: (step_index(*g), 0)))
        out_shapes.append(jax.ShapeDtypeStruct((rows, cols), BF16))
    return in_specs, out_specs, out_shapes


def _side_cast(side_in_refs, side_out_refs):
    for src_ref, dst_ref in zip(side_in_refs, side_out_refs):
        dst_ref[...] = src_ref[...].astype(BF16)


def _sgu_kernel(n_side, n_load, x_ref, wf_ref, ng_ref, nb_ref, ws_ref, bs_ref, *refs):
    side_in, (y_ref, *side_out), (w_ref, xb_ref, vg_ref, wc_ref, bst_ref) = (
        refs[:n_side], refs[n_side:2 * n_side + 1], refs[2 * n_side + 1:])
    step = pl.program_id(0)

    @pl.when(step < n_load)
    def _():
        slab = wf_ref.shape[0]
        w_ref[pl.ds(pl.multiple_of(step * slab, slab), slab), :] = wf_ref[...].astype(BF16)

    @pl.when(step == 0)
    def _():
        row = lax.broadcasted_iota(jnp.int32, (SGU_CHUNK, SGU_CHUNK), 0)
        col = lax.broadcasted_iota(jnp.int32, (SGU_CHUNK, SGU_CHUNK), 1)
        keep = (col <= row)[None]
        wc_ref[...] = jnp.where(keep, ws_ref[...], 0.0).astype(BF16)
        bst_ref[...] = bs_ref[...].T

    pl.when(step >= n_load)(partial(
        _sgu_tile, x_ref, w_ref, ng_ref, nb_ref, bst_ref, side_in, y_ref, side_out,
        xb_ref, vg_ref, wc_ref))


def _sgu_tile(x_ref, w_ref, ng_ref, nb_ref, bs_ref, side_in, y_ref, side_out,
              xb_ref, vg_ref, wc_ref):
    tm = x_ref.shape[0]
    n_pieces = WIDTH // MXU_COLS
    gw = WIDTH // SGU_GROUPS
    xb_ref[...] = x_ref[...].astype(BF16)

    def in_proj(path, c, epilogue):
        w_cols = slice(path * WIDTH + c * MXU_COLS, path * WIDTH + (c + 1) * MXU_COLS)
        parts = [epilogue(_dot(xb_ref[r * SGU_ROWS:(r + 1) * SGU_ROWS, :], w_ref[:, w_cols]))
                 for r in range(tm // SGU_ROWS)]
        return jnp.concatenate(parts, axis=0)

    stats = _RowStats()
    for c in range(n_pieces):
        vc = in_proj(1, c, jax.nn.gelu)
        vg_ref[:, c * MXU_COLS:(c + 1) * MXU_COLS] = vc
        stats.add(vc)
    _side_cast(side_in, side_out)
    mu, rstd = stats.mean_rstd(WIDTH)

    def uz_piece(c):
        return in_proj(0, c, jax.nn.gelu) * in_proj(2, c, _silu)

    uz_next = uz_piece(0)
    for c in range(n_pieces):
        cols = slice(c * MXU_COLS, (c + 1) * MXU_COLS)
        uz = uz_next
        if c + 1 < n_pieces:
            uz_next = uz_piece(c + 1)
        vn = ((vg_ref[:, cols] - mu) * rstd * ng_ref[:, cols] + nb_ref[:, cols]).astype(BF16)
        for k in range(tm // SGU_CHUNK):
            rows = slice(k * SGU_CHUNK, (k + 1) * SGU_CHUNK)
            for gi in range(MXU_COLS // gw):
                g = c * (MXU_COLS // gw) + gi
                gcols = slice(gi * gw, (gi + 1) * gw)
                s = _dot(wc_ref[g], vn[rows, gcols]) + bs_ref[:, g:g + 1]
                y_ref[rows, c * MXU_COLS + gi * gw:c * MXU_COLS + (gi + 1) * gw] = (
                    uz[rows, gcols] * s).astype(BF16)


def _sgu(layer, x, w, ng, nb, ws, bs, tm, side_sources):
    n = x.shape[0]
    n_load = D_MODEL // SGU_W_SLAB
    tile = lambda s: jnp.maximum(s - n_load, 0)
    row = lambda s: (tile(s), 0)
    side_in_specs, side_out_specs, side_out_shapes = _side_cast_specs(
        side_sources, n // tm, tile)
    return pl.pallas_call(
        partial(_sgu_kernel, len(side_sources), n_load),
        out_shape=[jax.ShapeDtypeStruct((n, WIDTH), BF16)] + side_out_shapes,
        grid=(n_load + n // tm,),
        in_specs=[
            pl.BlockSpec((tm, D_MODEL), row),
            pl.BlockSpec((None, SGU_W_SLAB, 3 * WIDTH),
                         lambda s: (layer, jnp.minimum(s, n_load - 1), 0)),
            pl.BlockSpec((1, WIDTH), lambda s: (layer, 0)),
            pl.BlockSpec((1, WIDTH), lambda s: (layer, 0)),
            pl.BlockSpec((None, SGU_GROUPS, SGU_CHUNK, SGU_CHUNK), lambda s: (layer, 0, 0, 0)),
            pl.BlockSpec((None, SGU_GROUPS, SGU_CHUNK), lambda s: (layer, 0, 0)),
        ] + side_in_specs,
        out_specs=[pl.BlockSpec((tm, WIDTH), row)] + side_out_specs,
        scratch_shapes=[pltpu.VMEM((D_MODEL, 3 * WIDTH), BF16),
                        pltpu.VMEM((tm, D_MODEL), BF16),
                        pltpu.VMEM((tm, WIDTH), F32),
                        pltpu.VMEM((SGU_GROUPS, SGU_CHUNK, SGU_CHUNK), BF16),
                        pltpu.VMEM((SGU_CHUNK, SGU_GROUPS), F32)],
        compiler_params=pltpu.CompilerParams(
            dimension_semantics=("arbitrary",), vmem_limit_bytes=SGU_VMEM_LIMIT),
        name="sgu",
    )(x, w, ng, nb, ws, bs, *[arr for arr, _ in side_sources])


def _tail_kernel(layer, y_ref, x_ref, p_ref, wo_ref, g_ref, b_ref, wg_ref, wp_ref, o_ref,
                 maybe_ob_ref=None):
    tm, d = o_ref.shape
    ln_g, ln_b = g_ref[layer:layer + 1, :], b_ref[layer:layer + 1, :]
    n_pieces = tm // TAIL_ROWS

    def pre_norm(r):
        rows = slice(r * TAIL_ROWS, (r + 1) * TAIL_ROWS)
        stats = _RowStats()
        for c in range(d // MXU_COLS):
            cols = slice(c * MXU_COLS, (c + 1) * MXU_COLS)
            t = ALPHA * x_ref[rows, cols] + _dot(y_ref[rows, :], wo_ref[:, cols])
            o_ref[rows, cols] = t
            stats.add(t)
        return stats.mean_rstd(d)

    mu_rstd = pre_norm(0)
    for r in range(n_pieces):
        rows = slice(r * TAIL_ROWS, (r + 1) * TAIL_ROWS)
        mu, rstd = mu_rstd
        if r + 1 < n_pieces:
            mu_rstd = pre_norm(r + 1)
        xnb = ((o_ref[rows, :] - mu) * rstd * ln_g + ln_b).astype(BF16)
        pb = p_ref[rows, :].astype(BF16)
        for c in range(d // MXU_COLS):
            cols = slice(c * MXU_COLS, (c + 1) * MXU_COLS)
            gate = jax.nn.sigmoid(_dot(xnb, wg_ref[:, cols]))
            proj = _dot(pb, wp_ref[:, cols])
            xn_c = (o_ref[rows, cols] - mu) * rstd * ln_g[:, cols] + ln_b[:, cols]
            out = xn_c + gate * proj
            o_ref[rows, cols] = out
            if maybe_ob_ref is not None:
                maybe_ob_ref[rows, cols] = out.astype(BF16)


def _tail(layer, y, x, p, wo, g, b, wg, wp, tm, with_bf16_copy):
    n = x.shape[0]
    row = lambda i: (i, 0)
    fix = lambda i: (layer, 0, 0)
    fix2 = lambda i: (0, 0)
    once = pl.Buffered(1)
    out_shape = [jax.ShapeDtypeStruct((n, D_MODEL), F32)]
    out_specs = [pl.BlockSpec((tm, D_MODEL), row)]
    if with_bf16_copy:
        out_shape.append(jax.ShapeDtypeStruct((n, D_MODEL), BF16))
        out_specs.append(pl.BlockSpec((tm, D_MODEL), row))
    return pl.pallas_call(
        partial(_tail_kernel, layer),
        out_shape=out_shape,
        grid=(n // tm,),
        in_specs=[
            pl.BlockSpec((tm, WIDTH), row),
            pl.BlockSpec((tm, D_MODEL), row),
            pl.BlockSpec((None, tm, PLE_DIM), lambda i: (layer, i, 0)),
            pl.BlockSpec((WIDTH, D_MODEL), fix2, pipeline_mode=once),
            pl.BlockSpec((DEPTH, D_MODEL), fix2),
            pl.BlockSpec((DEPTH, D_MODEL), fix2),
            pl.BlockSpec((D_MODEL, D_MODEL), fix2, pipeline_mode=once),
            pl.BlockSpec((None, PLE_DIM, D_MODEL), fix, pipeline_mode=once),
        ],
        out_specs=out_specs,
        compiler_params=pltpu.CompilerParams(
            dimension_semantics=("arbitrary",), vmem_limit_bytes=VMEM_LIMIT),
        name="layer_tail",
    )(y, x, p, wo, g, b, wg, wp)


def _moba_in_kernel(n_side, x_ref, wq_ref, wk_ref, wv_ref, wz_ref, *refs):
    side_in, (q_ref, k_ref, v_ref, zs_ref, *side_out), (wb_ref,) = (
        refs[:n_side], refs[n_side:2 * n_side + 4], refs[2 * n_side + 4:])

    @pl.when(pl.program_id(1) == 0)
    def _():
        for k, w_ref in enumerate((wq_ref, wk_ref, wv_ref, wz_ref)):
            wb_ref[k] = w_ref[...].astype(BF16)

    tm = x_ref.shape[0]
    for r in range(tm // EPILOGUE_ROWS):
        rows = slice(r * EPILOGUE_ROWS, (r + 1) * EPILOGUE_ROWS)
        xb = x_ref[rows, :]
        zs_ref[rows, :] = _silu(_dot(xb, wb_ref[3])).astype(BF16)
        q_ref[rows, :] = (_dot(xb, wb_ref[0]) * (HEAD_DIM ** -0.5 * LOG2_E)).astype(BF16)
        k_ref[rows, :] = _dot(xb, wb_ref[1]).astype(BF16)
        v_ref[rows, :] = _dot(xb, wb_ref[2]).astype(BF16)
        if r == 0:
            _side_cast(side_in, side_out)


def _moba_in(xb, w, tm, side_sources):
    n = xb.shape[0]
    tn = MXU_COLS
    nj, ni = WIDTH // tn, n // tm
    out = jax.ShapeDtypeStruct((n, WIDTH), BF16)
    side_in_specs, side_out_specs, side_out_shapes = _side_cast_specs(
        side_sources, nj * ni, lambda j, i: j * ni + i)

    def wspec(k):
        return pl.BlockSpec((D_MODEL, tn), lambda j, i: (0, j + k * nj))

    ospec = pl.BlockSpec((tm, tn), lambda j, i: (i, j))
    return pl.pallas_call(
        partial(_moba_in_kernel, len(side_sources)),
        out_shape=[out, out, out, out] + side_out_shapes,
        grid=(nj, ni),
        in_specs=[pl.BlockSpec((tm, D_MODEL), lambda j, i: (i, 0)),
                  wspec(0), wspec(1), wspec(2), wspec(3)] + side_in_specs,
        out_specs=[ospec, ospec, ospec, ospec] + side_out_specs,
        scratch_shapes=[pltpu.VMEM((4, D_MODEL, tn), BF16)],
        compiler_params=pltpu.CompilerParams(
            dimension_semantics=("arbitrary", "arbitrary"),
            vmem_limit_bytes=VMEM_LIMIT),
        name="moba_in",
    )(xb, w, w, w, w, *[arr for arr, _ in side_sources])


def _moba_attn_kernel(q_ref, k_ref, v_ref, zs_ref, y_ref, vt_ref):
    seq = k_ref.shape[0]
    n_heads = k_ref.shape[1] // HEAD_DIM
    nb = seq // MOBA_BLOCK
    blk = MOBA_BLOCK
    neg_inf = jnp.float32(-jnp.inf)

    key_i = lax.broadcasted_iota(jnp.int32, (blk, blk), 0)
    qry_i = lax.broadcasted_iota(jnp.int32, (blk, blk), 1)
    causal_t = key_i <= qry_i
    blk_i = lax.broadcasted_iota(jnp.int32, (nb, blk), 0)

    def hcols(h):
        return slice(h * HEAD_DIM, (h + 1) * HEAD_DIM)

    def qblock(own, h):
        return q_ref[own * blk:(own + 1) * blk, hcols(h)]

    def scores(own, h):
        q = qblock(own, h)
        return [_dot_nt(k_ref[n * blk:(n + 1) * blk, hcols(h)], q) for n in range(own + 1)]

    def stats(own, h, s_blocks):
        bias = None
        if own > MOBA_TOP_K:
            gate = _dot_nt(k_means[h], qblock(own, h).astype(F32))
            gate = jnp.where(blk_i < own, gate, neg_inf)
            cnt = jnp.zeros((nb, blk), jnp.int32)
            for m in range(own):
                gm = gate[m:m + 1, :]
                beats = (gm > gate) | ((gm == gate) & (m < blk_i))
                cnt = cnt + jnp.where(beats, 1, 0)
            bias = jnp.where((cnt < MOBA_TOP_K) & (blk_i < own), 0.0, neg_inf)

        s_blocks = s_blocks[:own] + [jnp.where(causal_t, s_blocks[own], neg_inf)]
        m_q = jnp.max(s_blocks[own], axis=0, keepdims=True)
        for n in range(own):
            bm = jnp.max(s_blocks[n], axis=0, keepdims=True)
            if bias is not None:
                bm = bm + bias[n:n + 1, :]
            m_q = jnp.maximum(m_q, bm)
        return s_blocks, m_q, bias

    def exp_pv(own, h, s_blocks, m_q, bias):
        acc = jnp.zeros((VT_ROWS, blk), F32)
        for n in range(own + 1):
            shift = m_q
            if bias is not None and n < own:
                shift = m_q - bias[n:n + 1, :]
            pn = jnp.exp2(s_blocks[n] - shift)
            acc = acc + _dot(vt_ref[h, :, n * blk:(n + 1) * blk], pn.astype(BF16))
        o = (acc[0:HEAD_DIM, :] / acc[HEAD_DIM:HEAD_DIM + 1, :]).T
        rows = slice(own * blk, (own + 1) * blk)
        y_ref[rows, hcols(h)] = (o * zs_ref[rows, hcols(h)].astype(F32)).astype(BF16)

    units = [(own, h) for own in reversed(range(nb)) for h in range(n_heads)]
    pending = [scores(*units[0]), scores(*units[1])]

    k_means = []
    for h in range(n_heads):
        for n in range(nb):
            vt_ref[h, 0:HEAD_DIM, n * blk:(n + 1) * blk] = (
                v_ref[n * blk:(n + 1) * blk, hcols(h)].astype(F32).T.astype(BF16))
        vt_ref[h, HEAD_DIM:VT_ROWS, :] = jnp.ones((ONES_ROWS, seq), BF16)
        kf = k_ref[:, hcols(h)].astype(F32)
        k_means.append(jnp.mean(kf.reshape(nb, blk, HEAD_DIM), axis=1))

    st = stats(*units[0], pending.pop(0))
    for i, unit in enumerate(units):
        if i + 2 < len(units):
            pending.append(scores(*units[i + 2]))
        st_next = stats(*units[i + 1], pending.pop(0)) if i + 1 < len(units) else None
        exp_pv(*unit, *st)
        st = st_next


def _moba_attn(q, k, v, zs, batch, seq, heads_per_step):
    spec = pl.BlockSpec((seq, heads_per_step * HEAD_DIM), lambda b, g: (b, g))
    return pl.pallas_call(
        _moba_attn_kernel,
        out_shape=jax.ShapeDtypeStruct((batch * seq, WIDTH), BF16),
        grid=(batch, HEADS // heads_per_step),
        in_specs=[spec, spec, spec, spec],
        out_specs=spec,
        scratch_shapes=[pltpu.VMEM((heads_per_step, VT_ROWS, seq), BF16)],
        compiler_params=pltpu.CompilerParams(
            dimension_semantics=("arbitrary", "arbitrary"),
            vmem_limit_bytes=VMEM_LIMIT),
        name="moba_attn",
    )(q, k, v, zs)


def kernel(x, p, w_in_a, sgu_norm_g, sgu_norm_b, w_s, b_s, w_in_b, w_out, ln_g, ln_b,
           w_ple_gate, w_ple_proj):
    batch, seq, d = x.shape
    n = batch * seq
    x2 = x.reshape(n, d)
    p2 = p.reshape(DEPTH, n, PLE_DIM)

    y, wo0, wg0, wp_flat = _sgu(
        0, x2, w_in_a, sgu_norm_g, sgu_norm_b, w_s, b_s, tm=SGU_TM,
        side_sources=[(w_out, 0), (w_ple_gate, 0),
                      (w_ple_proj.reshape(DEPTH * PLE_DIM, d), None)])
    wp = wp_flat.reshape(DEPTH, PLE_DIM, d)
    x2, x2b = _tail(0, y, x2, p2, wo0, ln_g, ln_b, wg0, wp, tm=TAIL_TM, with_bf16_copy=True)

    q, k, v, zs, wo1, wg1 = _moba_in(
        x2b, w_in_b[0], tm=MOBA_IN_TM, side_sources=[(w_out, 1), (w_ple_gate, 1)])
    y = _moba_attn(q, k, v, zs, batch, seq, heads_per_step=ATTN_HEADS_PER_STEP)
    (x2,) = _tail(1, y, x2, p2, wo1, ln_g, ln_b, wg1, wp, tm=TAIL_TM, with_bf16_copy=False)

    return x2.reshape(batch, seq, d)
```
